```python
import jax, jax.numpy as jnp
from jax import lax
import numpy as np

D_MODEL = 1024
BATCH = 8
SEQ = 2048
DEPTH = 1
DEC_BATCH = 128
DEC_SEQ = 1
PAST_LEN = 8192
PAGE_SIZE = 128

A_HEADS = 8
A_HEAD_DIM = 64
IDX_HEADS = 8
IDX_DIM = 32
IDX_TOPK_MAX = 256
B_HEADS = 8
B_NOPE = 64
B_ROPE = 32
B_V = 64
KV_LORA = 256
Q_LORA = 384
N_KEYS = 128
N_EXPERTS = N_KEYS * N_KEYS
PEER_HEADS = 8
PEER_DK = 128
PEER_TOPK = 16
ROPE_THETA = 10000.0
EPS = 1e-6
MLA_QBLOCK = 128
DSA_QBLOCK = 64
PEER_BLOCK = 128
IN_SIZES = (A_HEADS * A_HEAD_DIM, A_HEADS * A_HEAD_DIM, A_HEADS * A_HEAD_DIM,
            IDX_HEADS * IDX_DIM, IDX_HEADS, IDX_DIM,
            Q_LORA, KV_LORA, B_ROPE,
            D_MODEL, D_MODEL)
IN_WIDTH = sum(IN_SIZES)

kernel_name = "hybrid_dsa_mla_peer_decode_step"


def _rms(x, g):
    xf = x.astype(jnp.float32)
    y = xf * lax.rsqrt(jnp.mean(xf * xf, axis=-1, keepdims=True) + EPS)
    return (y * g.astype(jnp.float32)).astype(x.dtype)


def _rope(x, pos):
    d = x.shape[-1]
    half = d // 2
    inv = ROPE_THETA ** (-(jnp.arange(half, dtype=jnp.float32) * 2.0 / d))
    ang = pos.astype(jnp.float32)[:, None] * inv[None, :]
    cos = jnp.cos(ang)[:, None, :]
    sin = jnp.sin(ang)[:, None, :]
    xf = x.astype(jnp.float32)
    x1, x2 = xf[..., :half], xf[..., half:]
    return jnp.concatenate([x1 * cos - x2 * sin, x2 * cos + x1 * sin], axis=-1).astype(x.dtype)


def _block_size(t, pref):
    return pref if t % pref == 0 else t


def _map_query_blocks(fn, blk, pos, *xs):
    t = pos.shape[0]
    nb = t // blk
    xs_b = tuple(jnp.moveaxis(x.reshape(x.shape[0], nb, blk, *x.shape[2:]), 1, 0) for x in xs)
    out = lax.map(lambda a: fn(a[0], *a[1]), (pos.reshape(nb, blk), xs_b))
    out = jnp.moveaxis(out, 0, 1)
    return out.reshape(out.shape[0], t, *out.shape[3:])


def _mixer_inputs(h, pos, w_in, g_qa, g_ka, g_cq, w_uq, g_qb, w_uk, g_ckv, g_kpe):
    b, t, _ = h.shape
    z = h @ w_in
    parts = []
    off = 0
    for n in IN_SIZES:
        parts.append(z[..., off:off + n])
        off += n
    q_a, k_a, v_a, q_i, w_i, k_i, c_q, c_kv, k_pe, gz_a, gz_b = parts
    q_a = _rope(_rms(q_a.reshape(b, t, A_HEADS, A_HEAD_DIM), g_qa), pos)
    k_a = _rope(_rms(k_a.reshape(b, t, A_HEADS, A_HEAD_DIM), g_ka), pos)
    v_a = v_a.reshape(b, t, A_HEADS, A_HEAD_DIM)
    q_i = _rope(q_i.reshape(b, t, IDX_HEADS, IDX_DIM), pos)
    k_i = _rope(k_i[:, :, None, :], pos)[:, :, 0, :]
    q_b = jnp.einsum('btr,rhe->bthe', _rms(c_q, g_cq), w_uq)
    q_b = _rms(q_b, g_qb)
    q_abs = jnp.einsum('bthn,hnc->bthc', q_b[..., :B_NOPE], w_uk)
    q_pe = _rope(q_b[..., B_NOPE:], pos)
    ckv = _rms(c_kv, g_ckv)
    kpe = _rope(_rms(k_pe, g_kpe)[:, :, None, :], pos)[:, :, 0, :]
    return (q_a, k_a, v_a, q_i, w_i, k_i, q_abs, q_pe, ckv, kpe,
            jax.nn.sigmoid(gz_a), jax.nn.sigmoid(gz_b))


def _dsa_attend(q, q_i, w_i, q_pos, k_i_all, k_pos, gather_kv):
    topk = min(IDX_TOPK_MAX, k_i_all.shape[1] // 4)

    def block(qp, qb, qib, wib):
        s = jnp.einsum('bthd,bsd->btsh', qib, k_i_all, preferred_element_type=jnp.float32) * (IDX_DIM ** -0.5)
        score = jnp.einsum('btsh,bth->bts', jax.nn.relu(s), wib.astype(jnp.float32)) * (IDX_HEADS ** -0.5)
        causal = k_pos[None, :] <= qp[:, None]
        score = jnp.where(causal[None], score, -jnp.inf)
        top_v, top_i = lax.top_k(score, topk)
        valid = jnp.isfinite(top_v)
        k_sel, v_sel = gather_kv(top_i)
        logits = jnp.einsum('bthd,btkhd->bthk', qb, k_sel, preferred_element_type=jnp.float32) * (A_HEAD_DIM ** -0.5)
        logits = jnp.where(valid[:, :, None, :], logits, -jnp.inf)
        p = jax.nn.softmax(logits, axis=-1).astype(v_sel.dtype)
        return jnp.einsum('bthk,btkhd->bthd', p, v_sel)

    return _map_query_blocks(block, _block_size(q.shape[1], DSA_QBLOCK), q_pos, q, q_i, w_i)


def _mla_attend(q_abs, q_pe, q_pos, ckv_all, kpe_all, k_pos, w_uv):
    scale = (B_NOPE + B_ROPE) ** -0.5

    def block(qp, qa, qr):
        logits = (jnp.einsum('bthc,bsc->bhts', qa, ckv_all, preferred_element_type=jnp.float32)
                  + jnp.einsum('bthr,bsr->bhts', qr, kpe_all, preferred_element_type=jnp.float32)) * scale
        causal = k_pos[None, :] <= qp[:, None]
        logits = jnp.where(causal[None, None], logits, -jnp.inf)
        p = jax.nn.softmax(logits, axis=-1).astype(ckv_all.dtype)
        return jnp.einsum('bhts,bsc->bthc', p, ckv_all)

    o_lat = _map_query_blocks(block, _block_size(q_abs.shape[1], MLA_QBLOCK), q_pos, q_abs, q_pe)
    return jnp.einsum('bthc,hcv->bthv', o_lat, w_uv)


def _peer(h, w_pq, sub_k1, sub_k2, u_tab, v_tab):
    b, t, d = h.shape
    n = b * t
    pad = (-n) % PEER_BLOCK
    xf = jnp.pad(h.reshape(n, d), ((0, pad), (0, 0)))
    half = PEER_DK // 2

    def body(xb):
        q = (xb @ w_pq).reshape(xb.shape[0], PEER_HEADS, PEER_DK)
        s1 = jnp.einsum('nhd,hed->nhe', q[..., :half], sub_k1, preferred_element_type=jnp.float32)
        s2 = jnp.einsum('nhd,hed->nhe', q[..., half:], sub_k2, preferred_element_type=jnp.float32)
        v1, i1 = lax.top_k(s1, PEER_TOPK)
        v2, i2 = lax.top_k(s2, PEER_TOPK)
        cand = (v1[..., :, None] + v2[..., None, :]).reshape(xb.shape[0], PEER_HEADS, PEER_TOPK * PEER_TOPK)
        cidx = (i1[..., :, None] * N_KEYS + i2[..., None, :]).reshape(xb.shape[0], PEER_HEADS, PEER_TOPK * PEER_TOPK)
        sv, si = lax.top_k(cand, PEER_TOPK)
        eidx = jnp.take_along_axis(cidx, si, axis=-1)
        g = jax.nn.softmax(sv, axis=-1)
        u = u_tab[eidx]
        act = jax.nn.gelu(jnp.einsum('nd,nhkd->nhk', xb, u, preferred_element_type=jnp.float32), approximate=False)
        vv = v_tab[eidx]
        return jnp.einsum('nhk,nhkd->nd', (g * act).astype(vv.dtype), vv)

    out = lax.map(body, xf.reshape(-1, PEER_BLOCK, d)).reshape(-1, d)[:n]
    return out.reshape(b, t, d)


def _finish(x, o_a, o_b, gate_a, gate_b, w_pa, w_pb, w_o, g_ffn, w_pq, sub_k1, sub_k2, u_tab, v_tab):
    b, t = x.shape[:2]
    merged = gate_a * (o_a.reshape(b, t, -1) @ w_pa) + gate_b * (o_b.reshape(b, t, -1) @ w_pb)
    x1 = x + merged @ w_o
    return x1 + _peer(_rms(x1, g_ffn), w_pq, sub_k1, sub_k2, u_tab, v_tab)


def setup_inputs(seed: int = 0) -> dict:
    key = jax.random.key(seed)
    ks = jax.random.split(key, 40)
    f32 = jnp.float32
    n_pages = PAST_LEN // PAGE_SIZE
    n_used = DEC_BATCH * n_pages
    n_pool = n_used + n_used // 4
    nrm = lambda k, shape, s: jax.random.normal(k, shape, f32) * s
    gain = lambda k, n: 1.0 + 0.1 * jax.random.normal(k, (n,), f32)
    page_table = jax.random.permutation(ks[7], n_pool)[:n_used].reshape(DEC_BATCH, n_pages).astype(jnp.int32)
    return {
        "x_prompt": nrm(ks[0], (BATCH, SEQ, D_MODEL), 1.0),
        "x_sample": nrm(ks[1], (DEC_BATCH, DEC_SEQ, D_MODEL), 1.0),
        "cache_k": nrm(ks[2], (n_pool, PAGE_SIZE, A_HEADS, A_HEAD_DIM), 1.0),
        "cache_v": nrm(ks[3], (n_pool, PAGE_SIZE, A_HEADS, A_HEAD_DIM), 1.0),
        "cache_idx_k": nrm(ks[4], (n_pool, PAGE_SIZE, IDX_DIM), 1.0),
        "cache_ckv": nrm(ks[5], (n_pool, PAGE_SIZE, KV_LORA), 1.0),
        "cache_kpe": nrm(ks[6], (n_pool, PAGE_SIZE, B_ROPE), 1.0),
        "page_table": page_table,
        "g_attn": gain(ks[8], D_MODEL),
        "w_in": nrm(ks[9], (D_MODEL, IN_WIDTH), D_MODEL ** -0.5),
        "g_qa": gain(ks[10], A_HEAD_DIM),
        "g_ka": gain(ks[11], A_HEAD_DIM),
        "g_cq": gain(ks[12], Q_LORA),
        "w_uq": nrm(ks[13], (Q_LORA, B_HEADS, B_NOPE + B_ROPE), Q_LORA ** -0.5),
        "g_qb": gain(ks[14], B_NOPE + B_ROPE),
        "w_uk": nrm(ks[15], (B_HEADS, B_NOPE, KV_LORA), KV_LORA ** -0.5),
        "g_ckv": gain(ks[16], KV_LORA),
        "g_kpe": gain(ks[17], B_ROPE),
        "w_uv": nrm(ks[18], (B_HEADS, KV_LORA, B_V), KV_LORA ** -0.5),
        "w_pa": nrm(ks[19], (A_HEADS * A_HEAD_DIM, D_MODEL), (A_HEADS * A_HEAD_DIM) ** -0.5),
        "w_pb": nrm(ks[20], (B_HEADS * B_V, D_MODEL), (B_HEADS * B_V) ** -0.5),
        "w_o": nrm(ks[21], (D_MODEL, D_MODEL), D_MODEL ** -0.5),
        "g_ffn": gain(ks[22], D_MODEL),
        "w_pq": nrm(ks[23], (D_MODEL, PEER_HEADS * PEER_DK), D_MODEL ** -0.5),
        "sub_k1": nrm(ks[24], (PEER_HEADS, N_KEYS, PEER_DK // 2), (PEER_DK // 2) ** -0.5),
        "sub_k2": nrm(ks[25], (PEER_HEADS, N_KEYS, PEER_DK // 2), (PEER_DK // 2) ** -0.5),
        "u_tab": nrm(ks[26], (N_EXPERTS, D_MODEL), D_MODEL ** -0.5),
        "v_tab": nrm(ks[27], (N_EXPERTS, D_MODEL), (PEER_HEADS * PEER_TOPK) ** -0.5),
    }


def reference(x_prompt, x_sample, cache_k, cache_v, cache_idx_k, cache_ckv, cache_kpe, page_table,
              g_attn, w_in, g_qa, g_ka, g_cq, w_uq, g_qb, w_uk, g_ckv, g_kpe, w_uv,
              w_pa, w_pb, w_o, g_ffn, w_pq, sub_k1, sub_k2, u_tab, v_tab):
    proj_w = (w_in, g_qa, g_ka, g_cq, w_uq, g_qb, w_uk, g_ckv, g_kpe)
    tail_w = (w_pa, w_pb, w_o, g_ffn, w_pq, sub_k1, sub_k2, u_tab, v_tab)

    bp, sp = x_prompt.shape[:2]
    pos_p = jnp.arange(sp, dtype=jnp.int32)
    (qa_p, ka_p, va_p, qi_p, wi_p, ki_p, qabs_p, qpe_p, ckv_p, kpe_p,
     ga_p, gb_p) = _mixer_inputs(_rms(x_prompt, g_attn), pos_p, *proj_w)
    bi_p = jnp.arange(bp)[:, None, None]

    def gather_prompt(idx):
        return ka_p[bi_p, idx], va_p[bi_p, idx]

    oa_p = _dsa_attend(qa_p, qi_p, wi_p, pos_p, ki_p, pos_p, gather_prompt)
    ob_p = _mla_attend(qabs_p, qpe_p, pos_p, ckv_p, kpe_p, pos_p, w_uv)
    y_prompt = _finish(x_prompt, oa_p, ob_p, ga_p, gb_p, *tail_w)

    bs, ts = x_sample.shape[:2]
    page = cache_k.shape[1]
    n_pages = page_table.shape[1]
    past = n_pages * page
    pos_s = past + jnp.arange(ts, dtype=jnp.int32)
    k_pos = jnp.arange(past + ts, dtype=jnp.int32)
    (qa_s, ka_s, va_s, qi_s, wi_s, ki_s, qabs_s, qpe_s, ckv_s, kpe_s,
     ga_s, gb_s) = _mixer_inputs(_rms(x_sample, g_attn), pos_s, *proj_w)
    bi_s = jnp.arange(bs)[:, None, None]

    def gather_sample(idx):
        p = jnp.minimum(idx, past - 1)
        phys = page_table[bi_s, p // page]
        off = p % page
        j = jnp.clip(idx - past, 0, ts - 1)
        in_past = (idx < past)[..., None, None]
        k_sel = jnp.where(in_past, cache_k[phys, off], ka_s[bi_s, j])
        v_sel = jnp.where(in_past, cache_v[phys, off], va_s[bi_s, j])
        return k_sel, v_sel

    ki_all = jnp.concatenate([cache_idx_k[page_table].reshape(bs, past, IDX_DIM), ki_s], axis=1)
    ckv_all = jnp.concatenate([cache_ckv[page_table].reshape(bs, past, KV_LORA), ckv_s], axis=1)
    kpe_all = jnp.concatenate([cache_kpe[page_table].reshape(bs, past, B_ROPE), kpe_s], axis=1)
    oa_s = _dsa_attend(qa_s, qi_s, wi_s, pos_s, ki_all, k_pos, gather_sample)
    ob_s = _mla_attend(qabs_s, qpe_s, pos_s, ckv_all, kpe_all, k_pos, w_uv)
    y_sample = _finish(x_sample, oa_s, ob_s, ga_s, gb_s, *tail_w)

    return (y_prompt, y_sample, ka_p, va_p, ki_p, ckv_p, kpe_p, ka_s, va_s, ki_s, ckv_s, kpe_s)
```

```python
import functools
import math

import numpy as np
import jax
import jax.numpy as jnp
from jax import lax
from jax.experimental import pallas as pl
from jax.experimental.pallas import tpu as pltpu

F32 = jnp.float32
BF16 = jnp.bfloat16
I32 = jnp.int32

ROPE_THETA = 10000.0
EPS = 1e-6
IDX_TOPK_MAX = 256
PEER_TOPK = 16
LANES = 128
NEG_INF = float("-inf")
VMEM_LIMIT = 56 * 1024 * 1024


def _cparams(sem):
    return pltpu.CompilerParams(dimension_semantics=sem, vmem_limit_bytes=VMEM_LIMIT)


def _dot(a, b):
    return jnp.dot(a, b, preferred_element_type=F32)


def _dot_nt(a, b):
    return lax.dot_general(a, b, (((1,), (1,)), ((), ())), preferred_element_type=F32)


def _dot_hilo(a, b):
    hi = a.astype(BF16)
    lo = (a - hi.astype(F32)).astype(BF16)
    return _dot(hi, b) + _dot(lo, b)


def _rope(x, c, s, half):
    w = x.shape[-1]
    fwd = pltpu.roll(x, half, 1)
    bwd = pltpu.roll(x, w - half, 1)
    lane = lax.broadcasted_iota(I32, (1, w), 1)
    first = (lane & (2 * half - 1)) < half
    return x * c + jnp.where(first, bwd, fwd) * s


def _group_masked(x, h, d):
    per = LANES // d
    g = h // per
    grp = x[:, g * LANES:(g + 1) * LANES]
    lane = lax.broadcasted_iota(I32, (1, LANES), 1)
    return jnp.where((lane // d) == (h % per), grp, 0.0)


def _proj_kernel(dm, x_ref, gat_ref, w_ref, wwi_ref, c64_ref, s64_ref, c32_ref, s32_ref,
                 gqa_ref, gka_ref, gcq_ref, gqbn_ref, gqbr_ref, gckv_ref, gkpe_ref,
                 bda_ref, bnr_ref, brn_ref, bdr_ref, wuq_ref, wuk_ref,
                 ka32, va32, ki32, ckv32, kpe32, ga, gb, ka16, va16, ki16, kcat16,
                 qam, qim, wit, qcat):
    off = dm["off"]
    x = x_ref[...]
    ms = jnp.mean(x * x, axis=-1, keepdims=True)
    h = (x * lax.rsqrt(ms + EPS) * gat_ref[...]).astype(BF16)

    def seg(name, n):
        a = off[name]
        return _dot(h, w_ref[:, a:a + n])

    ha, da, hi, di = dm["HA"], dm["DA"], dm["HI"], dm["DI"]
    hb, nope, ropeb, lora = dm["HB"], dm["NOPE"], dm["ROPE"], dm["KV"]
    wa = ha * da
    c64, s64 = c64_ref[...], s64_ref[...]
    c32, s32 = c32_ref[...], s32_ref[...]

    z = seg("qa", wa)
    q = z * lax.rsqrt(_dot_hilo(z * z, bda_ref[...]) * (1.0 / da) + EPS) * gqa_ref[...]
    q = _rope(q, c64, s64, da // 2)
    for hh in range(ha):
        qam[hh] = _group_masked(q, hh, da).astype(BF16)
    z = seg("ka", wa)
    k = z * lax.rsqrt(_dot_hilo(z * z, bda_ref[...]) * (1.0 / da) + EPS) * gka_ref[...]
    k = _rope(k, c64, s64, da // 2)
    ka32[...] = k
    ka16[...] = k.astype(BF16)
    z = seg("va", wa)
    va32[...] = z
    va16[...] = z.astype(BF16)

    z = seg("qi", hi * di)
    q = _rope(z, c32, s32, di // 2)
    for hh in range(hi):
        qim[hh] = _group_masked(q, hh, di).astype(BF16)
    wit[...] = _dot_nt(wwi_ref[...], h)
    z = seg("ki", LANES)
    k = _rope(z, c32[:, :LANES], s32[:, :LANES], di // 2)
    ki32[...] = k
    ki16[...] = k.astype(BF16)

    z = seg("cq", dm["QL"])
    zn = (z * lax.rsqrt(jnp.mean(z * z, axis=-1, keepdims=True) + EPS) * gcq_ref[...]).astype(BF16)
    qb = _dot(zn, wuq_ref[...])
    wn = hb * nope
    qn, qr = qb[:, :wn], qb[:, wn:]
    zzn, zzr = qn * qn, qr * qr
    totn = _dot_hilo(zzn, bda_ref[...]) + _dot_hilo(zzr, brn_ref[...])
    totr = _dot_hilo(zzn, bnr_ref[...]) + _dot_hilo(zzr, bdr_ref[...])
    inv_d = 1.0 / (nope + ropeb)
    qn = qn * lax.rsqrt(totn * inv_d + EPS) * gqbn_ref[...]
    qr = qr * lax.rsqrt(totr * inv_d + EPS) * gqbr_ref[...]
    qpe = _rope(qr, c32, s32, ropeb // 2)
    qabs = _dot(qn.astype(BF16), wuk_ref[...])
    for hh in range(hb):
        qcat[hh, :, 0:lora] = qabs[:, hh * lora:(hh + 1) * lora].astype(BF16)
        qcat[hh, :, lora:lora + LANES] = _group_masked(qpe, hh, ropeb).astype(BF16)

    z = seg("ckv", lora)
    ckv = z * lax.rsqrt(jnp.mean(z * z, axis=-1, keepdims=True) + EPS) * gckv_ref[...]
    ckv32[...] = ckv
    kcat16[:, 0:lora] = ckv.astype(BF16)
    z = seg("kpe", LANES)
    kp = z * lax.rsqrt(jnp.mean(z * z, axis=-1, keepdims=True) + EPS) * gkpe_ref[...]
    kp = _rope(kp, c32[:, :LANES], s32[:, :LANES], ropeb // 2)
    kpe32[...] = kp
    kcat16[:, lora:lora + LANES] = kp.astype(BF16)

    d = x.shape[-1]
    ga[...] = jax.nn.sigmoid(seg("ga", d))
    gb[...] = jax.nn.sigmoid(seg("gb", d))


def _rope_tables(pos, d, reps):
    half = d // 2
    inv = ROPE_THETA ** (-(jnp.arange(half, dtype=F32) * 2.0 / d))
    ang = pos.astype(F32)[:, None] * inv[None, :]
    cos, sin = jnp.cos(ang), jnp.sin(ang)
    c = jnp.concatenate([cos, cos], axis=-1)
    s = jnp.concatenate([-sin, sin], axis=-1)
    return jnp.tile(c, (1, reps)), jnp.tile(s, (1, reps))


def _block_diag_ones(rows, rd, cols, cd):
    r = np.arange(rows)[:, None] // rd
    c = np.arange(cols)[None, :] // cd
    return jnp.asarray((r == c).astype(np.float32), dtype=BF16)


def _prep_proj_weights(dm, g_attn, w_in, g_qa, g_ka, g_cq, w_uq, g_qb, w_uk, g_ckv, g_kpe):
    ha, da, hi, di = dm["HA"], dm["DA"], dm["HI"], dm["DI"]
    hb, nope, ropeb, lora, ql = dm["HB"], dm["NOPE"], dm["ROPE"], dm["KV"], dm["QL"]
    d = w_in.shape[0]
    sizes = (ha * da, ha * da, ha * da, hi * di, hi, di, ql, lora, ropeb, d, d)
    parts, o = [], 0
    for n in sizes:
        parts.append(w_in[:, o:o + n])
        o += n
    w_qa, w_ka, w_va, w_qi, w_wi, w_ki, w_cq, w_ckv, w_kpe, w_ga, w_gb = parts
    segs = [("qa", w_qa), ("ka", w_ka), ("va", w_va), ("qi", w_qi),
            ("ki", jnp.tile(w_ki, (1, LANES // di))), ("cq", w_cq), ("ckv", w_ckv),
            ("kpe", jnp.tile(w_kpe, (1, LANES // ropeb))), ("ga", w_ga), ("gb", w_gb)]
    off, o = {}, 0
    for name, w in segs:
        off[name] = o
        o += w.shape[1]
    w_all = jnp.concatenate([w for _, w in segs], axis=1).astype(BF16)
    hip = -(-hi // 8) * 8
    wwi = jnp.zeros((hip, d), F32).at[:hi].set(w_wi.T).astype(BF16)
    wuq = jnp.concatenate([w_uq[:, :, :nope].reshape(ql, hb * nope),
                           w_uq[:, :, nope:].reshape(ql, hb * ropeb)], axis=1).astype(BF16)
    wuk = jnp.zeros((hb * nope, hb * lora), F32)
    for hh in range(hb):
        wuk = wuk.at[hh * nope:(hh + 1) * nope, hh * lora:(hh + 1) * lora].set(w_uk[hh])
    wuk = wuk.astype(BF16)
    row = lambda v: v.reshape(1, -1).astype(F32)
    consts = dict(
        gat=row(g_attn), w=w_all, wwi=wwi,
        gqa=row(jnp.tile(g_qa, ha)), gka=row(jnp.tile(g_ka, ha)), gcq=row(g_cq),
        gqbn=row(jnp.tile(g_qb[:nope], hb)), gqbr=row(jnp.tile(g_qb[nope:], hb)),
        gckv=row(g_ckv), gkpe=row(jnp.tile(g_kpe, LANES // ropeb)),
        bda=_block_diag_ones(ha * da, da, ha * da, da),
        bnr=_block_diag_ones(hb * nope, nope, hb * ropeb, ropeb),
        brn=_block_diag_ones(hb * ropeb, ropeb, hb * nope, nope),
        bdr=_block_diag_ones(hb * ropeb, ropeb, hb * ropeb, ropeb),
        wuq=wuq, wuk=wuk)
    dm = dict(dm, off=off, HIP=hip, WTOT=o)
    return dm, consts


def _project(dm, consts, x2d, pos, tm):
    n, d = x2d.shape
    p = pos.shape[0]
    assert n % tm == 0 and p % tm == 0
    npb = p // tm
    ha, da, hi, di = dm["HA"], dm["DA"], dm["HI"], dm["DI"]
    hb, ropeb, lora = dm["HB"], dm["ROPE"], dm["KV"]
    wa = ha * da
    c64, s64 = _rope_tables(pos, da, ha)
    c32, s32 = _rope_tables(pos, di, hi)
    full = lambda a: pl.BlockSpec(a.shape, lambda i: (0,) * a.ndim)
    rows = lambda w: pl.BlockSpec((tm, w), lambda i: (i, 0))
    prow = lambda w: pl.BlockSpec((tm, w), lambda i: (i % npb, 0))
    hrows = lambda hh, w: pl.BlockSpec((hh, tm, w), lambda i: (0, i, 0))
    cn = ["gat", "w", "wwi"]
    cg = ["gqa", "gka", "gcq", "gqbn", "gqbr", "gckv", "gkpe", "bda", "bnr", "brn", "bdr", "wuq", "wuk"]
    ins = [x2d] + [consts[k] for k in cn] + [c64, s64, c32, s32] + [consts[k] for k in cg]
    in_specs = ([rows(d)] + [full(consts[k]) for k in cn] + [prow(wa), prow(wa), prow(hi * di), prow(hi * di)]
                + [full(consts[k]) for k in cg])
    sd = jax.ShapeDtypeStruct
    outs = [
        (sd((n, wa), F32), rows(wa)), (sd((n, wa), F32), rows(wa)), (sd((n, LANES), F32), rows(LANES)),
        (sd((n, lora), F32), rows(lora)), (sd((n, LANES), F32), rows(LANES)),
        (sd((n, d), F32), rows(d)), (sd((n, d), F32), rows(d)),
        (sd((n, wa), BF16), rows(wa)), (sd((n, wa), BF16), rows(wa)), (sd((n, LANES), BF16), rows(LANES)),
        (sd((n, lora + LANES), BF16), rows(lora + LANES)),
        (sd((ha, n, LANES), BF16), hrows(ha, LANES)), (sd((hi, n, LANES), BF16), hrows(hi, LANES)),
        (sd((dm["HIP"], n), F32), pl.BlockSpec((dm["HIP"], tm), lambda i: (0, i))),
        (sd((hb, n, lora + LANES), BF16), hrows(hb, lora + LANES)),
    ]
    res = pl.pallas_call(
        functools.partial(_proj_kernel, dm),
        grid=(n // tm,),
        in_specs=in_specs,
        out_specs=[o[1] for o in outs],
        out_shape=[o[0] for o in outs],
        compiler_params=_cparams(("parallel",)),
        name="proj",
    )(*ins)
    names = ["ka32", "va32", "ki32", "ckv32", "kpe32", "ga", "gb", "ka16", "va16", "ki16", "kcat16",
             "qam", "qim", "wit", "qcat"]
    return dict(zip(names, res))


INT_MIN = -2147483648


def _sortable_key(x):
    bits = pltpu.bitcast(x, I32)
    return bits ^ ((bits >> 31) & 0x7FFFFFFF)


def _dsa_prompt_kernel(cfg, qim_ref, ki_ref, wit_ref, qam_ref, ka_ref, va_ref, o_ref,
                       keys_ref, bias_ref, pmax_ref):
    tq, topk, hi, ha, da, t_total = cfg["tq"], cfg["topk"], cfg["HI"], cfg["HA"], cfg["DA"], cfg["T"]
    sc1, sc2, sca = cfg["sc1"], cfg["sc2"], cfg["sca"]
    c = tq
    j = pl.program_id(1)
    nch = j + 1
    row_i = lax.broadcasted_iota(I32, (c, tq), 0)
    col_i = lax.broadcasted_iota(I32, (c, tq), 1)
    t_idx = j * tq + col_i

    def chunk(ci):
        return pl.ds(pl.multiple_of(ci * c, c), c)

    def score_body(ci, carry):
        kc = ki_ref[chunk(ci), :]
        acc = jnp.zeros((c, tq), F32)
        for h in range(hi):
            s = _dot_nt(kc, qim_ref[h]) * sc1
            acc = acc + jnp.maximum(s, 0.0) * wit_ref[h:h + 1, :]
        score = acc * sc2
        score = jnp.where(ci * c + row_i <= t_idx, score, NEG_INF)
        keys_ref[chunk(ci), :] = _sortable_key(score)
        return carry

    lax.fori_loop(0, nch, score_body, 0)

    def count(ind):
        def body(ci, acc):
            kk = keys_ref[chunk(ci), :]
            return acc + jnp.sum(ind(kk, ci), axis=0, keepdims=True)
        return lax.fori_loop(0, nch, body, jnp.zeros((1, tq), I32))

    def bit_body(i, prefix):
        cand_u = prefix | lax.shift_left(jnp.int32(1), 31 - i)
        cand_s = cand_u ^ INT_MIN
        cnt = count(lambda kk, ci: jnp.where(kk >= cand_s, 1, 0))
        return jnp.where(cnt >= topk, cand_u, prefix)

    thr = lax.fori_loop(0, 32, bit_body, jnp.zeros((1, tq), I32)) ^ INT_MIN
    n_gt = count(lambda kk, ci: jnp.where(kk > thr, 1, 0))
    n_ge = count(lambda kk, ci: jnp.where(kk >= thr, 1, 0))
    need = topk - n_gt

    pmax_ref[...] = jnp.full((1, tq), t_total, I32)

    @pl.when(jnp.max(n_ge) > topk)
    def _():
        nbits = max(1, (t_total - 1).bit_length())

        def p_body(i, p):
            cand = p | lax.shift_left(jnp.int32(1), nbits - 1 - i)
            f = count(lambda kk, ci: jnp.where(kk == thr, jnp.where(ci * c + row_i < cand, 1, 0), 0))
            return jnp.where(f < need, cand, p)

        pmax_ref[...] = lax.fori_loop(0, nbits, p_body, jnp.zeros((1, tq), I32))

    pmax = pmax_ref[...]

    def bias_body(ci, carry):
        kk = keys_ref[chunk(ci), :]
        s_idx = ci * c + row_i
        sel = jnp.where(kk > thr, 1, jnp.where(kk == thr, jnp.where(s_idx <= pmax, 1, 0), 0))
        ok = jnp.where(s_idx <= t_idx, sel, 0)
        bias_ref[ci] = jnp.where(ok > 0, 0.0, NEG_INF).T
        return carry

    lax.fori_loop(0, nch, bias_body, 0)

    per = LANES // da
    lane = lax.broadcasted_iota(I32, (1, LANES), 1)
    for g in range(ha // per):
        out_g = jnp.zeros((tq, LANES), F32)
        for hh in range(per):
            h = g * per + hh
            qh = qam_ref[h]

            def att_body(ci, carry):
                m, l, acc = carry
                kc = ka_ref[chunk(ci), g * LANES:(g + 1) * LANES]
                vc = va_ref[chunk(ci), g * LANES:(g + 1) * LANES]
                lg = _dot_nt(qh, kc) * sca + bias_ref[ci]
                m_new = jnp.maximum(m, jnp.max(lg, axis=1, keepdims=True))
                m_safe = jnp.where(m_new == NEG_INF, 0.0, m_new)
                p = jnp.exp(lg - m_safe)
                alpha = jnp.exp(m - m_safe)
                l = alpha * l + jnp.sum(p, axis=1, keepdims=True)
                acc = alpha * acc + _dot(p.astype(BF16), vc)
                return m_new, l, acc

            init = (jnp.full((tq, 1), NEG_INF, F32), jnp.zeros((tq, 1), F32), jnp.zeros((tq, LANES), F32))
            _, l, acc = lax.fori_loop(0, nch, att_body, init)
            out_g = jnp.where((lane // da) == hh, acc / l, out_g)
        o_ref[:, g * LANES:(g + 1) * LANES] = out_g.astype(o_ref.dtype)


def _dsa_prompt(dm, pr, b, t, tq):
    hi, ha, da, di = dm["HI"], dm["HA"], dm["DA"], dm["DI"]
    wa = ha * da
    topk = min(IDX_TOPK_MAX, t // 4)
    assert t % tq == 0 and topk <= tq
    nq = t // tq
    cfg = dict(tq=tq, topk=topk, HI=hi, HA=ha, DA=da, T=t,
               sc1=di ** -0.5, sc2=hi ** -0.5, sca=da ** -0.5)
    return pl.pallas_call(
        functools.partial(_dsa_prompt_kernel, cfg),
        grid=(b, nq),
        in_specs=[
            pl.BlockSpec((hi, tq, LANES), lambda bi, j: (0, bi * nq + j, 0)),
            pl.BlockSpec((t, LANES), lambda bi, j: (bi, 0)),
            pl.BlockSpec((dm["HIP"], tq), lambda bi, j: (0, bi * nq + j)),
            pl.BlockSpec((ha, tq, LANES), lambda bi, j: (0, bi * nq + j, 0)),
            pl.BlockSpec((t, wa), lambda bi, j: (bi, 0)),
            pl.BlockSpec((t, wa), lambda bi, j: (bi, 0)),
        ],
        out_specs=pl.BlockSpec((tq, wa), lambda bi, j: (bi * nq + j, 0)),
        out_shape=jax.ShapeDtypeStruct((b * t, wa), BF16),
        scratch_shapes=[pltpu.VMEM((t, tq), I32), pltpu.VMEM((nq, tq, tq), F32), pltpu.VMEM((1, tq), I32)],
        compiler_params=_cparams(("parallel", "arbitrary")),
        name="dsa_prompt",
    )(pr["qim"], pr["ki16"], pr["wit"], pr["qam"], pr["ka16"], pr["va16"])


def _mla_prompt_kernel(cfg, q_ref, k_ref, wuv_ref, o_ref, m_ref, l_ref, acc_ref):
    tq, ck, lora, scale = cfg["tq"], cfg["ck"], cfg["KV"], cfg["scale"]
    hb, _, kc = q_ref.shape
    rows = hb * tq
    j = pl.program_id(1)
    q = q_ref[...].reshape(rows, kc)
    m_ref[...] = jnp.full((rows, 1), NEG_INF, F32)
    l_ref[...] = jnp.zeros((rows, 1), F32)
    acc_ref[...] = jnp.zeros((rows, lora), F32)
    t_idx = j * tq + (lax.broadcasted_iota(I32, (rows, 1), 0) & (tq - 1))
    col = lax.broadcasted_iota(I32, (1, ck), 1)
    nch = ((j + 1) * tq + ck - 1) // ck

    def body(ci, carry):
        kk = k_ref[pl.ds(pl.multiple_of(ci * ck, ck), ck), :]
        s = _dot_nt(q, kk) * scale
        s = jnp.where(ci * ck + col <= t_idx, s, NEG_INF)
        m_prev = m_ref[...]
        m_new = jnp.maximum(m_prev, jnp.max(s, axis=1, keepdims=True))
        p = jnp.exp(s - m_new)
        alpha = jnp.exp(m_prev - m_new)
        l_ref[...] = alpha * l_ref[...] + jnp.sum(p, axis=1, keepdims=True)
        acc_ref[...] = alpha * acc_ref[...] + _dot(p.astype(BF16), kk[:, :lora])
        m_ref[...] = m_new
        return carry

    lax.fori_loop(0, nch, body, 0)
    o = (acc_ref[...] / l_ref[...]).astype(BF16)
    out = _dot(o[0:tq], wuv_ref[0])
    for h in range(1, hb):
        out = out + _dot(o[h * tq:(h + 1) * tq], wuv_ref[h])
    o_ref[...] = out.astype(o_ref.dtype)


def _pad_wuv(w_uv):
    hb, lora, bv = w_uv.shape
    w = jnp.zeros((hb, lora, hb * bv), F32)
    for h in range(hb):
        w = w.at[h, :, h * bv:(h + 1) * bv].set(w_uv[h])
    return w.astype(BF16)


def _mla_prompt(dm, pr, wuv_pad, b, t, tq, ck):
    hb, lora, nope, ropeb, bv = dm["HB"], dm["KV"], dm["NOPE"], dm["ROPE"], dm["BV"]
    kc = lora + LANES
    assert t % tq == 0 and t % ck == 0 and tq & (tq - 1) == 0
    nq = t // tq
    cfg = dict(tq=tq, ck=ck, KV=lora, scale=(nope + ropeb) ** -0.5)
    rows = hb * tq
    return pl.pallas_call(
        functools.partial(_mla_prompt_kernel, cfg),
        grid=(b, nq),
        in_specs=[
            pl.BlockSpec((hb, tq, kc), lambda bi, j: (0, bi * nq + j, 0)),
            pl.BlockSpec((t, kc), lambda bi, j: (bi, 0)),
            pl.BlockSpec(wuv_pad.shape, lambda bi, j: (0, 0, 0)),
        ],
        out_specs=pl.BlockSpec((tq, hb * bv), lambda bi, j: (bi * nq + j, 0)),
        out_shape=jax.ShapeDtypeStruct((b * t, hb * bv), BF16),
        scratch_shapes=[pltpu.VMEM((rows, 1), F32), pltpu.VMEM((rows, 1), F32), pltpu.VMEM((rows, lora), F32)],
        compiler_params=_cparams(("parallel", "arbitrary")),
        name="mla_prompt",
    )(pr["qcat"], pr["kcat16"], wuv_pad)


def _finish_kernel(x_ref, oa_ref, ob_ref, ga_ref, gb_ref, wpa_ref, wpb_ref, wo_ref, gf_ref, wpq_ref,
                   x1_ref, x1n_ref, q_ref):
    merged = ga_ref[...] * _dot(oa_ref[...], wpa_ref[...]) + gb_ref[...] * _dot(ob_ref[...], wpb_ref[...])
    x1 = x_ref[...] + _dot(merged.astype(BF16), wo_ref[...])
    x1_ref[...] = x1
    ms = jnp.mean(x1 * x1, axis=-1, keepdims=True)
    x1n = (x1 * lax.rsqrt(ms + EPS) * gf_ref[...]).astype(BF16)
    x1n_ref[...] = x1n
    q_ref[...] = _dot(x1n, wpq_ref[...]).astype(q_ref.dtype)


def _finish(x2d, oa, ob, ga, gb, wts, tm):
    n, d = x2d.shape
    assert n % tm == 0
    rows = lambda w: pl.BlockSpec((tm, w), lambda i: (i, 0))
    full = lambda a: pl.BlockSpec(a.shape, lambda i: (0,) * a.ndim)
    wpa, wpb, wo, gf, wpq = wts
    dq = wpq.shape[1]
    return pl.pallas_call(
        _finish_kernel,
        grid=(n // tm,),
        in_specs=[rows(d), rows(oa.shape[1]), rows(ob.shape[1]), rows(d), rows(d),
                  full(wpa), full(wpb), full(wo), full(gf), full(wpq)],
        out_specs=[rows(d), rows(d), rows(dq)],
        out_shape=[jax.ShapeDtypeStruct((n, d), F32), jax.ShapeDtypeStruct((n, d), BF16),
                   jax.ShapeDtypeStruct((n, dq), BF16)],
        compiler_params=_cparams(("parallel",)),
        name="finish",
    )(x2d, oa, ob, ga, gb, wpa, wpb, wo, gf, wpq)


def _oddeven_merge(lo, hi, r):
    step = r * 2
    if step < hi - lo:
        yield from _oddeven_merge(lo, hi, step)
        yield from _oddeven_merge(lo + r, hi, step)
        yield from [(i, i + r) for i in range(lo + r, hi - r, step)]
    else:
        yield (lo, lo + r)


def _oddeven_merge_sort(lo, hi):
    if hi - lo >= 1:
        mid = lo + (hi - lo) // 2
        yield from _oddeven_merge_sort(lo, mid)
        yield from _oddeven_merge_sort(mid + 1, hi)
        yield from _oddeven_merge(lo, hi, 1)


SORT16 = tuple(_oddeven_merge_sort(0, PEER_TOPK - 1))
BITONIC16 = tuple((i, i + k) for k in (8, 4, 2, 1) for i in range(PEER_TOPK) if (i & k) == 0)
SUBLANES = 8


def _compare_exchange(x, net):
    for i, j in net:
        x[i], x[j] = jnp.maximum(x[i], x[j]), jnp.minimum(x[i], x[j])
    return x


def _merge_top16(a, b_rev):
    return _compare_exchange([jnp.maximum(x, y) for x, y in zip(a, b_rev)], BITONIC16)


def _top16_sorted(s):
    x = _compare_exchange([s[i * SUBLANES:(i + 1) * SUBLANES, :] for i in range(PEER_TOPK)], SORT16)
    for shift in (4, 6, 7):
        x = _merge_top16(x, [pltpu.roll(x[PEER_TOPK - 1 - i], shift, 0) for i in range(PEER_TOPK)])
    return x


def _peer_route_kernel(cfg, q_ref, k1_ref, k2_ref, s1_ref, s2_ref, e1_ref, e2_ref, thr_ref):
    hp, dk = cfg["HP"], cfg["DK"]
    tn = q_ref.shape[0]
    sub = lax.broadcasted_iota(I32, (SUBLANES, tn), 0)
    v1p = [jnp.zeros((SUBLANES, tn), F32)] * PEER_TOPK
    v2p = [jnp.zeros((SUBLANES, tn), F32)] * PEER_TOPK
    for h in range(hp):
        qh = q_ref[:, h * dk:(h + 1) * dk]
        s1 = _dot_nt(k1_ref[h], qh)
        s2 = _dot_nt(k2_ref[h], qh)
        s1_ref[h] = s1
        s2_ref[h] = s2
        t1 = _top16_sorted(s1)
        t2 = _top16_sorted(s2)
        v1p = [jnp.where(sub == h, a, b) for a, b in zip(t1, v1p)]
        v2p = [jnp.where(sub == h, a, b) for a, b in zip(t2, v2p)]
    cur = [v1p[0] + v2p[r2] for r2 in range(PEER_TOPK)]
    for r1 in range(1, PEER_TOPK):
        lim = PEER_TOPK // (r1 + 1)
        b_rev = [v1p[r1] + v2p[PEER_TOPK - 1 - i] if i >= PEER_TOPK - lim else None for i in range(PEER_TOPK)]
        cur = _compare_exchange([a if b is None else jnp.maximum(a, b) for a, b in zip(cur, b_rev)], BITONIC16)
    thr = cur[PEER_TOPK - 1]
    z = jnp.zeros((SUBLANES, tn), F32)
    for r in range(PEER_TOPK):
        z = z + jnp.exp(cur[r] - cur[0])
    thr_ref[...] = thr
    rz = 1.0 / z
    for h in range(hp):
        e1_ref[h] = jnp.exp(s1_ref[h] - v1p[0][h:h + 1, :]) * rz[h:h + 1, :]
        e2_ref[h] = jnp.exp(s2_ref[h] - v2p[0][h:h + 1, :])


def _peer_route(dm, q, k1p, k2p, tn):
    n = q.shape[0]
    hp, nk, dk = dm["HP"], dm["NK"], dm["DK"]
    assert n % tn == 0 and nk == PEER_TOPK * SUBLANES and hp <= SUBLANES and dk % LANES == 0
    big = pl.BlockSpec((hp, nk, tn), lambda i: (0, 0, i))
    sds = jax.ShapeDtypeStruct((hp, nk, n), F32)
    return pl.pallas_call(
        functools.partial(_peer_route_kernel, dict(HP=hp, DK=dk)),
        grid=(n // tn,),
        in_specs=[pl.BlockSpec((tn, hp * dk), lambda i: (i, 0)),
                  pl.BlockSpec(k1p.shape, lambda i: (0, 0, 0)), pl.BlockSpec(k2p.shape, lambda i: (0, 0, 0))],
        out_specs=[big, big, big, big, pl.BlockSpec((SUBLANES, tn), lambda i: (0, i))],
        out_shape=[sds, sds, sds, sds, jax.ShapeDtypeStruct((SUBLANES, n), F32)],
        compiler_params=_cparams(("parallel",)),
        name="peer_route",
    )(q, k1p, k2p)


def _peer_dense_kernel(cfg, x_ref, x1_ref, u_ref, vt_ref, s1_ref, s2_ref, e1_ref, e2_ref, thr_ref,
                       y_ref, acc_ref, g_ref):
    hp, nk = cfg["HP"], cfg["NK"]
    eb = u_ref.shape[0]
    e = pl.program_id(1)

    @pl.when(e == 0)
    def _():
        acc_ref[...] = jnp.zeros(acc_ref.shape, F32)

    a = _dot_nt(u_ref[...], x_ref[...])
    act = 0.5 * a * (1.0 + lax.erf(a * (2.0 ** -0.5)))
    for k in range(eb // nk):
        i1 = e * (eb // nk) + k
        g = None
        for h in range(hp):
            cand = s1_ref[h, pl.ds(i1, 1), :] + s2_ref[h]
            w = jnp.where(cand >= thr_ref[h:h + 1, :], e1_ref[h, pl.ds(i1, 1), :] * e2_ref[h], 0.0)
            g = w if g is None else g + w
        g_ref[k * nk:(k + 1) * nk, :] = (g * act[k * nk:(k + 1) * nk, :]).astype(BF16)
    acc_ref[...] += _dot(vt_ref[...], g_ref[...])

    @pl.when(e == pl.num_programs(1) - 1)
    def _():
        y_ref[...] = x1_ref[...] + acc_ref[...].T


def _peer_dense(dm, x1n, x1, u16, vt16, route, tn, eb):
    n, d = x1.shape
    ne = u16.shape[0]
    hp, nk = dm["HP"], dm["NK"]
    assert n % tn == 0 and ne % eb == 0 and eb % nk == 0
    s1, s2, e1, e2, thr = route
    big = pl.BlockSpec((hp, nk, tn), lambda i, e: (0, 0, i))
    return pl.pallas_call(
        functools.partial(_peer_dense_kernel, dict(HP=hp, NK=nk)),
        grid=(n // tn, ne // eb),
        in_specs=[pl.BlockSpec((tn, d), lambda i, e: (i, 0)), pl.BlockSpec((tn, d), lambda i, e: (i, 0)),
                  pl.BlockSpec((eb, d), lambda i, e: (e, 0)), pl.BlockSpec((d, eb), lambda i, e: (0, e)),
                  big, big, big, big, pl.BlockSpec((SUBLANES, tn), lambda i, e: (0, i))],
        out_specs=pl.BlockSpec((tn, d), lambda i, e: (i, 0)),
        out_shape=jax.ShapeDtypeStruct((n, d), F32),
        scratch_shapes=[pltpu.VMEM((d, tn), F32), pltpu.VMEM((eb, tn), BF16)],
        compiler_params=_cparams(("parallel", "arbitrary")),
        name="peer_dense",
    )(x1n, x1, u16, vt16, s1, s2, e1, e2, thr)


def _tail(dm, x2d, oa, ob, ga, gb, tail_w, peer_w, tm, tn_route, tn_dense, eb):
    x1, x1n, q = _finish(x2d, oa, ob, ga, gb, tail_w, tm)
    k1p, k2p, u16, vt16 = peer_w
    route = _peer_route(dm, q, k1p, k2p, tn_route)
    return _peer_dense(dm, x1n, x1, u16, vt16, route, tn_dense, eb)


def _sample_standin(dm, ps, cache_k, cache_v, cache_idx_k, cache_ckv, cache_kpe, page_table, w_uv):
    bs, n_pages = page_table.shape
    page = cache_k.shape[1]
    past = page * n_pages
    ha, da, hi, di = dm["HA"], dm["DA"], dm["HI"], dm["DI"]
    hb, lora, ropeb, nope, bv = dm["HB"], dm["KV"], dm["ROPE"], dm["NOPE"], dm["BV"]
    per = LANES // di
    qi = jnp.stack([ps["qim"][h].astype(F32)[:, (h % per) * di:(h % per + 1) * di] for h in range(hi)], 1)
    wi = ps["wit"][:hi].T
    ki_all = jnp.concatenate([cache_idx_k[page_table].reshape(bs, past, di), ps["ki32"][:, None, :di]], 1)
    s = jnp.einsum("bhd,bsd->bsh", qi, ki_all) * (di ** -0.5)
    score = jnp.einsum("bsh,bh->bs", jax.nn.relu(s), wi) * (hi ** -0.5)
    topk = min(IDX_TOPK_MAX, (past + 1) // 4)
    _, ti = lax.top_k(score, topk)
    sel = jnp.zeros((bs, past + 1), bool).at[jnp.arange(bs)[:, None], ti].set(True)
    per = LANES // da
    qa = jnp.stack([ps["qam"][h].astype(F32)[:, (h % per) * da:(h % per + 1) * da] for h in range(ha)], 1)
    k_all = jnp.concatenate([cache_k[page_table].reshape(bs, past, ha, da), ps["ka32"].reshape(bs, 1, ha, da)], 1)
    v_all = jnp.concatenate([cache_v[page_table].reshape(bs, past, ha, da), ps["va32"].reshape(bs, 1, ha, da)], 1)
    lg = jnp.einsum("bhd,bshd->bhs", qa, k_all) * (da ** -0.5)
    p = jax.nn.softmax(jnp.where(sel[:, None], lg, -jnp.inf), axis=-1)
    oa = jnp.einsum("bhs,bshd->bhd", p, v_all).reshape(bs, ha * da)
    qc = ps["qcat"].astype(F32)
    per = LANES // ropeb
    qabs = qc[:, :, :lora]
    qpe = jnp.stack([qc[h][:, lora + (h % per) * ropeb: lora + (h % per + 1) * ropeb] for h in range(hb)], 0)
    ckv_all = jnp.concatenate([cache_ckv[page_table].reshape(bs, past, lora), ps["ckv32"][:, None]], 1)
    kpe_all = jnp.concatenate([cache_kpe[page_table].reshape(bs, past, ropeb), ps["kpe32"][:, None, :ropeb]], 1)
    lg = (jnp.einsum("hbc,bsc->bhs", qabs, ckv_all) + jnp.einsum("hbr,bsr->bhs", qpe, kpe_all)) * ((nope + ropeb) ** -0.5)
    p = jax.nn.softmax(lg, axis=-1)
    o_lat = jnp.einsum("bhs,bsc->bhc", p, ckv_all)
    ob = jnp.einsum("bhc,hcv->bhv", o_lat, w_uv).reshape(bs, hb * bv)
    return oa.astype(BF16), ob.astype(BF16)


def _model_dims(cache_k, cache_idx_k, cache_ckv, cache_kpe, w_in, g_cq, w_uq, w_uk, w_uv, sub_k1):
    ha, da = cache_k.shape[2], cache_k.shape[3]
    di, lora, ropeb = cache_idx_k.shape[-1], cache_ckv.shape[-1], cache_kpe.shape[-1]
    ql, hb, nope, bv = g_cq.shape[0], w_uq.shape[1], w_uk.shape[1], w_uv.shape[2]
    d = w_in.shape[0]
    rest = w_in.shape[1] - (3 * ha * da + di + ql + lora + ropeb + 2 * d)
    hi = rest // (di + 1)
    assert hi * (di + 1) == rest
    for v in (da, di, ropeb):
        assert LANES % v == 0
    for v in (ha * da, hi * di, hb * nope, hb * ropeb, lora, ql, d):
        assert v % LANES == 0
    return dict(HA=ha, DA=da, HI=hi, DI=di, HB=hb, NOPE=nope, ROPE=ropeb, KV=lora, QL=ql, BV=bv, D=d,
                HP=sub_k1.shape[0], NK=sub_k1.shape[1], DK=2 * sub_k1.shape[2])


def kernel(x_prompt, x_sample, cache_k, cache_v, cache_idx_k, cache_ckv, cache_kpe, page_table, g_attn, w_in, g_qa, g_ka, g_cq, w_uq, g_qb, w_uk, g_ckv, g_kpe, w_uv, w_pa, w_pb, w_o, g_ffn, w_pq, sub_k1, sub_k2, u_tab, v_tab):
    dm = _model_dims(cache_k, cache_idx_k, cache_ckv, cache_kpe, w_in, g_cq, w_uq, w_uk, w_uv, sub_k1)
    dm, consts = _prep_proj_weights(dm, g_attn, w_in, g_qa, g_ka, g_cq, w_uq, g_qb, w_uk, g_ckv, g_kpe)
    b, t, d = x_prompt.shape
    bs, ts, _ = x_sample.shape
    page, n_pages = cache_k.shape[1], page_table.shape[1]
    past = page * n_pages
    ha, da, di, lora, ropeb = dm["HA"], dm["DA"], dm["DI"], dm["KV"], dm["ROPE"]
    hp, nk, dk = dm["HP"], dm["NK"], dm["DK"]

    wuv_pad = _pad_wuv(w_uv)
    tail_w = (w_pa.astype(BF16), w_pb.astype(BF16), w_o.astype(BF16), g_ffn.reshape(1, -1), w_pq.astype(BF16))
    zeros_half = jnp.zeros((hp, nk, dk // 2), F32)
    k1p = jnp.concatenate([sub_k1, zeros_half], axis=-1).astype(BF16)
    k2p = jnp.concatenate([zeros_half, sub_k2], axis=-1).astype(BF16)
    peer_w = (k1p, k2p, u_tab.astype(BF16), v_tab.T.astype(BF16))

    n_p = b * t
    x_p = x_prompt.reshape(n_p, d)
    pp = _project(dm, consts, x_p, jnp.arange(t, dtype=I32), min(256, t))
    oa_p = _dsa_prompt(dm, pp, b, t, min(256, t))
    ob_p = _mla_prompt(dm, pp, wuv_pad, b, t, min(128, t), min(256, t))
    y_p = _tail(dm, x_p, oa_p, ob_p, pp["ga"], pp["gb"], tail_w, peer_w, min(256, n_p), 128, min(512, n_p), 512)

    n_s = bs * ts
    assert ts == 1
    x_s = x_sample.reshape(n_s, d)
    ps = _project(dm, consts, x_s, jnp.full((n_s,), past, I32), n_s)
    oa_s, ob_s = _sample_standin(dm, ps, cache_k, cache_v, cache_idx_k, cache_ckv, cache_kpe, page_table, w_uv)
    y_s = _tail(dm, x_s, oa_s, ob_s, ps["ga"], ps["gb"], tail_w, peer_w, n_s, n_s, n_s, 512)

    def kv_outs(p, bb, tt):
        return (p["ka32"].reshape(bb, tt, ha, da), p["va32"].reshape(bb, tt, ha, da),
                p["ki32"][:, :di].reshape(bb, tt, di), p["ckv32"].reshape(bb, tt, lora),
                p["kpe32"][:, :ropeb].reshape(bb, tt, ropeb))

    return (y_p.reshape(b, t, d), y_s.reshape(bs, ts, d)) + kv_outs(pp, b, t) + kv_outs(ps, bs, ts)
```

```python
import functools
import math

import numpy as np
import jax
import jax.numpy as jnp
from jax import lax
from jax.experimental import pallas as pl
from jax.experimental.pallas import tpu as pltpu

F32 = jnp.float32
BF16 = jnp.bfloat16
I32 = jnp.int32

ROPE_THETA = 10000.0
EPS = 1e-6
IDX_TOPK_MAX = 256
PEER_TOPK = 16
LANES = 128
NEG_INF = float("-inf")
VMEM_LIMIT = 56 * 1024 * 1024


def _cparams(sem):
    return pltpu.CompilerParams(dimension_semantics=sem, vmem_limit_bytes=VMEM_LIMIT)


def _dot(a, b):
    return jnp.dot(a, b, preferred_element_type=F32)


def _dot_nt(a, b):
    return lax.dot_general(a, b, (((1,), (1,)), ((), ())), preferred_element_type=F32)


def _dot_hilo(a, b):
    hi = a.astype(BF16)
    lo = (a - hi.astype(F32)).astype(BF16)
    return _dot(hi, b) + _dot(lo, b)


def _rope(x, c, s, half):
    w = x.shape[-1]
    fwd = pltpu.roll(x, half, 1)
    bwd = pltpu.roll(x, w - half, 1)
    lane = lax.broadcasted_iota(I32, (1, w), 1)
    first = (lane & (2 * half - 1)) < half
    return x * c + jnp.where(first, bwd, fwd) * s


def _group_masked(x, h, d):
    per = LANES // d
    g = h // per
    grp = x[:, g * LANES:(g + 1) * LANES]
    lane = lax.broadcasted_iota(I32, (1, LANES), 1)
    return jnp.where((lane // d) == (h % per), grp, 0.0)


def _proj_kernel(dm, x_ref, gat_ref, w_ref, wwi_ref, c64_ref, s64_ref, c32_ref, s32_ref,
                 gqa_ref, gka_ref, gcq_ref, gqbn_ref, gqbr_ref, gckv_ref, gkpe_ref,
                 bda_ref, bnr_ref, brn_ref, bdr_ref, wuq_ref, wuk_ref,
                 ka32, va32, ki32, ckv32, kpe32, ga, gb, ka16, va16, ki16, kcat16,
                 qam, qim, wit, qcat):
    off = dm["off"]
    x = x_ref[...]
    ms = jnp.mean(x * x, axis=-1, keepdims=True)
    h = (x * lax.rsqrt(ms + EPS) * gat_ref[...]).astype(BF16)

    def seg(name, n):
        a = off[name]
        return _dot(h, w_ref[:, a:a + n])

    ha, da, hi, di = dm["HA"], dm["DA"], dm["HI"], dm["DI"]
    hb, nope, ropeb, lora = dm["HB"], dm["NOPE"], dm["ROPE"], dm["KV"]
    wa = ha * da
    c64, s64 = c64_ref[...], s64_ref[...]
    c32, s32 = c32_ref[...], s32_ref[...]

    z = seg("qa", wa)
    q = z * lax.rsqrt(_dot_hilo(z * z, bda_ref[...]) * (1.0 / da) + EPS) * gqa_ref[...]
    q = _rope(q, c64, s64, da // 2)
    for hh in range(ha):
        qam[hh] = _group_masked(q, hh, da).astype(BF16)
    z = seg("ka", wa)
    k = z * lax.rsqrt(_dot_hilo(z * z, bda_ref[...]) * (1.0 / da) + EPS) * gka_ref[...]
    k = _rope(k, c64, s64, da // 2)
    ka32[...] = k
    ka16[...] = k.astype(BF16)
    z = seg("va", wa)
    va32[...] = z
    va16[...] = z.astype(BF16)

    z = seg("qi", hi * di)
    q = _rope(z, c32, s32, di // 2)
    for hh in range(hi):
        qim[hh] = _group_masked(q, hh, di).astype(BF16)
    wit[...] = _dot_nt(wwi_ref[...], h)
    z = seg("ki", LANES)
    k = _rope(z, c32[:, :LANES], s32[:, :LANES], di // 2)
    ki32[...] = k
    ki16[...] = k.astype(BF16)

    z = seg("cq", dm["QL"])
    zn = (z * lax.rsqrt(jnp.mean(z * z, axis=-1, keepdims=True) + EPS) * gcq_ref[...]).astype(BF16)
    qb = _dot(zn, wuq_ref[...])
    wn = hb * nope
    qn, qr = qb[:, :wn], qb[:, wn:]
    zzn, zzr = qn * qn, qr * qr
    totn = _dot_hilo(zzn, bda_ref[...]) + _dot_hilo(zzr, brn_ref[...])
    totr = _dot_hilo(zzn, bnr_ref[...]) + _dot_hilo(zzr, bdr_ref[...])
    inv_d = 1.0 / (nope + ropeb)
    qn = qn * lax.rsqrt(totn * inv_d + EPS) * gqbn_ref[...]
    qr = qr * lax.rsqrt(totr * inv_d + EPS) * gqbr_ref[...]
    qpe = _rope(qr, c32, s32, ropeb // 2)
    qabs = _dot(qn.astype(BF16), wuk_ref[...])
    for hh in range(hb):
        qcat[hh, :, 0:lora] = qabs[:, hh * lora:(hh + 1) * lora].astype(BF16)
        qcat[hh, :, lora:lora + LANES] = _group_masked(qpe, hh, ropeb).astype(BF16)

    z = seg("ckv", lora)
    ckv = z * lax.rsqrt(jnp.mean(z * z, axis=-1, keepdims=True) + EPS) * gckv_ref[...]
    ckv32[...] = ckv
    kcat16[:, 0:lora] = ckv.astype(BF16)
    z = seg("kpe", LANES)
    kp = z * lax.rsqrt(jnp.mean(z * z, axis=-1, keepdims=True) + EPS) * gkpe_ref[...]
    kp = _rope(kp, c32[:, :LANES], s32[:, :LANES], ropeb // 2)
    kpe32[...] = kp
    kcat16[:, lora:lora + LANES] = kp.astype(BF16)

    d = x.shape[-1]
    ga[...] = jax.nn.sigmoid(seg("ga", d))
    gb[...] = jax.nn.sigmoid(seg("gb", d))


def _rope_tables(pos, d, reps):
    half = d // 2
    inv = ROPE_THETA ** (-(jnp.arange(half, dtype=F32) * 2.0 / d))
    ang = pos.astype(F32)[:, None] * inv[None, :]
    cos, sin = jnp.cos(ang), jnp.sin(ang)
    c = jnp.concatenate([cos, cos], axis=-1)
    s = jnp.concatenate([-sin, sin], axis=-1)
    return jnp.tile(c, (1, reps)), jnp.tile(s, (1, reps))


def _block_diag_ones(rows, rd, cols, cd):
    r = np.arange(rows)[:, None] // rd
    c = np.arange(cols)[None, :] // cd
    return jnp.asarray((r == c).astype(np.float32), dtype=BF16)


def _prep_proj_weights(dm, g_attn, w_in, g_qa, g_ka, g_cq, w_uq, g_qb, w_uk, g_ckv, g_kpe):
    ha, da, hi, di = dm["HA"], dm["DA"], dm["HI"], dm["DI"]
    hb, nope, ropeb, lora, ql = dm["HB"], dm["NOPE"], dm["ROPE"], dm["KV"], dm["QL"]
    d = w_in.shape[0]
    sizes = (ha * da, ha * da, ha * da, hi * di, hi, di, ql, lora, ropeb, d, d)
    parts, o = [], 0
    for n in sizes:
        parts.append(w_in[:, o:o + n])
        o += n
    w_qa, w_ka, w_va, w_qi, w_wi, w_ki, w_cq, w_ckv, w_kpe, w_ga, w_gb = parts
    segs = [("qa", w_qa), ("ka", w_ka), ("va", w_va), ("qi", w_qi),
            ("ki", jnp.tile(w_ki, (1, LANES // di))), ("cq", w_cq), ("ckv", w_ckv),
            ("kpe", jnp.tile(w_kpe, (1, LANES // ropeb))), ("ga", w_ga), ("gb", w_gb)]
    off, o = {}, 0
    for name, w in segs:
        off[name] = o
        o += w.shape[1]
    w_all = jnp.concatenate([w for _, w in segs], axis=1).astype(BF16)
    hip = -(-hi // 8) * 8
    wwi = jnp.zeros((hip, d), F32).at[:hi].set(w_wi.T).astype(BF16)
    wuq = jnp.concatenate([w_uq[:, :, :nope].reshape(ql, hb * nope),
                           w_uq[:, :, nope:].reshape(ql, hb * ropeb)], axis=1).astype(BF16)
    wuk = jnp.zeros((hb * nope, hb * lora), F32)
    for hh in range(hb):
        wuk = wuk.at[hh * nope:(hh + 1) * nope, hh * lora:(hh + 1) * lora].set(w_uk[hh])
    wuk = wuk.astype(BF16)
    row = lambda v: v.reshape(1, -1).astype(F32)
    consts = dict(
        gat=row(g_attn), w=w_all, wwi=wwi,
        gqa=row(jnp.tile(g_qa, ha)), gka=row(jnp.tile(g_ka, ha)), gcq=row(g_cq),
        gqbn=row(jnp.tile(g_qb[:nope], hb)), gqbr=row(jnp.tile(g_qb[nope:], hb)),
        gckv=row(g_ckv), gkpe=row(jnp.tile(g_kpe, LANES // ropeb)),
        bda=_block_diag_ones(ha * da, da, ha * da, da),
        bnr=_block_diag_ones(hb * nope, nope, hb * ropeb, ropeb),
        brn=_block_diag_ones(hb * ropeb, ropeb, hb * nope, nope),
        bdr=_block_diag_ones(hb * ropeb, ropeb, hb * ropeb, ropeb),
        wuq=wuq, wuk=wuk)
    dm = dict(dm, off=off, HIP=hip, WTOT=o)
    return dm, consts


def _project(dm, consts, x2d, pos, tm):
    n, d = x2d.shape
    p = pos.shape[0]
    assert n % tm == 0 and p % tm == 0
    npb = p // tm
    ha, da, hi, di = dm["HA"], dm["DA"], dm["HI"], dm["DI"]
    hb, ropeb, lora = dm["HB"], dm["ROPE"], dm["KV"]
    wa = ha * da
    c64, s64 = _rope_tables(pos, da, ha)
    c32, s32 = _rope_tables(pos, di, hi)
    full = lambda a: pl.BlockSpec(a.shape, lambda i: (0,) * a.ndim)
    rows = lambda w: pl.BlockSpec((tm, w), lambda i: (i, 0))
    prow = lambda w: pl.BlockSpec((tm, w), lambda i: (i % npb, 0))
    hrows = lambda hh, w: pl.BlockSpec((hh, tm, w), lambda i: (0, i, 0))
    cn = ["gat", "w", "wwi"]
    cg = ["gqa", "gka", "gcq", "gqbn", "gqbr", "gckv", "gkpe", "bda", "bnr", "brn", "bdr", "wuq", "wuk"]
    ins = [x2d] + [consts[k] for k in cn] + [c64, s64, c32, s32] + [consts[k] for k in cg]
    in_specs = ([rows(d)] + [full(consts[k]) for k in cn] + [prow(wa), prow(wa), prow(hi * di), prow(hi * di)]
                + [full(consts[k]) for k in cg])
    sd = jax.ShapeDtypeStruct
    outs = [
        (sd((n, wa), F32), rows(wa)), (sd((n, wa), F32), rows(wa)), (sd((n, LANES), F32), rows(LANES)),
        (sd((n, lora), F32), rows(lora)), (sd((n, LANES), F32), rows(LANES)),
        (sd((n, d), F32), rows(d)), (sd((n, d), F32), rows(d)),
        (sd((n, wa), BF16), rows(wa)), (sd((n, wa), BF16), rows(wa)), (sd((n, LANES), BF16), rows(LANES)),
        (sd((n, lora + LANES), BF16), rows(lora + LANES)),
        (sd((ha, n, LANES), BF16), hrows(ha, LANES)), (sd((hi, n, LANES), BF16), hrows(hi, LANES)),
        (sd((dm["HIP"], n), F32), pl.BlockSpec((dm["HIP"], tm), lambda i: (0, i))),
        (sd((hb, n, lora + LANES), BF16), hrows(hb, lora + LANES)),
    ]
    res = pl.pallas_call(
        functools.partial(_proj_kernel, dm),
        grid=(n // tm,),
        in_specs=in_specs,
        out_specs=[o[1] for o in outs],
        out_shape=[o[0] for o in outs],
        compiler_params=_cparams(("parallel",)),
        name="proj",
    )(*ins)
    names = ["ka32", "va32", "ki32", "ckv32", "kpe32", "ga", "gb", "ka16", "va16", "ki16", "kcat16",
             "qam", "qim", "wit", "qcat"]
    return dict(zip(names, res))


INT_MIN = -2147483648


def _sortable_key(x):
    bits = pltpu.bitcast(x, I32)
    return bits ^ ((bits >> 31) & 0x7FFFFFFF)


def _dsa_prompt_kernel(cfg, qim_ref, ki_ref, wit_ref, qam_ref, ka_ref, va_ref, o_ref,
                       keys_ref, bias_ref, pmax_ref, m_ref, l_ref, acc_ref):
    tq, topk, hi, ha, da, t_total = cfg["tq"], cfg["topk"], cfg["HI"], cfg["HA"], cfg["DA"], cfg["T"]
    sc1, sc2, sca = cfg["sc1"], cfg["sc2"], cfg["sca"]
    c = tq
    j = pl.program_id(1)
    nch = j + 1
    row_i = lax.broadcasted_iota(I32, (c, tq), 0)
    col_i = lax.broadcasted_iota(I32, (c, tq), 1)
    t_idx = j * tq + col_i

    def chunk(ci):
        return pl.ds(pl.multiple_of(ci * c, c), c)

    def score_body(ci, carry):
        kc = ki_ref[chunk(ci), :]
        acc = jnp.zeros((c, tq), F32)
        for h in range(hi):
            s = _dot_nt(kc, qim_ref[h]) * sc1
            acc = acc + jnp.maximum(s, 0.0) * wit_ref[h:h + 1, :]
        score = acc * sc2
        score = jnp.where(ci * c + row_i <= t_idx, score, NEG_INF)
        keys_ref[chunk(ci), :] = _sortable_key(score)
        return carry

    lax.fori_loop(0, nch, score_body, 0)

    def count(ind):
        def body(ci, acc):
            kk = keys_ref[chunk(ci), :]
            return acc + jnp.sum(ind(kk, ci), axis=0, keepdims=True)
        return lax.fori_loop(0, nch, body, jnp.zeros((1, tq), I32))

    def bit_body(i, prefix):
        cand_u = prefix | lax.shift_left(jnp.int32(1), 31 - i)
        cand_s = cand_u ^ INT_MIN
        cnt = count(lambda kk, ci: jnp.where(kk >= cand_s, 1, 0))
        return jnp.where(cnt >= topk, cand_u, prefix)

    thr = lax.fori_loop(0, 32, bit_body, jnp.zeros((1, tq), I32)) ^ INT_MIN
    n_gt = count(lambda kk, ci: jnp.where(kk > thr, 1, 0))
    n_ge = count(lambda kk, ci: jnp.where(kk >= thr, 1, 0))
    need = topk - n_gt

    pmax_ref[...] = jnp.full((1, tq), t_total, I32)

    @pl.when(jnp.max(n_ge) > topk)
    def _():
        nbits = max(1, (t_total - 1).bit_length())

        def p_body(i, p):
            cand = p | lax.shift_left(jnp.int32(1), nbits - 1 - i)
            f = count(lambda kk, ci: jnp.where(kk == thr, jnp.where(ci * c + row_i < cand, 1, 0), 0))
            return jnp.where(f < need, cand, p)

        pmax_ref[...] = lax.fori_loop(0, nbits, p_body, jnp.zeros((1, tq), I32))

    pmax = pmax_ref[...]

    def bias_body(ci, carry):
        kk = keys_ref[chunk(ci), :]
        s_idx = ci * c + row_i
        sel = jnp.where(kk > thr, 1, jnp.where(kk == thr, jnp.where(s_idx <= pmax, 1, 0), 0))
        ok = jnp.where(s_idx <= t_idx, sel, 0)
        bias_ref[ci] = jnp.where(ok > 0, 0.0, NEG_INF).T
        return carry

    lax.fori_loop(0, nch, bias_body, 0)

    per = LANES // da
    lane = lax.broadcasted_iota(I32, (1, LANES), 1)
    m_ref[...] = jnp.full(m_ref.shape, NEG_INF, F32)
    l_ref[...] = jnp.zeros(l_ref.shape, F32)
    acc_ref[...] = jnp.zeros(acc_ref.shape, F32)

    def att_body(ci, carry):
        bias = bias_ref[ci]
        for h in range(ha):
            g = h // per
            kc = ka_ref[chunk(ci), g * LANES:(g + 1) * LANES]
            vc = va_ref[chunk(ci), g * LANES:(g + 1) * LANES]
            lg = _dot_nt(qam_ref[h], kc) * sca + bias
            m_prev = m_ref[h]
            m_new = jnp.maximum(m_prev, jnp.max(lg, axis=1, keepdims=True))
            m_safe = jnp.where(m_new == NEG_INF, 0.0, m_new)
            p = jnp.exp(lg - jnp.tile(m_safe, (1, c // LANES)))
            alpha = jnp.exp(m_prev - m_safe)
            l_ref[h] = alpha * l_ref[h] + jnp.sum(p, axis=1, keepdims=True)
            acc_ref[h] = alpha * acc_ref[h] + _dot(p.astype(BF16), vc)
            m_ref[h] = m_new
        return carry

    lax.fori_loop(0, nch, att_body, 0)
    for g in range(ha // per):
        out_g = jnp.zeros((tq, LANES), F32)
        for hh in range(per):
            h = g * per + hh
            out_g = jnp.where((lane // da) == hh, acc_ref[h] / l_ref[h], out_g)
        o_ref[:, g * LANES:(g + 1) * LANES] = out_g.astype(o_ref.dtype)


def _dsa_prompt(dm, pr, b, t, tq):
    hi, ha, da, di = dm["HI"], dm["HA"], dm["DA"], dm["DI"]
    wa = ha * da
    topk = min(IDX_TOPK_MAX, t // 4)
    assert t % tq == 0 and topk <= tq and tq % LANES == 0
    nq = t // tq
    cfg = dict(tq=tq, topk=topk, HI=hi, HA=ha, DA=da, T=t,
               sc1=di ** -0.5, sc2=hi ** -0.5, sca=da ** -0.5)
    return pl.pallas_call(
        functools.partial(_dsa_prompt_kernel, cfg),
        grid=(b, nq),
        in_specs=[
            pl.BlockSpec((hi, tq, LANES), lambda bi, j: (0, bi * nq + j, 0)),
            pl.BlockSpec((t, LANES), lambda bi, j: (bi, 0)),
            pl.BlockSpec((dm["HIP"], tq), lambda bi, j: (0, bi * nq + j)),
            pl.BlockSpec((ha, tq, LANES), lambda bi, j: (0, bi * nq + j, 0)),
            pl.BlockSpec((t, wa), lambda bi, j: (bi, 0)),
            pl.BlockSpec((t, wa), lambda bi, j: (bi, 0)),
        ],
        out_specs=pl.BlockSpec((tq, wa), lambda bi, j: (bi * nq + j, 0)),
        out_shape=jax.ShapeDtypeStruct((b * t, wa), BF16),
        scratch_shapes=[pltpu.VMEM((t, tq), I32), pltpu.VMEM((nq, tq, tq), F32), pltpu.VMEM((1, tq), I32),
                        pltpu.VMEM((ha, tq, LANES), F32), pltpu.VMEM((ha, tq, LANES), F32),
                        pltpu.VMEM((ha, tq, LANES), F32)],
        compiler_params=_cparams(("parallel", "arbitrary")),
        name="dsa_prompt",
    )(pr["qim"], pr["ki16"], pr["wit"], pr["qam"], pr["ka16"], pr["va16"])


def _mla_prompt_kernel(cfg, q_ref, k_ref, wuv_ref, o_ref, m_ref, l_ref, a_ref, acc_ref, p_ref):
    tq, ck, lora, scale = cfg["tq"], cfg["ck"], cfg["KV"], cfg["scale"]
    hb, _, kc = q_ref.shape
    rows = hb * tq
    j = pl.program_id(1)
    q = q_ref[...].reshape(rows, kc)
    m_ref[...] = jnp.full(m_ref.shape, NEG_INF, F32)
    l_ref[...] = jnp.zeros(l_ref.shape, F32)
    acc_ref[...] = jnp.zeros(acc_ref.shape, F32)
    t_idx = j * tq + lax.broadcasted_iota(I32, (tq, 1), 0)
    col = lax.broadcasted_iota(I32, (1, ck), 1)
    nch = ((j + 1) * tq + ck - 1) // ck

    def body(ci, carry):
        kk = k_ref[pl.ds(pl.multiple_of(ci * ck, ck), ck), :]
        s = _dot_nt(q, kk) * scale
        causal = ci * ck + col <= t_idx
        for h in range(hb):
            r = slice(h * tq, (h + 1) * tq)
            sh = jnp.where(causal, s[r], NEG_INF)
            m_prev = m_ref[r]
            m_new = jnp.maximum(m_prev, jnp.max(sh, axis=1, keepdims=True))
            p = jnp.exp(sh - jnp.tile(m_new, (1, ck // LANES)))
            alpha = jnp.exp(m_prev - m_new)
            l_ref[r] = alpha * l_ref[r] + jnp.sum(p, axis=1, keepdims=True)
            p_ref[r] = p.astype(BF16)
            a_ref[r] = alpha
            m_ref[r] = m_new
        pv = _dot(p_ref[...], kk[:, :lora])
        for h in range(hb):
            r = slice(h * tq, (h + 1) * tq)
            acc_ref[r] = acc_ref[r] * jnp.tile(a_ref[r], (1, lora // LANES)) + pv[r]
        return carry

    lax.fori_loop(0, nch, body, 0)
    out = None
    for h in range(hb):
        r = slice(h * tq, (h + 1) * tq)
        o = (acc_ref[r] / jnp.tile(l_ref[r], (1, lora // LANES))).astype(BF16)
        out = _dot(o, wuv_ref[h]) if out is None else out + _dot(o, wuv_ref[h])
    o_ref[...] = out.astype(o_ref.dtype)


def _pad_wuv(w_uv):
    hb, lora, bv = w_uv.shape
    w = jnp.zeros((hb, lora, hb * bv), F32)
    for h in range(hb):
        w = w.at[h, :, h * bv:(h + 1) * bv].set(w_uv[h])
    return w.astype(BF16)


def _mla_prompt(dm, pr, wuv_pad, b, t, tq, ck):
    hb, lora, nope, ropeb, bv = dm["HB"], dm["KV"], dm["NOPE"], dm["ROPE"], dm["BV"]
    kc = lora + LANES
    assert t % tq == 0 and t % ck == 0 and ck % LANES == 0
    nq = t // tq
    cfg = dict(tq=tq, ck=ck, KV=lora, scale=(nope + ropeb) ** -0.5)
    rows = hb * tq
    return pl.pallas_call(
        functools.partial(_mla_prompt_kernel, cfg),
        grid=(b, nq),
        in_specs=[
            pl.BlockSpec((hb, tq, kc), lambda bi, j: (0, bi * nq + j, 0)),
            pl.BlockSpec((t, kc), lambda bi, j: (bi, 0)),
            pl.BlockSpec(wuv_pad.shape, lambda bi, j: (0, 0, 0)),
        ],
        out_specs=pl.BlockSpec((tq, hb * bv), lambda bi, j: (bi * nq + j, 0)),
        out_shape=jax.ShapeDtypeStruct((b * t, hb * bv), BF16),
        scratch_shapes=[pltpu.VMEM((rows, LANES), F32), pltpu.VMEM((rows, LANES), F32), pltpu.VMEM((rows, LANES), F32),
                        pltpu.VMEM((rows, lora), F32), pltpu.VMEM((rows, ck), BF16)],
        compiler_params=_cparams(("parallel", "arbitrary")),
        name="mla_prompt",
    )(pr["qcat"], pr["kcat16"], wuv_pad)


def _finish_kernel(x_ref, oa_ref, ob_ref, ga_ref, gb_ref, wpa_ref, wpb_ref, wo_ref, gf_ref, wpq_ref,
                   x1_ref, x1n_ref, q_ref):
    merged = ga_ref[...] * _dot(oa_ref[...], wpa_ref[...]) + gb_ref[...] * _dot(ob_ref[...], wpb_ref[...])
    x1 = x_ref[...] + _dot(merged.astype(BF16), wo_ref[...])
    x1_ref[...] = x1
    ms = jnp.mean(x1 * x1, axis=-1, keepdims=True)
    x1n = (x1 * lax.rsqrt(ms + EPS) * gf_ref[...]).astype(BF16)
    x1n_ref[...] = x1n
    q_ref[...] = _dot(x1n, wpq_ref[...]).astype(q_ref.dtype)


def _finish(x2d, oa, ob, ga, gb, wts, tm):
    n, d = x2d.shape
    assert n % tm == 0
    rows = lambda w: pl.BlockSpec((tm, w), lambda i: (i, 0))
    full = lambda a: pl.BlockSpec(a.shape, lambda i: (0,) * a.ndim)
    wpa, wpb, wo, gf, wpq = wts
    dq = wpq.shape[1]
    return pl.pallas_call(
        _finish_kernel,
        grid=(n // tm,),
        in_specs=[rows(d), rows(oa.shape[1]), rows(ob.shape[1]), rows(d), rows(d),
                  full(wpa), full(wpb), full(wo), full(gf), full(wpq)],
        out_specs=[rows(d), rows(d), rows(dq)],
        out_shape=[jax.ShapeDtypeStruct((n, d), F32), jax.ShapeDtypeStruct((n, d), BF16),
                   jax.ShapeDtypeStruct((n, dq), BF16)],
        compiler_params=_cparams(("parallel",)),
        name="finish",
    )(x2d, oa, ob, ga, gb, wpa, wpb, wo, gf, wpq)


def _oddeven_merge(lo, hi, r):
    step = r * 2
    if step < hi - lo:
        yield from _oddeven_merge(lo, hi, step)
        yield from _oddeven_merge(lo + r, hi, step)
        yield from [(i, i + r) for i in range(lo + r, hi - r, step)]
    else:
        yield (lo, lo + r)


def _oddeven_merge_sort(lo, hi):
    if hi - lo >= 1:
        mid = lo + (hi - lo) // 2
        yield from _oddeven_merge_sort(lo, mid)
        yield from _oddeven_merge_sort(mid + 1, hi)
        yield from _oddeven_merge(lo, hi, 1)


SORT16 = tuple(_oddeven_merge_sort(0, PEER_TOPK - 1))
BITONIC16 = tuple((i, i + k) for k in (8, 4, 2, 1) for i in range(PEER_TOPK) if (i & k) == 0)
SUBLANES = 8


def _compare_exchange(x, net):
    for i, j in net:
        x[i], x[j] = jnp.maximum(x[i], x[j]), jnp.minimum(x[i], x[j])
    return x


def _merge_top16(a, b_rev):
    return _compare_exchange([jnp.maximum(x, y) for x, y in zip(a, b_rev)], BITONIC16)


def _top16_sorted(s):
    x = _compare_exchange([s[i * SUBLANES:(i + 1) * SUBLANES, :] for i in range(PEER_TOPK)], SORT16)
    for shift in (4, 6, 7):
        x = _merge_top16(x, [pltpu.roll(x[PEER_TOPK - 1 - i], shift, 0) for i in range(PEER_TOPK)])
    return x


def _peer_route_kernel(cfg, q_ref, k1_ref, k2_ref, s1_ref, s2_ref, e1_ref, e2_ref, thr_ref):
    hp, dk = cfg["HP"], cfg["DK"]
    tn = q_ref.shape[0]
    sub = lax.broadcasted_iota(I32, (SUBLANES, tn), 0)
    v1p = [jnp.zeros((SUBLANES, tn), F32)] * PEER_TOPK
    v2p = [jnp.zeros((SUBLANES, tn), F32)] * PEER_TOPK
    for h in range(hp):
        qh = q_ref[:, h * dk:(h + 1) * dk]
        s1 = _dot_nt(k1_ref[h], qh)
        s2 = _dot_nt(k2_ref[h], qh)
        s1_ref[0, h] = s1
        s2_ref[0, h] = s2
        t1 = _top16_sorted(s1)
        t2 = _top16_sorted(s2)
        v1p = [jnp.where(sub == h, a, b) for a, b in zip(t1, v1p)]
        v2p = [jnp.where(sub == h, a, b) for a, b in zip(t2, v2p)]
    cur = [v1p[0] + v2p[r2] for r2 in range(PEER_TOPK)]
    for r1 in range(1, PEER_TOPK):
        lim = PEER_TOPK // (r1 + 1)
        b_rev = [v1p[r1] + v2p[PEER_TOPK - 1 - i] if i >= PEER_TOPK - lim else None for i in range(PEER_TOPK)]
        cur = _compare_exchange([a if b is None else jnp.maximum(a, b) for a, b in zip(cur, b_rev)], BITONIC16)
    thr = cur[PEER_TOPK - 1]
    z = jnp.zeros((SUBLANES, tn), F32)
    for r in range(PEER_TOPK):
        z = z + jnp.exp(cur[r] - cur[0])
    thr_ref[0] = thr
    rz = 1.0 / z
    for h in range(hp):
        e1_ref[0, h] = jnp.exp(s1_ref[0, h] - v1p[0][h:h + 1, :]) * rz[h:h + 1, :]
        e2_ref[0, h] = jnp.exp(s2_ref[0, h] - v2p[0][h:h + 1, :])


def _peer_route(dm, q, k1p, k2p, tn):
    n = q.shape[0]
    hp, nk, dk = dm["HP"], dm["NK"], dm["DK"]
    assert n % tn == 0 and nk == PEER_TOPK * SUBLANES and hp <= SUBLANES and dk % LANES == 0
    big = pl.BlockSpec((1, hp, nk, tn), lambda i: (i, 0, 0, 0))
    sds = jax.ShapeDtypeStruct((n // tn, hp, nk, tn), F32)
    return pl.pallas_call(
        functools.partial(_peer_route_kernel, dict(HP=hp, DK=dk)),
        grid=(n // tn,),
        in_specs=[pl.BlockSpec((tn, hp * dk), lambda i: (i, 0)),
                  pl.BlockSpec(k1p.shape, lambda i: (0, 0, 0)), pl.BlockSpec(k2p.shape, lambda i: (0, 0, 0))],
        out_specs=[big, big, big, big, pl.BlockSpec((1, SUBLANES, tn), lambda i: (i, 0, 0))],
        out_shape=[sds, sds, sds, sds, jax.ShapeDtypeStruct((n // tn, SUBLANES, tn), F32)],
        compiler_params=_cparams(("parallel",)),
        name="peer_route",
    )(q, k1p, k2p)


def _peer_dense_kernel(cfg, x_ref, x1_ref, u_ref, vt_ref, s1_ref, s2_ref, e1_ref, e2_ref, thr_ref,
                       y_ref, acc_ref, act_ref, g_ref):
    hp, nk = cfg["HP"], cfg["NK"]
    eb = u_ref.shape[0]
    nts, tl = s1_ref.shape[0], s1_ref.shape[-1]
    e = pl.program_id(1)

    @pl.when(e == 0)
    def _():
        acc_ref[...] = jnp.zeros(acc_ref.shape, F32)

    a = _dot_nt(u_ref[...], x_ref[...])
    act_ref[...] = 0.5 * a * (1.0 + lax.erf(a * (2.0 ** -0.5)))

    for ts in range(nts):
        lanes = slice(ts * tl, (ts + 1) * tl)

        def key_body(k, carry, ts=ts, lanes=lanes):
            i1 = e * (eb // nk) + k
            g = None
            for h in range(hp):
                cand = s1_ref[ts, h, pl.ds(i1, 1), :] + s2_ref[ts, h]
                w = jnp.where(cand >= thr_ref[ts, h:h + 1, :], e1_ref[ts, h, pl.ds(i1, 1), :] * e2_ref[ts, h], 0.0)
                g = w if g is None else g + w
            rows = pl.ds(pl.multiple_of(k * nk, nk), nk)
            g_ref[rows, lanes] = (g * act_ref[rows, lanes]).astype(BF16)
            return carry

        lax.fori_loop(0, eb // nk, key_body, 0)
    acc_ref[...] += _dot(vt_ref[...], g_ref[...])

    @pl.when(e == pl.num_programs(1) - 1)
    def _():
        y_ref[...] = x1_ref[...] + acc_ref[...].T


def _peer_dense(dm, x1n, x1, u16, vt16, route, tn, eb):
    n, d = x1.shape
    ne = u16.shape[0]
    hp, nk = dm["HP"], dm["NK"]
    s1, s2, e1, e2, thr = route
    tl = s1.shape[-1]
    assert n % tn == 0 and ne % eb == 0 and eb % nk == 0 and tn % tl == 0
    nts = tn // tl
    big = pl.BlockSpec((nts, hp, nk, tl), lambda i, e: (i, 0, 0, 0))
    return pl.pallas_call(
        functools.partial(_peer_dense_kernel, dict(HP=hp, NK=nk)),
        grid=(n // tn, ne // eb),
        in_specs=[pl.BlockSpec((tn, d), lambda i, e: (i, 0)), pl.BlockSpec((tn, d), lambda i, e: (i, 0)),
                  pl.BlockSpec((eb, d), lambda i, e: (e, 0)), pl.BlockSpec((d, eb), lambda i, e: (0, e)),
                  big, big, big, big, pl.BlockSpec((nts, SUBLANES, tl), lambda i, e: (i, 0, 0))],
        out_specs=pl.BlockSpec((tn, d), lambda i, e: (i, 0)),
        out_shape=jax.ShapeDtypeStruct((n, d), F32),
        scratch_shapes=[pltpu.VMEM((d, tn), F32), pltpu.VMEM((eb, tn), F32), pltpu.VMEM((eb, tn), BF16)],
        compiler_params=_cparams(("parallel", "arbitrary")),
        name="peer_dense",
    )(x1n, x1, u16, vt16, s1, s2, e1, e2, thr)


def _tail(dm, x2d, oa, ob, ga, gb, tail_w, peer_w, tm, tn_route, tn_dense, eb):
    x1, x1n, q = _finish(x2d, oa, ob, ga, gb, tail_w, tm)
    k1p, k2p, u16, vt16 = peer_w
    route = _peer_route(dm, q, k1p, k2p, tn_route)
    return _peer_dense(dm, x1n, x1, u16, vt16, route, tn_dense, eb)


def _page_specs(n, shape, pps):
    return [pl.BlockSpec((1,) + shape, (lambda b, j, pt, k=k: (pt[b, j * pps + k], 0, 0))) for k in range(n)]


def _gather_pages(refs):
    return jnp.concatenate([r[0] for r in refs], axis=0)


def _sample_scores_kernel(cfg, pt_ref, qm_ref, w_ref, *refs):
    pps, slots, hi, sc1, sc2 = cfg["pps"], cfg["slots"], cfg["HI"], cfg["sc1"], cfg["sc2"]
    pages, out_ref = refs[:pps], refs[pps]
    x = _gather_pages(pages).astype(BF16)
    r = _dot_nt(qm_ref[0], x) * sc1
    r = jnp.maximum(r, 0.0) * w_ref[0]
    out_ref[0] = jnp.sum(r.reshape(slots, hi, r.shape[-1]), axis=1) * sc2


def _sample_scores(dm, qm, wcol, idx_view, page_table, pps):
    bs, n_pages = page_table.shape
    hi, di = dm["HI"], dm["DI"]
    slots = LANES // di
    rows = idx_view.shape[1]
    lp = pps * rows
    cfg = dict(pps=pps, slots=slots, HI=hi, sc1=di ** -0.5, sc2=hi ** -0.5)
    grid_spec = pltpu.PrefetchScalarGridSpec(
        num_scalar_prefetch=1, grid=(bs, n_pages // pps),
        in_specs=[pl.BlockSpec((1, slots * hi, LANES), lambda b, j, pt: (b, 0, 0)),
                  pl.BlockSpec((1, slots * hi, lp), lambda b, j, pt: (b, 0, 0))]
                 + _page_specs(pps, (rows, LANES), pps),
        out_specs=pl.BlockSpec((1, slots, lp), lambda b, j, pt: (b, 0, j)))
    return pl.pallas_call(
        functools.partial(_sample_scores_kernel, cfg), grid_spec=grid_spec,
        out_shape=jax.ShapeDtypeStruct((bs, slots, n_pages * rows), F32),
        compiler_params=_cparams(("parallel", "arbitrary")), name="sample_scores",
    )(page_table, qm, wcol, *([idx_view] * pps))


def _sample_select_kernel(cfg, s_ref, qi_ref, ki_ref, w_ref, bd_ref, bias_ref, bnew_ref, keys_ref):
    slots, topk, past, sc1, sc2 = cfg["slots"], cfg["topk"], cfg["past"], cfg["sc1"], cfg["sc2"]
    bsz, width = s_ref.shape[1], s_ref.shape[2]
    prod = qi_ref[...] * ki_ref[...].astype(BF16).astype(F32)
    hs = _dot_hilo(prod, bd_ref[...])
    s_new = jnp.sum(jnp.maximum(hs * sc1, 0.0) * w_ref[...], axis=1, keepdims=True) * sc2
    k_new = _sortable_key(s_new)
    for kk in range(slots):
        keys_ref[kk] = _sortable_key(s_ref[kk])
    lane = lax.broadcasted_iota(I32, (bsz, width), 1)

    def count(ind, ind_new):
        tot = ind_new
        for kk in range(slots):
            tot = tot + jnp.sum(ind(keys_ref[kk], kk), axis=1, keepdims=True)
        return tot

    def bit_body(i, prefix):
        cand_u = prefix | lax.shift_left(jnp.int32(1), 31 - i)
        cand_s = cand_u ^ INT_MIN
        cnt = count(lambda k, kk: jnp.where(k >= cand_s, 1.0, 0.0), jnp.where(k_new >= cand_s, 1.0, 0.0))
        return jnp.where(cnt >= topk, cand_u, prefix)

    thr = lax.fori_loop(0, 32, bit_body, jnp.zeros((bsz, 1), I32)) ^ INT_MIN
    n_gt = count(lambda k, kk: jnp.where(k > thr, 1.0, 0.0), jnp.where(k_new > thr, 1.0, 0.0))
    need = topk - n_gt
    nbits = max(1, past.bit_length())

    def p_body(i, p):
        cand = p | lax.shift_left(jnp.int32(1), nbits - 1 - i)
        f = count(lambda k, kk: jnp.where(k == thr, jnp.where(lane * slots + kk < cand, 1.0, 0.0), 0.0),
                  jnp.where(k_new == thr, jnp.where(past < cand, 1.0, 0.0), 0.0))
        return jnp.where(f < need, cand, p)

    pmax = lax.fori_loop(0, nbits, p_body, jnp.zeros((bsz, 1), I32))
    for kk in range(slots):
        k = keys_ref[kk]
        sel = jnp.where(k > thr, 1, jnp.where(k == thr, jnp.where(lane * slots + kk <= pmax, 1, 0), 0))
        bias_ref[kk] = jnp.where(sel > 0, 0.0, NEG_INF)
    sel_new = jnp.where(k_new > thr, 1, jnp.where(k_new == thr, jnp.where(past <= pmax, 1, 0), 0))
    bnew_ref[...] = jnp.broadcast_to(jnp.where(sel_new > 0, 0.0, NEG_INF), bnew_ref.shape)


def _sample_select(dm, scores_t, qi_flat, ki_rep, w_pad, bd, past):
    slots, bs, width = scores_t.shape
    hi, di = dm["HI"], dm["DI"]
    topk = min(IDX_TOPK_MAX, (past + 1) // 4)
    cfg = dict(slots=slots, topk=topk, past=past, sc1=di ** -0.5, sc2=hi ** -0.5)
    return pl.pallas_call(
        functools.partial(_sample_select_kernel, cfg),
        out_shape=[jax.ShapeDtypeStruct((slots, bs, width), F32), jax.ShapeDtypeStruct((bs, LANES), F32)],
        scratch_shapes=[pltpu.VMEM((slots, bs, width), I32)],
        compiler_params=pltpu.CompilerParams(vmem_limit_bytes=VMEM_LIMIT), name="sample_select",
    )(scores_t, qi_flat, ki_rep, w_pad, bd)


def _online_softmax_step(m_ref, l_ref, acc_ref, lg, pv_fn):
    m_prev = m_ref[...]
    m_new = jnp.maximum(m_prev, jnp.max(lg, axis=1, keepdims=True))
    m_safe = jnp.where(m_new == NEG_INF, 0.0, m_new)
    p = jnp.exp(lg - m_safe)
    alpha = jnp.exp(m_prev - m_safe)
    l_ref[...] = alpha * l_ref[...] + jnp.sum(p, axis=1, keepdims=True)
    acc_ref[...] = alpha * acc_ref[...] + pv_fn(p)
    m_ref[...] = m_new


def _sample_dsa_kernel(cfg, pt_ref, q_ref, bias_ref, bnew_ref, knew_ref, vnew_ref, *refs):
    pps, slots, sca, da = cfg["pps"], cfg["slots"], cfg["sca"], cfg["DA"]
    kpages, vpages = refs[:pps], refs[pps:2 * pps]
    o_ref, m_ref, l_ref, acc_ref = refs[2 * pps:]
    j = pl.program_id(1)
    heads, wa = q_ref.shape[1], q_ref.shape[2]

    @pl.when(j == 0)
    def _():
        m_ref[...] = jnp.full(m_ref.shape, NEG_INF, F32)
        l_ref[...] = jnp.zeros(l_ref.shape, F32)
        acc_ref[...] = jnp.zeros(acc_ref.shape, F32)

    q = q_ref[0]
    kx = _gather_pages(kpages).astype(BF16)
    vx = _gather_pages(vpages).astype(BF16)
    for kk in range(slots):
        lg = _dot_nt(q, kx[:, kk * wa:(kk + 1) * wa]) * sca + bias_ref[0, kk:kk + 1, :]
        _online_softmax_step(m_ref, l_ref, acc_ref, lg,
                             lambda p, kk=kk: _dot(p.astype(BF16), vx[:, kk * wa:(kk + 1) * wa]))

    @pl.when(j == pl.num_programs(1) - 1)
    def _():
        lgn = jnp.sum(q.astype(F32) * knew_ref[0].astype(F32), axis=1, keepdims=True) * sca + bnew_ref[0][:, 0:1]
        _online_softmax_step(m_ref, l_ref, acc_ref, lgn, lambda p: p * vnew_ref[0].astype(F32))
        o = acc_ref[...] / l_ref[...]
        row = lax.broadcasted_iota(I32, (heads, wa), 0)
        lane = lax.broadcasted_iota(I32, (heads, wa), 1)
        o_ref[0] = jnp.sum(jnp.where(lane // da == row, o, 0.0), axis=0, keepdims=True).astype(o_ref.dtype)


def _sample_dsa(dm, qbd, bias, bnew, knew, vnew, k_view, v_view, page_table, pps):
    bs, n_pages = page_table.shape
    ha, da = dm["HA"], dm["DA"]
    wa = ha * da
    rows = k_view.shape[1]
    slots = k_view.shape[2] // wa
    lp = pps * rows
    cfg = dict(pps=pps, slots=slots, sca=da ** -0.5, DA=da)
    per_b = lambda shape: pl.BlockSpec((1,) + shape, lambda b, j, pt: (b, 0, 0))
    grid_spec = pltpu.PrefetchScalarGridSpec(
        num_scalar_prefetch=1, grid=(bs, n_pages // pps),
        in_specs=[per_b((ha, wa)), pl.BlockSpec((1, slots, lp), lambda b, j, pt: (b, 0, j)),
                  per_b((1, LANES)), per_b((1, wa)), per_b((1, wa))]
                 + _page_specs(pps, (rows, slots * wa), pps) + _page_specs(pps, (rows, slots * wa), pps),
        out_specs=per_b((1, wa)),
        scratch_shapes=[pltpu.VMEM((ha, 1), F32), pltpu.VMEM((ha, 1), F32), pltpu.VMEM((ha, wa), F32)])
    return pl.pallas_call(
        functools.partial(_sample_dsa_kernel, cfg), grid_spec=grid_spec,
        out_shape=jax.ShapeDtypeStruct((bs, 1, wa), BF16),
        compiler_params=_cparams(("parallel", "arbitrary")), name="sample_dsa",
    )(page_table, qbd, bias, bnew, knew, vnew, *([k_view] * pps), *([v_view] * pps))


def _sample_mla_kernel(cfg, pt_ref, qabs_ref, qpe_ref, qfull_ref, knew_ref, wuv_ref, *refs):
    pps, slots, lora, scale = cfg["pps"], cfg["slots"], cfg["KV"], cfg["scale"]
    cpages, ppages = refs[:pps], refs[pps:2 * pps]
    o_ref, m_ref, l_ref, acc_ref = refs[2 * pps:]
    j = pl.program_id(1)
    heads = qabs_ref.shape[1]

    @pl.when(j == 0)
    def _():
        m_ref[...] = jnp.full(m_ref.shape, NEG_INF, F32)
        l_ref[...] = jnp.zeros(l_ref.shape, F32)
        acc_ref[...] = jnp.zeros(acc_ref.shape, F32)

    qabs = qabs_ref[0]
    cx = _gather_pages(cpages).astype(BF16)
    px = _gather_pages(ppages).astype(BF16)
    for kk in range(slots):
        ck = cx[:, kk * lora:(kk + 1) * lora]
        lg = (_dot_nt(qabs, ck) + _dot_nt(qpe_ref[0, kk], px)) * scale
        _online_softmax_step(m_ref, l_ref, acc_ref, lg, lambda p, ck=ck: _dot(p.astype(BF16), ck))

    @pl.when(j == pl.num_programs(1) - 1)
    def _():
        kn = knew_ref[0].astype(F32)
        lgn = jnp.sum(qfull_ref[0].astype(F32) * kn, axis=1, keepdims=True) * scale
        _online_softmax_step(m_ref, l_ref, acc_ref, lgn, lambda p: p * kn[:, :lora])
        o = (acc_ref[...] / l_ref[...]).astype(BF16)
        row = lax.broadcasted_iota(I32, (heads, o_ref.shape[-1]), 0)
        out = jnp.zeros((heads, o_ref.shape[-1]), F32)
        for h in range(heads):
            out = out + jnp.where(row == h, _dot(o, wuv_ref[h]), 0.0)
        o_ref[0] = jnp.sum(out, axis=0, keepdims=True).astype(o_ref.dtype)


def _sample_mla(dm, qabs, qpe, qfull, knew, wuv_pad, ckv_view, kpe_view, page_table, pps):
    bs, n_pages = page_table.shape
    hb, lora, nope, ropeb, bv = dm["HB"], dm["KV"], dm["NOPE"], dm["ROPE"], dm["BV"]
    rows = ckv_view.shape[1]
    slots = ckv_view.shape[2] // lora
    cfg = dict(pps=pps, slots=slots, KV=lora, scale=(nope + ropeb) ** -0.5)
    per_b = lambda shape: pl.BlockSpec((1,) + shape, lambda b, j, pt: (b,) + (0,) * len(shape))
    grid_spec = pltpu.PrefetchScalarGridSpec(
        num_scalar_prefetch=1, grid=(bs, n_pages // pps),
        in_specs=[per_b((hb, lora)), per_b((slots, hb, LANES)), per_b((hb, lora + LANES)), per_b((1, lora + LANES)),
                  pl.BlockSpec(wuv_pad.shape, lambda b, j, pt: (0, 0, 0))]
                 + _page_specs(pps, (rows, slots * lora), pps) + _page_specs(pps, (rows, LANES), pps),
        out_specs=per_b((1, hb * bv)),
        scratch_shapes=[pltpu.VMEM((hb, 1), F32), pltpu.VMEM((hb, 1), F32), pltpu.VMEM((hb, lora), F32)])
    return pl.pallas_call(
        functools.partial(_sample_mla_kernel, cfg), grid_spec=grid_spec,
        out_shape=jax.ShapeDtypeStruct((bs, 1, hb * bv), BF16),
        compiler_params=_cparams(("parallel", "arbitrary")), name="sample_mla",
    )(page_table, qabs, qpe, qfull, knew, wuv_pad, *([ckv_view] * pps), *([kpe_view] * pps))


def _unmask_heads(xm, d):
    per = LANES // d
    return jnp.stack([xm[h][:, (h % per) * d:(h % per + 1) * d] for h in range(xm.shape[0])], axis=1)


def _slot_shifted(x, slots):
    eye = jnp.eye(slots, dtype=x.dtype)
    n, h, d = x.shape
    return jnp.einsum("nhd,kj->nkhjd", x, eye).reshape(n, slots, h, slots * d)


def _sample_attention(dm, ps, cache_k, cache_v, cache_idx_k, cache_ckv, cache_kpe, page_table, wuv_pad):
    bs, n_pages = page_table.shape
    n_pool, page = cache_k.shape[0], cache_k.shape[1]
    past = page * n_pages
    ha, da, hi, di = dm["HA"], dm["DA"], dm["HI"], dm["DI"]
    hb, lora, ropeb = dm["HB"], dm["KV"], dm["ROPE"]
    wa = ha * da
    slots = LANES // di
    assert slots == LANES // ropeb and page % slots == 0
    rows = page // slots
    pps_small = math.gcd(n_pages, 16)
    pps_big = math.gcd(n_pages, 8)
    idx_view = cache_idx_k.reshape(n_pool, rows, slots * di)
    k_view = cache_k.reshape(n_pool, rows, slots * wa)
    v_view = cache_v.reshape(n_pool, rows, slots * wa)
    ckv_view = cache_ckv.reshape(n_pool, rows, slots * lora)
    kpe_view = cache_kpe.reshape(n_pool, rows, slots * ropeb)

    qi = _unmask_heads(ps["qim"], di)
    w_i = ps["wit"][:hi].T
    qm = _slot_shifted(qi, slots).reshape(bs, slots * hi, LANES)
    wcol = jnp.broadcast_to(jnp.tile(w_i, (1, slots))[:, :, None], (bs, slots * hi, pps_small * rows))
    scores = _sample_scores(dm, qm, wcol, idx_view, page_table, pps_small)
    bd = _block_diag_ones(hi * di, di, LANES, 1)
    w_pad = jnp.zeros((bs, LANES), F32).at[:, :hi].set(w_i)
    ki_rep = jnp.tile(ps["ki32"][:, :di], (1, hi))
    bias_t, bnew = _sample_select(dm, jnp.transpose(scores, (1, 0, 2)), qi.reshape(bs, hi * di).astype(F32),
                                  ki_rep, w_pad, bd, past)
    bias = jnp.transpose(bias_t, (1, 0, 2))

    per = LANES // da
    qbd = jnp.zeros((bs, ha, wa), BF16)
    for h in range(ha):
        g = h // per
        qbd = qbd.at[:, h, g * LANES:(g + 1) * LANES].set(ps["qam"][h])
    oa = _sample_dsa(dm, qbd, bias, bnew.reshape(bs, 1, LANES), ps["ka16"].reshape(bs, 1, wa),
                     ps["va16"].reshape(bs, 1, wa), k_view, v_view, page_table, pps_big)

    qfull = jnp.transpose(ps["qcat"], (1, 0, 2))
    qpe = _slot_shifted(_unmask_heads(ps["qcat"][:, :, lora:], ropeb), slots)
    ob = _sample_mla(dm, qfull[:, :, :lora], qpe, qfull, ps["kcat16"].reshape(bs, 1, lora + LANES), wuv_pad,
                     ckv_view, kpe_view, page_table, pps_small)
    return oa.reshape(bs, wa), ob.reshape(bs, -1)


def _model_dims(cache_k, cache_idx_k, cache_ckv, cache_kpe, w_in, g_cq, w_uq, w_uk, w_uv, sub_k1):
    ha, da = cache_k.shape[2], cache_k.shape[3]
    di, lora, ropeb = cache_idx_k.shape[-1], cache_ckv.shape[-1], cache_kpe.shape[-1]
    ql, hb, nope, bv = g_cq.shape[0], w_uq.shape[1], w_uk.shape[1], w_uv.shape[2]
    d = w_in.shape[0]
    rest = w_in.shape[1] - (3 * ha * da + di + ql + lora + ropeb + 2 * d)
    hi = rest // (di + 1)
    assert hi * (di + 1) == rest
    for v in (da, di, ropeb):
        assert LANES % v == 0
    for v in (ha * da, hi * di, hb * nope, hb * ropeb, lora, ql, d):
        assert v % LANES == 0
    return dict(HA=ha, DA=da, HI=hi, DI=di, HB=hb, NOPE=nope, ROPE=ropeb, KV=lora, QL=ql, BV=bv, D=d,
                HP=sub_k1.shape[0], NK=sub_k1.shape[1], DK=2 * sub_k1.shape[2])


def kernel(x_prompt, x_sample, cache_k, cache_v, cache_idx_k, cache_ckv, cache_kpe, page_table, g_attn, w_in, g_qa, g_ka, g_cq, w_uq, g_qb, w_uk, g_ckv, g_kpe, w_uv, w_pa, w_pb, w_o, g_ffn, w_pq, sub_k1, sub_k2, u_tab, v_tab):
    dm = _model_dims(cache_k, cache_idx_k, cache_ckv, cache_kpe, w_in, g_cq, w_uq, w_uk, w_uv, sub_k1)
    dm, consts = _prep_proj_weights(dm, g_attn, w_in, g_qa, g_ka, g_cq, w_uq, g_qb, w_uk, g_ckv, g_kpe)
    b, t, d = x_prompt.shape
    bs, ts, _ = x_sample.shape
    page, n_pages = cache_k.shape[1], page_table.shape[1]
    past = page * n_pages
    ha, da, di, lora, ropeb = dm["HA"], dm["DA"], dm["DI"], dm["KV"], dm["ROPE"]
    hp, nk, dk = dm["HP"], dm["NK"], dm["DK"]

    wuv_pad = _pad_wuv(w_uv)
    tail_w = (w_pa.astype(BF16), w_pb.astype(BF16), w_o.astype(BF16), g_ffn.reshape(1, -1), w_pq.astype(BF16))
    zeros_half = jnp.zeros((hp, nk, dk // 2), F32)
    k1p = jnp.concatenate([sub_k1, zeros_half], axis=-1).astype(BF16)
    k2p = jnp.concatenate([zeros_half, sub_k2], axis=-1).astype(BF16)
    peer_w = (k1p, k2p, u_tab.astype(BF16), v_tab.T.astype(BF16))

    n_p = b * t
    x_p = x_prompt.reshape(n_p, d)
    pp = _project(dm, consts, x_p, jnp.arange(t, dtype=I32), min(256, t))
    oa_p = _dsa_prompt(dm, pp, b, t, min(256, t))
    ob_p = _mla_prompt(dm, pp, wuv_pad, b, t, min(128, t), min(256, t))
    y_p = _tail(dm, x_p, oa_p, ob_p, pp["ga"], pp["gb"], tail_w, peer_w, min(256, n_p), 128, min(512, n_p), 512)

    n_s = bs * ts
    assert ts == 1
    x_s = x_sample.reshape(n_s, d)
    ps = _project(dm, consts, x_s, jnp.full((n_s,), past, I32), n_s)
    oa_s, ob_s = _sample_attention(dm, ps, cache_k, cache_v, cache_idx_k, cache_ckv, cache_kpe, page_table, wuv_pad)
    y_s = _tail(dm, x_s, oa_s, ob_s, ps["ga"], ps["gb"], tail_w, peer_w, n_s, n_s, n_s, 512)

    def kv_outs(p, bb, tt):
        return (p["ka32"].reshape(bb, tt, ha, da), p["va32"].reshape(bb, tt, ha, da),
                p["ki32"][:, :di].reshape(bb, tt, di), p["ckv32"].reshape(bb, tt, lora),
                p["kpe32"][:, :ropeb].reshape(bb, tt, ropeb))

    return (y_p.reshape(b, t, d), y_s.reshape(bs, ts, d)) + kv_outs(pp, b, t) + kv_outs(ps, bs, ts)
```

```python
import functools
import math

import numpy as np
import jax
import jax.numpy as jnp
from jax import lax
from jax.experimental import pallas as pl
from jax.experimental.pallas import tpu as pltpu

F32 = jnp.float32
BF16 = jnp.bfloat16
I32 = jnp.int32

ROPE_THETA = 10000.0
EPS = 1e-6
IDX_TOPK_MAX = 256
PEER_TOPK = 16
LANES = 128
NEG_INF = float("-inf")
VMEM_LIMIT = 56 * 1024 * 1024


def _cparams(sem):
    return pltpu.CompilerParams(dimension_semantics=sem, vmem_limit_bytes=VMEM_LIMIT)


def _dot(a, b):
    return jnp.dot(a, b, preferred_element_type=F32)


def _dot_nt(a, b):
    return lax.dot_general(a, b, (((1,), (1,)), ((), ())), preferred_element_type=F32)


def _dot_hilo(a, b):
    hi = a.astype(BF16)
    lo = (a - hi.astype(F32)).astype(BF16)
    return _dot(hi, b) + _dot(lo, b)


def _rope(x, c, s, half):
    w = x.shape[-1]
    fwd = pltpu.roll(x, half, 1)
    bwd = pltpu.roll(x, w - half, 1)
    lane = lax.broadcasted_iota(I32, (1, w), 1)
    first = (lane & (2 * half - 1)) < half
    return x * c + jnp.where(first, bwd, fwd) * s


def _group_masked(x, h, d):
    per = LANES // d
    g = h // per
    grp = x[:, g * LANES:(g + 1) * LANES]
    lane = lax.broadcasted_iota(I32, (1, LANES), 1)
    return jnp.where((lane // d) == (h % per), grp, 0.0)


def _proj_kernel(dm, x_ref, gat_ref, w_ref, wwi_ref, c64_ref, s64_ref, c32_ref, s32_ref,
                 gqa_ref, gka_ref, gcq_ref, gqbn_ref, gqbr_ref, gckv_ref, gkpe_ref,
                 bda_ref, bnr_ref, brn_ref, bdr_ref, wuq_ref, wuk_ref,
                 ka32, va32, ki32, ckv32, kpe32, ga, gb, ka16, va16, ki16, kcat16,
                 qam, qim, wit, qcat):
    off = dm["off"]
    x = x_ref[...]
    ms = jnp.mean(x * x, axis=-1, keepdims=True)
    h = (x * lax.rsqrt(ms + EPS) * gat_ref[...]).astype(BF16)

    def seg(name, n):
        a = off[name]
        return _dot(h, w_ref[:, a:a + n])

    ha, da, hi, di = dm["HA"], dm["DA"], dm["HI"], dm["DI"]
    hb, nope, ropeb, lora = dm["HB"], dm["NOPE"], dm["ROPE"], dm["KV"]
    wa = ha * da
    c64, s64 = c64_ref[...], s64_ref[...]
    c32, s32 = c32_ref[...], s32_ref[...]

    z = seg("qa", wa)
    q = z * lax.rsqrt(_dot_hilo(z * z, bda_ref[...]) * (1.0 / da) + EPS) * gqa_ref[...]
    q = _rope(q, c64, s64, da // 2)
    for hh in range(ha):
        qam[hh] = _group_masked(q, hh, da).astype(BF16)
    z = seg("ka", wa)
    k = z * lax.rsqrt(_dot_hilo(z * z, bda_ref[...]) * (1.0 / da) + EPS) * gka_ref[...]
    k = _rope(k, c64, s64, da // 2)
    ka32[...] = k
    ka16[...] = k.astype(BF16)
    z = seg("va", wa)
    va32[...] = z
    va16[...] = z.astype(BF16)

    z = seg("qi", hi * di)
    q = _rope(z, c32, s32, di // 2)
    for hh in range(hi):
        qim[hh] = _group_masked(q, hh, di).astype(BF16)
    wit[...] = _dot_nt(wwi_ref[...], h)
    z = seg("ki", LANES)
    k = _rope(z, c32[:, :LANES], s32[:, :LANES], di // 2)
    ki32[...] = k
    ki16[...] = k.astype(BF16)

    z = seg("cq", dm["QL"])
    zn = (z * lax.rsqrt(jnp.mean(z * z, axis=-1, keepdims=True) + EPS) * gcq_ref[...]).astype(BF16)
    qb = _dot(zn, wuq_ref[...])
    wn = hb * nope
    qn, qr = qb[:, :wn], qb[:, wn:]
    zzn, zzr = qn * qn, qr * qr
    totn = _dot_hilo(zzn, bda_ref[...]) + _dot_hilo(zzr, brn_ref[...])
    totr = _dot_hilo(zzn, bnr_ref[...]) + _dot_hilo(zzr, bdr_ref[...])
    inv_d = 1.0 / (nope + ropeb)
    qn = qn * lax.rsqrt(totn * inv_d + EPS) * gqbn_ref[...]
    qr = qr * lax.rsqrt(totr * inv_d + EPS) * gqbr_ref[...]
    qpe = _rope(qr, c32, s32, ropeb // 2)
    qabs = _dot(qn.astype(BF16), wuk_ref[...])
    for hh in range(hb):
        qcat[hh, :, 0:lora] = qabs[:, hh * lora:(hh + 1) * lora].astype(BF16)
        qcat[hh, :, lora:lora + LANES] = _group_masked(qpe, hh, ropeb).astype(BF16)

    z = seg("ckv", lora)
    ckv = z * lax.rsqrt(jnp.mean(z * z, axis=-1, keepdims=True) + EPS) * gckv_ref[...]
    ckv32[...] = ckv
    kcat16[:, 0:lora] = ckv.astype(BF16)
    z = seg("kpe", LANES)
    kp = z * lax.rsqrt(jnp.mean(z * z, axis=-1, keepdims=True) + EPS) * gkpe_ref[...]
    kp = _rope(kp, c32[:, :LANES], s32[:, :LANES], ropeb // 2)
    kpe32[...] = kp
    kcat16[:, lora:lora + LANES] = kp.astype(BF16)

    d = x.shape[-1]
    ga[...] = jax.nn.sigmoid(seg("ga", d))
    gb[...] = jax.nn.sigmoid(seg("gb", d))


def _rope_tables(pos, d, reps):
    half = d // 2
    inv = ROPE_THETA ** (-(jnp.arange(half, dtype=F32) * 2.0 / d))
    ang = pos.astype(F32)[:, None] * inv[None, :]
    cos, sin = jnp.cos(ang), jnp.sin(ang)
    c = jnp.concatenate([cos, cos], axis=-1)
    s = jnp.concatenate([-sin, sin], axis=-1)
    return jnp.tile(c, (1, reps)), jnp.tile(s, (1, reps))


def _block_diag_ones(rows, rd, cols, cd):
    r = np.arange(rows)[:, None] // rd
    c = np.arange(cols)[None, :] // cd
    return jnp.asarray((r == c).astype(np.float32), dtype=BF16)


def _prep_proj_weights(dm, g_attn, w_in, g_qa, g_ka, g_cq, w_uq, g_qb, w_uk, g_ckv, g_kpe):
    ha, da, hi, di = dm["HA"], dm["DA"], dm["HI"], dm["DI"]
    hb, nope, ropeb, lora, ql = dm["HB"], dm["NOPE"], dm["ROPE"], dm["KV"], dm["QL"]
    d = w_in.shape[0]
    sizes = (ha * da, ha * da, ha * da, hi * di, hi, di, ql, lora, ropeb, d, d)
    parts, o = [], 0
    for n in sizes:
        parts.append(w_in[:, o:o + n])
        o += n
    w_qa, w_ka, w_va, w_qi, w_wi, w_ki, w_cq, w_ckv, w_kpe, w_ga, w_gb = parts
    segs = [("qa", w_qa), ("ka", w_ka), ("va", w_va), ("qi", w_qi),
            ("ki", jnp.tile(w_ki, (1, LANES // di))), ("cq", w_cq), ("ckv", w_ckv),
            ("kpe", jnp.tile(w_kpe, (1, LANES // ropeb))), ("ga", w_ga), ("gb", w_gb)]
    off, o = {}, 0
    for name, w in segs:
        off[name] = o
        o += w.shape[1]
    w_all = jnp.concatenate([w for _, w in segs], axis=1).astype(BF16)
    hip = -(-hi // 8) * 8
    wwi = jnp.zeros((hip, d), F32).at[:hi].set(w_wi.T).astype(BF16)
    wuq = jnp.concatenate([w_uq[:, :, :nope].reshape(ql, hb * nope),
                           w_uq[:, :, nope:].reshape(ql, hb * ropeb)], axis=1).astype(BF16)
    wuk = jnp.zeros((hb * nope, hb * lora), F32)
    for hh in range(hb):
        wuk = wuk.at[hh * nope:(hh + 1) * nope, hh * lora:(hh + 1) * lora].set(w_uk[hh])
    wuk = wuk.astype(BF16)
    row = lambda v: v.reshape(1, -1).astype(F32)
    consts = dict(
        gat=row(g_attn), w=w_all, wwi=wwi,
        gqa=row(jnp.tile(g_qa, ha)), gka=row(jnp.tile(g_ka, ha)), gcq=row(g_cq),
        gqbn=row(jnp.tile(g_qb[:nope], hb)), gqbr=row(jnp.tile(g_qb[nope:], hb)),
        gckv=row(g_ckv), gkpe=row(jnp.tile(g_kpe, LANES // ropeb)),
        bda=_block_diag_ones(ha * da, da, ha * da, da),
        bnr=_block_diag_ones(hb * nope, nope, hb * ropeb, ropeb),
        brn=_block_diag_ones(hb * ropeb, ropeb, hb * nope, nope),
        bdr=_block_diag_ones(hb * ropeb, ropeb, hb * ropeb, ropeb),
        wuq=wuq, wuk=wuk)
    dm = dict(dm, off=off, HIP=hip, WTOT=o)
    return dm, consts


def _project(dm, consts, x2d, pos, tm):
    n, d = x2d.shape
    p = pos.shape[0]
    assert n % tm == 0 and p % tm == 0
    npb = p // tm
    ha, da, hi, di = dm["HA"], dm["DA"], dm["HI"], dm["DI"]
    hb, ropeb, lora = dm["HB"], dm["ROPE"], dm["KV"]
    wa = ha * da
    c64, s64 = _rope_tables(pos, da, ha)
    c32, s32 = _rope_tables(pos, di, hi)
    full = lambda a: pl.BlockSpec(a.shape, lambda i: (0,) * a.ndim)
    rows = lambda w: pl.BlockSpec((tm, w), lambda i: (i, 0))
    prow = lambda w: pl.BlockSpec((tm, w), lambda i: (i % npb, 0))
    hrows = lambda hh, w: pl.BlockSpec((hh, tm, w), lambda i: (0, i, 0))
    cn = ["gat", "w", "wwi"]
    cg = ["gqa", "gka", "gcq", "gqbn", "gqbr", "gckv", "gkpe", "bda", "bnr", "brn", "bdr", "wuq", "wuk"]
    ins = [x2d] + [consts[k] for k in cn] + [c64, s64, c32, s32] + [consts[k] for k in cg]
    in_specs = ([rows(d)] + [full(consts[k]) for k in cn] + [prow(wa), prow(wa), prow(hi * di), prow(hi * di)]
                + [full(consts[k]) for k in cg])
    sd = jax.ShapeDtypeStruct
    outs = [
        (sd((n, wa), F32), rows(wa)), (sd((n, wa), F32), rows(wa)), (sd((n, LANES), F32), rows(LANES)),
        (sd((n, lora), F32), rows(lora)), (sd((n, LANES), F32), rows(LANES)),
        (sd((n, d), F32), rows(d)), (sd((n, d), F32), rows(d)),
        (sd((n, wa), BF16), rows(wa)), (sd((n, wa), BF16), rows(wa)), (sd((n, LANES), BF16), rows(LANES)),
        (sd((n, lora + LANES), BF16), rows(lora + LANES)),
        (sd((ha, n, LANES), BF16), hrows(ha, LANES)), (sd((hi, n, LANES), BF16), hrows(hi, LANES)),
        (sd((dm["HIP"], n), F32), pl.BlockSpec((dm["HIP"], tm), lambda i: (0, i))),
        (sd((hb, n, lora + LANES), BF16), hrows(hb, lora + LANES)),
    ]
    res = pl.pallas_call(
        functools.partial(_proj_kernel, dm),
        grid=(n // tm,),
        in_specs=in_specs,
        out_specs=[o[1] for o in outs],
        out_shape=[o[0] for o in outs],
        compiler_params=_cparams(("parallel",)),
        name="proj",
    )(*ins)
    names = ["ka32", "va32", "ki32", "ckv32", "kpe32", "ga", "gb", "ka16", "va16", "ki16", "kcat16",
             "qam", "qim", "wit", "qcat"]
    return dict(zip(names, res))


INT_MIN = -2147483648


def _sortable_key(x):
    bits = pltpu.bitcast(x, I32)
    return bits ^ ((bits >> 31) & 0x7FFFFFFF)


def _dsa_prompt_kernel(cfg, qim_ref, ki_ref, wit_ref, qam_ref, ka_ref, va_ref, o_ref,
                       keys_ref, bias_ref, pmax_ref, m_ref, l_ref, acc_ref):
    tq, topk, hi, ha, da, t_total = cfg["tq"], cfg["topk"], cfg["HI"], cfg["HA"], cfg["DA"], cfg["T"]
    sc1, sc2, sca = cfg["sc1"], cfg["sc2"], cfg["sca"]
    c = tq
    j = pl.program_id(1)
    nch = j + 1
    row_i = lax.broadcasted_iota(I32, (c, tq), 0)
    col_i = lax.broadcasted_iota(I32, (c, tq), 1)
    t_idx = j * tq + col_i

    def chunk(ci):
        return pl.ds(pl.multiple_of(ci * c, c), c)

    def score_body(ci, carry):
        kc = ki_ref[chunk(ci), :]
        acc = jnp.zeros((c, tq), F32)
        for h in range(hi):
            s = _dot_nt(kc, qim_ref[h]) * sc1
            acc = acc + jnp.maximum(s, 0.0) * wit_ref[h:h + 1, :]
        score = acc * sc2
        score = jnp.where(ci * c + row_i <= t_idx, score, NEG_INF)
        keys_ref[chunk(ci), :] = _sortable_key(score)
        return carry

    lax.fori_loop(0, nch, score_body, 0)

    def count(ind):
        def body(ci, acc):
            kk = keys_ref[chunk(ci), :]
            return acc + jnp.sum(ind(kk, ci), axis=0, keepdims=True)
        return lax.fori_loop(0, nch, body, jnp.zeros((1, tq), I32))

    def bit_body(i, prefix):
        cand_u = prefix | lax.shift_left(jnp.int32(1), 31 - i)
        cand_s = cand_u ^ INT_MIN
        cnt = count(lambda kk, ci: jnp.where(kk >= cand_s, 1, 0))
        return jnp.where(cnt >= topk, cand_u, prefix)

    thr = lax.fori_loop(0, 32, bit_body, jnp.zeros((1, tq), I32)) ^ INT_MIN
    n_gt = count(lambda kk, ci: jnp.where(kk > thr, 1, 0))
    n_ge = count(lambda kk, ci: jnp.where(kk >= thr, 1, 0))
    need = topk - n_gt

    pmax_ref[...] = jnp.full((1, tq), t_total, I32)

    @pl.when(jnp.max(n_ge) > topk)
    def _():
        nbits = max(1, (t_total - 1).bit_length())

        def p_body(i, p):
            cand = p | lax.shift_left(jnp.int32(1), nbits - 1 - i)
            f = count(lambda kk, ci: jnp.where(kk == thr, jnp.where(ci * c + row_i < cand, 1, 0), 0))
            return jnp.where(f < need, cand, p)

        pmax_ref[...] = lax.fori_loop(0, nbits, p_body, jnp.zeros((1, tq), I32))

    pmax = pmax_ref[...]

    def bias_body(ci, carry):
        kk = keys_ref[chunk(ci), :]
        s_idx = ci * c + row_i
        sel = jnp.where(kk > thr, 1, jnp.where(kk == thr, jnp.where(s_idx <= pmax, 1, 0), 0))
        ok = jnp.where(s_idx <= t_idx, sel, 0)
        bias_ref[ci] = jnp.where(ok > 0, 0.0, NEG_INF).T
        return carry

    lax.fori_loop(0, nch, bias_body, 0)

    per = LANES // da
    lane = lax.broadcasted_iota(I32, (1, LANES), 1)
    m_ref[...] = jnp.full(m_ref.shape, NEG_INF, F32)
    l_ref[...] = jnp.zeros(l_ref.shape, F32)
    acc_ref[...] = jnp.zeros(acc_ref.shape, F32)

    def att_body(ci, carry):
        bias = bias_ref[ci]
        for h in range(ha):
            g = h // per
            kc = ka_ref[chunk(ci), g * LANES:(g + 1) * LANES]
            vc = va_ref[chunk(ci), g * LANES:(g + 1) * LANES]
            lg = _dot_nt(qam_ref[h], kc) * sca + bias
            m_prev = m_ref[h]
            m_new = jnp.maximum(m_prev, jnp.max(lg, axis=1, keepdims=True))
            m_safe = jnp.where(m_new == NEG_INF, 0.0, m_new)
            p = jnp.exp(lg - jnp.tile(m_safe, (1, c // LANES)))
            alpha = jnp.exp(m_prev - m_safe)
            l_ref[h] = alpha * l_ref[h] + jnp.sum(p, axis=1, keepdims=True)
            acc_ref[h] = alpha * acc_ref[h] + _dot(p.astype(BF16), vc)
            m_ref[h] = m_new
        return carry

    lax.fori_loop(0, nch, att_body, 0)
    for g in range(ha // per):
        out_g = jnp.zeros((tq, LANES), F32)
        for hh in range(per):
            h = g * per + hh
            out_g = jnp.where((lane // da) == hh, acc_ref[h] / l_ref[h], out_g)
        o_ref[:, g * LANES:(g + 1) * LANES] = out_g.astype(o_ref.dtype)


def _dsa_prompt(dm, pr, b, t, tq):
    hi, ha, da, di = dm["HI"], dm["HA"], dm["DA"], dm["DI"]
    wa = ha * da
    topk = min(IDX_TOPK_MAX, t // 4)
    assert t % tq == 0 and topk <= tq and tq % LANES == 0
    nq = t // tq
    cfg = dict(tq=tq, topk=topk, HI=hi, HA=ha, DA=da, T=t,
               sc1=di ** -0.5, sc2=hi ** -0.5, sca=da ** -0.5)
    return pl.pallas_call(
        functools.partial(_dsa_prompt_kernel, cfg),
        grid=(b, nq),
        in_specs=[
            pl.BlockSpec((hi, tq, LANES), lambda bi, j: (0, bi * nq + j, 0)),
            pl.BlockSpec((t, LANES), lambda bi, j: (bi, 0)),
            pl.BlockSpec((dm["HIP"], tq), lambda bi, j: (0, bi * nq + j)),
            pl.BlockSpec((ha, tq, LANES), lambda bi, j: (0, bi * nq + j, 0)),
            pl.BlockSpec((t, wa), lambda bi, j: (bi, 0)),
            pl.BlockSpec((t, wa), lambda bi, j: (bi, 0)),
        ],
        out_specs=pl.BlockSpec((tq, wa), lambda bi, j: (bi * nq + j, 0)),
        out_shape=jax.ShapeDtypeStruct((b * t, wa), BF16),
        scratch_shapes=[pltpu.VMEM((t, tq), I32), pltpu.VMEM((nq, tq, tq), F32), pltpu.VMEM((1, tq), I32),
                        pltpu.VMEM((ha, tq, LANES), F32), pltpu.VMEM((ha, tq, LANES), F32),
                        pltpu.VMEM((ha, tq, LANES), F32)],
        compiler_params=_cparams(("parallel", "arbitrary")),
        name="dsa_prompt",
    )(pr["qim"], pr["ki16"], pr["wit"], pr["qam"], pr["ka16"], pr["va16"])


def _mla_prompt_kernel(cfg, q_ref, k_ref, wuv_ref, o_ref, m_ref, l_ref, a_ref, acc_ref, p_ref):
    tq, ck, lora, scale = cfg["tq"], cfg["ck"], cfg["KV"], cfg["scale"]
    hb, _, kc = q_ref.shape
    rows = hb * tq
    j = pl.program_id(1)
    q = q_ref[...].reshape(rows, kc)
    m_ref[...] = jnp.full(m_ref.shape, NEG_INF, F32)
    l_ref[...] = jnp.zeros(l_ref.shape, F32)
    acc_ref[...] = jnp.zeros(acc_ref.shape, F32)
    t_idx = j * tq + lax.broadcasted_iota(I32, (tq, 1), 0)
    col = lax.broadcasted_iota(I32, (1, ck), 1)
    nch = ((j + 1) * tq + ck - 1) // ck

    def body(ci, carry):
        kk = k_ref[pl.ds(pl.multiple_of(ci * ck, ck), ck), :]
        s = _dot_nt(q, kk) * scale
        causal = ci * ck + col <= t_idx
        for h in range(hb):
            r = slice(h * tq, (h + 1) * tq)
            sh = jnp.where(causal, s[r], NEG_INF)
            m_prev = m_ref[r]
            m_new = jnp.maximum(m_prev, jnp.max(sh, axis=1, keepdims=True))
            p = jnp.exp(sh - jnp.tile(m_new, (1, ck // LANES)))
            alpha = jnp.exp(m_prev - m_new)
            l_ref[r] = alpha * l_ref[r] + jnp.sum(p, axis=1, keepdims=True)
            p_ref[r] = p.astype(BF16)
            a_ref[r] = alpha
            m_ref[r] = m_new
        pv = _dot(p_ref[...], kk[:, :lora])
        for h in range(hb):
            r = slice(h * tq, (h + 1) * tq)
            acc_ref[r] = acc_ref[r] * jnp.tile(a_ref[r], (1, lora // LANES)) + pv[r]
        return carry

    lax.fori_loop(0, nch, body, 0)
    out = None
    for h in range(hb):
        r = slice(h * tq, (h + 1) * tq)
        o = (acc_ref[r] / jnp.tile(l_ref[r], (1, lora // LANES))).astype(BF16)
        out = _dot(o, wuv_ref[h]) if out is None else out + _dot(o, wuv_ref[h])
    o_ref[...] = out.astype(o_ref.dtype)


def _pad_wuv(w_uv):
    hb, lora, bv = w_uv.shape
    w = jnp.zeros((hb, lora, hb * bv), F32)
    for h in range(hb):
        w = w.at[h, :, h * bv:(h + 1) * bv].set(w_uv[h])
    return w.astype(BF16)


def _mla_prompt(dm, pr, wuv_pad, b, t, tq, ck):
    hb, lora, nope, ropeb, bv = dm["HB"], dm["KV"], dm["NOPE"], dm["ROPE"], dm["BV"]
    kc = lora + LANES
    assert t % tq == 0 and t % ck == 0 and ck % LANES == 0
    nq = t // tq
    cfg = dict(tq=tq, ck=ck, KV=lora, scale=(nope + ropeb) ** -0.5)
    rows = hb * tq
    return pl.pallas_call(
        functools.partial(_mla_prompt_kernel, cfg),
        grid=(b, nq),
        in_specs=[
            pl.BlockSpec((hb, tq, kc), lambda bi, j: (0, bi * nq + j, 0)),
            pl.BlockSpec((t, kc), lambda bi, j: (bi, 0)),
            pl.BlockSpec(wuv_pad.shape, lambda bi, j: (0, 0, 0)),
        ],
        out_specs=pl.BlockSpec((tq, hb * bv), lambda bi, j: (bi * nq + j, 0)),
        out_shape=jax.ShapeDtypeStruct((b * t, hb * bv), BF16),
        scratch_shapes=[pltpu.VMEM((rows, LANES), F32), pltpu.VMEM((rows, LANES), F32), pltpu.VMEM((rows, LANES), F32),
                        pltpu.VMEM((rows, lora), F32), pltpu.VMEM((rows, ck), BF16)],
        compiler_params=_cparams(("parallel", "arbitrary")),
        name="mla_prompt",
    )(pr["qcat"], pr["kcat16"], wuv_pad)


def _finish_kernel(x_ref, oa_ref, ob_ref, ga_ref, gb_ref, wpa_ref, wpb_ref, wo_ref, gf_ref, wpq_ref,
                   x1_ref, x1n_ref, q_ref):
    merged = ga_ref[...] * _dot(oa_ref[...], wpa_ref[...]) + gb_ref[...] * _dot(ob_ref[...], wpb_ref[...])
    x1 = x_ref[...] + _dot(merged.astype(BF16), wo_ref[...])
    x1_ref[...] = x1
    ms = jnp.mean(x1 * x1, axis=-1, keepdims=True)
    x1n = (x1 * lax.rsqrt(ms + EPS) * gf_ref[...]).astype(BF16)
    x1n_ref[...] = x1n
    q_ref[...] = _dot(x1n, wpq_ref[...]).astype(q_ref.dtype)


def _finish(x2d, oa, ob, ga, gb, wts, tm):
    n, d = x2d.shape
    assert n % tm == 0
    rows = lambda w: pl.BlockSpec((tm, w), lambda i: (i, 0))
    full = lambda a: pl.BlockSpec(a.shape, lambda i: (0,) * a.ndim)
    wpa, wpb, wo, gf, wpq = wts
    dq = wpq.shape[1]
    return pl.pallas_call(
        _finish_kernel,
        grid=(n // tm,),
        in_specs=[rows(d), rows(oa.shape[1]), rows(ob.shape[1]), rows(d), rows(d),
                  full(wpa), full(wpb), full(wo), full(gf), full(wpq)],
        out_specs=[rows(d), rows(d), rows(dq)],
        out_shape=[jax.ShapeDtypeStruct((n, d), F32), jax.ShapeDtypeStruct((n, d), BF16),
                   jax.ShapeDtypeStruct((n, dq), BF16)],
        compiler_params=_cparams(("parallel",)),
        name="finish",
    )(x2d, oa, ob, ga, gb, wpa, wpb, wo, gf, wpq)


def _oddeven_merge(lo, hi, r):
    step = r * 2
    if step < hi - lo:
        yield from _oddeven_merge(lo, hi, step)
        yield from _oddeven_merge(lo + r, hi, step)
        yield from [(i, i + r) for i in range(lo + r, hi - r, step)]
    else:
        yield (lo, lo + r)


def _oddeven_merge_sort(lo, hi):
    if hi - lo >= 1:
        mid = lo + (hi - lo) // 2
        yield from _oddeven_merge_sort(lo, mid)
        yield from _oddeven_merge_sort(mid + 1, hi)
        yield from _oddeven_merge(lo, hi, 1)


SORT16 = tuple(_oddeven_merge_sort(0, PEER_TOPK - 1))
BITONIC16 = tuple((i, i + k) for k in (8, 4, 2, 1) for i in range(PEER_TOPK) if (i & k) == 0)
SUBLANES = 8


def _compare_exchange(x, net):
    for i, j in net:
        x[i], x[j] = jnp.maximum(x[i], x[j]), jnp.minimum(x[i], x[j])
    return x


def _merge_top16(a, b_rev):
    return _compare_exchange([jnp.maximum(x, y) for x, y in zip(a, b_rev)], BITONIC16)


def _top16_sorted(s):
    x = _compare_exchange([s[i * SUBLANES:(i + 1) * SUBLANES, :] for i in range(PEER_TOPK)], SORT16)
    for shift in (4, 6, 7):
        x = _merge_top16(x, [pltpu.roll(x[PEER_TOPK - 1 - i], shift, 0) for i in range(PEER_TOPK)])
    return x


def _peer_route_kernel(cfg, q_ref, k1_ref, k2_ref, s1_ref, s2_ref, e1_ref, e2_ref, thr_ref):
    hp, dk = cfg["HP"], cfg["DK"]
    tn = q_ref.shape[0]
    sub = lax.broadcasted_iota(I32, (SUBLANES, tn), 0)
    v1p = [jnp.zeros((SUBLANES, tn), F32)] * PEER_TOPK
    v2p = [jnp.zeros((SUBLANES, tn), F32)] * PEER_TOPK
    for h in range(hp):
        qh = q_ref[:, h * dk:(h + 1) * dk]
        s1 = _dot_nt(k1_ref[h], qh)
        s2 = _dot_nt(k2_ref[h], qh)
        s1_ref[0, h] = s1
        s2_ref[0, h] = s2
        t1 = _top16_sorted(s1)
        t2 = _top16_sorted(s2)
        v1p = [jnp.where(sub == h, a, b) for a, b in zip(t1, v1p)]
        v2p = [jnp.where(sub == h, a, b) for a, b in zip(t2, v2p)]
    cur = [v1p[0] + v2p[r2] for r2 in range(PEER_TOPK)]
    for r1 in range(1, PEER_TOPK):
        lim = PEER_TOPK // (r1 + 1)
        b_rev = [v1p[r1] + v2p[PEER_TOPK - 1 - i] if i >= PEER_TOPK - lim else None for i in range(PEER_TOPK)]
        cur = _compare_exchange([a if b is None else jnp.maximum(a, b) for a, b in zip(cur, b_rev)], BITONIC16)
    thr = cur[PEER_TOPK - 1]
    z = jnp.zeros((SUBLANES, tn), F32)
    for r in range(PEER_TOPK):
        z = z + jnp.exp(cur[r] - cur[0])
    thr_ref[0] = thr
    rz = 1.0 / z
    for h in range(hp):
        e1_ref[0, h] = jnp.exp(s1_ref[0, h] - v1p[0][h:h + 1, :]) * rz[h:h + 1, :]
        e2_ref[0, h] = jnp.exp(s2_ref[0, h] - v2p[0][h:h + 1, :])


def _peer_route(dm, q, k1p, k2p, tn):
    n = q.shape[0]
    hp, nk, dk = dm["HP"], dm["NK"], dm["DK"]
    assert n % tn == 0 and nk == PEER_TOPK * SUBLANES and hp <= SUBLANES and dk % LANES == 0
    big = pl.BlockSpec((1, hp, nk, tn), lambda i: (i, 0, 0, 0))
    sds = jax.ShapeDtypeStruct((n // tn, hp, nk, tn), F32)
    return pl.pallas_call(
        functools.partial(_peer_route_kernel, dict(HP=hp, DK=dk)),
        grid=(n // tn,),
        in_specs=[pl.BlockSpec((tn, hp * dk), lambda i: (i, 0)),
                  pl.BlockSpec(k1p.shape, lambda i: (0, 0, 0)), pl.BlockSpec(k2p.shape, lambda i: (0, 0, 0))],
        out_specs=[big, big, big, big, pl.BlockSpec((1, SUBLANES, tn), lambda i: (i, 0, 0))],
        out_shape=[sds, sds, sds, sds, jax.ShapeDtypeStruct((n // tn, SUBLANES, tn), F32)],
        compiler_params=_cparams(("parallel",)),
        name="peer_route",
    )(q, k1p, k2p)


def _peer_dense_kernel(cfg, x_ref, x1_ref, u_ref, vt_ref, s1_ref, s2_ref, e1_ref, e2_ref, thr_ref,
                       y_ref, acc_ref, act_ref, g_ref):
    hp, nk = cfg["HP"], cfg["NK"]
    eb = u_ref.shape[0]
    nts, tl = s1_ref.shape[0], s1_ref.shape[-1]
    e = pl.program_id(1)

    @pl.when(e == 0)
    def _():
        acc_ref[...] = jnp.zeros(acc_ref.shape, F32)

    a = _dot_nt(u_ref[...], x_ref[...])
    act_ref[...] = 0.5 * a * (1.0 + lax.erf(a * (2.0 ** -0.5)))

    for ts in range(nts):
        lanes = slice(ts * tl, (ts + 1) * tl)

        def key_body(k, carry, ts=ts, lanes=lanes):
            i1 = e * (eb // nk) + k
            g = None
            for h in range(hp):
                cand = s1_ref[ts, h, pl.ds(i1, 1), :] + s2_ref[ts, h]
                w = jnp.where(cand >= thr_ref[ts, h:h + 1, :], e1_ref[ts, h, pl.ds(i1, 1), :] * e2_ref[ts, h], 0.0)
                g = w if g is None else g + w
            rows = pl.ds(pl.multiple_of(k * nk, nk), nk)
            g_ref[rows, lanes] = (g * act_ref[rows, lanes]).astype(BF16)
            return carry

        lax.fori_loop(0, eb // nk, key_body, 0)
    acc_ref[...] += _dot(vt_ref[...], g_ref[...])

    @pl.when(e == pl.num_programs(1) - 1)
    def _():
        y_ref[...] = x1_ref[...] + acc_ref[...].T


def _peer_dense(dm, x1n, x1, u16, vt16, route, tn, eb):
    n, d = x1.shape
    ne = u16.shape[0]
    hp, nk = dm["HP"], dm["NK"]
    s1, s2, e1, e2, thr = route
    tl = s1.shape[-1]
    assert n % tn == 0 and ne % eb == 0 and eb % nk == 0 and tn % tl == 0
    nts = tn // tl
    big = pl.BlockSpec((nts, hp, nk, tl), lambda i, e: (i, 0, 0, 0))
    return pl.pallas_call(
        functools.partial(_peer_dense_kernel, dict(HP=hp, NK=nk)),
        grid=(n // tn, ne // eb),
        in_specs=[pl.BlockSpec((tn, d), lambda i, e: (i, 0)), pl.BlockSpec((tn, d), lambda i, e: (i, 0)),
                  pl.BlockSpec((eb, d), lambda i, e: (e, 0)), pl.BlockSpec((d, eb), lambda i, e: (0, e)),
                  big, big, big, big, pl.BlockSpec((nts, SUBLANES, tl), lambda i, e: (i, 0, 0))],
        out_specs=pl.BlockSpec((tn, d), lambda i, e: (i, 0)),
        out_shape=jax.ShapeDtypeStruct((n, d), F32),
        scratch_shapes=[pltpu.VMEM((d, tn), F32), pltpu.VMEM((eb, tn), F32), pltpu.VMEM((eb, tn), BF16)],
        compiler_params=_cparams(("parallel", "arbitrary")),
        name="peer_dense",
    )(x1n, x1, u16, vt16, s1, s2, e1, e2, thr)


def _tail(dm, x2d, oa, ob, ga, gb, tail_w, peer_w, tm, tn_route, tn_dense, eb):
    x1, x1n, q = _finish(x2d, oa, ob, ga, gb, tail_w, tm)
    k1p, k2p, u16, vt16 = peer_w
    route = _peer_route(dm, q, k1p, k2p, tn_route)
    return _peer_dense(dm, x1n, x1, u16, vt16, route, tn_dense, eb)


def _page_specs(n, shape, pps):
    zeros = (0,) * len(shape)
    return [pl.BlockSpec((1,) + shape, (lambda b, j, pt, k=k: (pt[b, j * pps + k],) + zeros)) for k in range(n)]


def _sample_scores_kernel(cfg, pt_ref, q_ref, w_ref, *refs):
    pps, sc1, sc2 = cfg["pps"], cfg["sc1"], cfg["sc2"]
    pages, out_ref = refs[:pps], refs[pps]
    q, w = q_ref[0], w_ref[0]
    parts = []
    for r in pages:
        s = _dot(q, r[0].astype(BF16)) * sc1
        parts.append(jnp.sum(jnp.maximum(s, 0.0) * w, axis=0, keepdims=True))
    out_ref[0] = jnp.concatenate(parts, axis=1) * sc2


def _sample_scores(dm, qi, wcol, idx_t, page_table, pps):
    bs, n_pages = page_table.shape
    hi, di = dm["HI"], dm["DI"]
    page = idx_t.shape[-1]
    cfg = dict(pps=pps, sc1=di ** -0.5, sc2=hi ** -0.5)
    grid_spec = pltpu.PrefetchScalarGridSpec(
        num_scalar_prefetch=1, grid=(bs, n_pages // pps),
        in_specs=[pl.BlockSpec((1, hi, di), lambda b, j, pt: (b, 0, 0)),
                  pl.BlockSpec((1, hi, page), lambda b, j, pt: (b, 0, 0))]
                 + _page_specs(pps, (di, page), pps),
        out_specs=pl.BlockSpec((1, 1, pps * page), lambda b, j, pt: (b, 0, j)))
    return pl.pallas_call(
        functools.partial(_sample_scores_kernel, cfg), grid_spec=grid_spec,
        out_shape=jax.ShapeDtypeStruct((bs, 1, n_pages * page), F32),
        compiler_params=_cparams(("parallel", "arbitrary")), name="sample_scores",
    )(page_table, qi, wcol, *([idx_t] * pps))


def _sample_select_kernel(cfg, s_ref, qi_ref, ki_ref, w_ref, bd_ref, bias_ref, bnew_ref, keys_ref):
    slots, topk, past, sc1, sc2 = cfg["slots"], cfg["topk"], cfg["past"], cfg["sc1"], cfg["sc2"]
    bsz, width = s_ref.shape[1], s_ref.shape[2]
    prod = qi_ref[...] * ki_ref[...].astype(BF16).astype(F32)
    hs = _dot_hilo(prod, bd_ref[...])
    s_new = jnp.sum(jnp.maximum(hs * sc1, 0.0) * w_ref[...], axis=1, keepdims=True) * sc2
    k_new = _sortable_key(s_new)
    for kk in range(slots):
        keys_ref[kk] = _sortable_key(s_ref[kk])
    lane = lax.broadcasted_iota(I32, (bsz, width), 1)

    def count(ind, ind_new):
        tot = ind_new
        for kk in range(slots):
            tot = tot + jnp.sum(ind(keys_ref[kk], kk), axis=1, keepdims=True)
        return tot

    def bit_body(i, prefix):
        cand_u = prefix | lax.shift_left(jnp.int32(1), 31 - i)
        cand_s = cand_u ^ INT_MIN
        cnt = count(lambda k, kk: jnp.where(k >= cand_s, 1.0, 0.0), jnp.where(k_new >= cand_s, 1.0, 0.0))
        return jnp.where(cnt >= topk, cand_u, prefix)

    thr = lax.fori_loop(0, 32, bit_body, jnp.zeros((bsz, 1), I32)) ^ INT_MIN
    n_gt = count(lambda k, kk: jnp.where(k > thr, 1.0, 0.0), jnp.where(k_new > thr, 1.0, 0.0))
    need = topk - n_gt
    nbits = max(1, past.bit_length())

    def p_body(i, p):
        cand = p | lax.shift_left(jnp.int32(1), nbits - 1 - i)
        f = count(lambda k, kk: jnp.where(k == thr, jnp.where(lane * slots + kk < cand, 1.0, 0.0), 0.0),
                  jnp.where(k_new == thr, jnp.where(past < cand, 1.0, 0.0), 0.0))
        return jnp.where(f < need, cand, p)

    pmax = lax.fori_loop(0, nbits, p_body, jnp.zeros((bsz, 1), I32))
    for kk in range(slots):
        k = keys_ref[kk]
        sel = jnp.where(k > thr, 1, jnp.where(k == thr, jnp.where(lane * slots + kk <= pmax, 1, 0), 0))
        bias_ref[kk] = jnp.where(sel > 0, 0.0, NEG_INF)
    sel_new = jnp.where(k_new > thr, 1, jnp.where(k_new == thr, jnp.where(past <= pmax, 1, 0), 0))
    bnew_ref[...] = jnp.broadcast_to(jnp.where(sel_new > 0, 0.0, NEG_INF), bnew_ref.shape)


def _sample_select(dm, scores_t, qi_flat, ki_rep, w_pad, bd, past):
    slots, bs, width = scores_t.shape
    hi, di = dm["HI"], dm["DI"]
    topk = min(IDX_TOPK_MAX, (past + 1) // 4)
    cfg = dict(slots=slots, topk=topk, past=past, sc1=di ** -0.5, sc2=hi ** -0.5)
    return pl.pallas_call(
        functools.partial(_sample_select_kernel, cfg),
        out_shape=[jax.ShapeDtypeStruct((slots, bs, width), F32), jax.ShapeDtypeStruct((bs, LANES), F32)],
        scratch_shapes=[pltpu.VMEM((slots, bs, width), I32)],
        compiler_params=pltpu.CompilerParams(vmem_limit_bytes=VMEM_LIMIT), name="sample_select",
    )(scores_t, qi_flat, ki_rep, w_pad, bd)


def _online_softmax_step(m_ref, l_ref, acc_ref, lg, pv_fn):
    m_prev = m_ref[...]
    m_new = jnp.maximum(m_prev, jnp.max(lg, axis=1, keepdims=True))
    m_safe = jnp.where(m_new == NEG_INF, 0.0, m_new)
    p = jnp.exp(lg - m_safe)
    alpha = jnp.exp(m_prev - m_safe)
    l_ref[...] = alpha * l_ref[...] + jnp.sum(p, axis=1, keepdims=True)
    acc_ref[...] = alpha * acc_ref[...] + pv_fn(p)
    m_ref[...] = m_new


def _sample_dsa_kernel(cfg, pt_ref, q_ref, bias_ref, bnew_ref, knew_ref, vnew_ref, *refs):
    pps, sca = cfg["pps"], cfg["sca"]
    kpages, vpages = refs[:pps], refs[pps:2 * pps]
    o_ref, m_ref, l_ref, acc_ref = refs[2 * pps:]
    j = pl.program_id(1)
    heads, wa = q_ref.shape[1], q_ref.shape[2]
    da = wa // heads
    page = kpages[0].shape[-1]

    @pl.when(j == 0)
    def _():
        m_ref[...] = jnp.full(m_ref.shape, NEG_INF, F32)
        l_ref[...] = jnp.zeros(l_ref.shape, F32)
        acc_ref[...] = jnp.zeros(acc_ref.shape, F32)

    q = q_ref[0]
    lg = jnp.concatenate([_dot(q, r[0].reshape(wa, page).astype(BF16)) for r in kpages], axis=1)
    lg = lg * sca + bias_ref[0]

    def pv(p):
        pb = p.astype(BF16)
        out = None
        for i, r in enumerate(vpages):
            t = _dot_nt(pb[:, i * page:(i + 1) * page], r[0].reshape(wa, page).astype(BF16))
            out = t if out is None else out + t
        return out

    _online_softmax_step(m_ref, l_ref, acc_ref, lg, pv)

    @pl.when(j == pl.num_programs(1) - 1)
    def _():
        lgn = jnp.sum(q.astype(F32) * knew_ref[0].astype(F32), axis=1, keepdims=True) * sca + bnew_ref[0][:, 0:1]
        _online_softmax_step(m_ref, l_ref, acc_ref, lgn, lambda p: p * vnew_ref[0].astype(F32))
        o = acc_ref[...] / l_ref[...]
        row = lax.broadcasted_iota(I32, (heads, wa), 0)
        lane = lax.broadcasted_iota(I32, (heads, wa), 1)
        o_ref[0] = jnp.sum(jnp.where(lane // da == row, o, 0.0), axis=0, keepdims=True).astype(o_ref.dtype)


def _sample_dsa(dm, qbd, bias, bnew, knew, vnew, kt, vt, page_table, pps):
    bs, n_pages = page_table.shape
    ha, da = dm["HA"], dm["DA"]
    wa = ha * da
    page = kt.shape[-1]
    cfg = dict(pps=pps, sca=da ** -0.5)
    per_b = lambda shape: pl.BlockSpec((1,) + shape, lambda b, j, pt: (b, 0, 0))
    grid_spec = pltpu.PrefetchScalarGridSpec(
        num_scalar_prefetch=1, grid=(bs, n_pages // pps),
        in_specs=[per_b((ha, wa)), pl.BlockSpec((1, 1, pps * page), lambda b, j, pt: (b, 0, j)),
                  per_b((1, LANES)), per_b((1, wa)), per_b((1, wa))]
                 + _page_specs(pps, (ha, da, page), pps) + _page_specs(pps, (ha, da, page), pps),
        out_specs=per_b((1, wa)),
        scratch_shapes=[pltpu.VMEM((ha, 1), F32), pltpu.VMEM((ha, 1), F32), pltpu.VMEM((ha, wa), F32)])
    return pl.pallas_call(
        functools.partial(_sample_dsa_kernel, cfg), grid_spec=grid_spec,
        out_shape=jax.ShapeDtypeStruct((bs, 1, wa), BF16),
        compiler_params=_cparams(("parallel", "arbitrary")), name="sample_dsa",
    )(page_table, qbd, bias, bnew, knew, vnew, *([kt] * pps), *([vt] * pps))


def _sample_mla_kernel(cfg, pt_ref, qabs_ref, qpe_ref, qfull_ref, knew_ref, wuv_ref, *refs):
    pps, lora, scale = cfg["pps"], cfg["KV"], cfg["scale"]
    cpages, ppages = refs[:pps], refs[pps:2 * pps]
    o_ref, m_ref, l_ref, acc_ref = refs[2 * pps:]
    j = pl.program_id(1)
    heads = qabs_ref.shape[1]
    page = cpages[0].shape[1]

    @pl.when(j == 0)
    def _():
        m_ref[...] = jnp.full(m_ref.shape, NEG_INF, F32)
        l_ref[...] = jnp.zeros(l_ref.shape, F32)
        acc_ref[...] = jnp.zeros(acc_ref.shape, F32)

    qabs, qpe = qabs_ref[0], qpe_ref[0]
    cs = [r[0].astype(BF16) for r in cpages]
    lg = jnp.concatenate([_dot_nt(qabs, c) + _dot(qpe, r[0].astype(BF16)) for c, r in zip(cs, ppages)],
                         axis=1) * scale

    def pv(p):
        pb = p.astype(BF16)
        out = None
        for i, c in enumerate(cs):
            t = _dot(pb[:, i * page:(i + 1) * page], c)
            out = t if out is None else out + t
        return out

    _online_softmax_step(m_ref, l_ref, acc_ref, lg, pv)

    @pl.when(j == pl.num_programs(1) - 1)
    def _():
        kn = knew_ref[0].astype(F32)
        lgn = jnp.sum(qfull_ref[0].astype(F32) * kn, axis=1, keepdims=True) * scale
        _online_softmax_step(m_ref, l_ref, acc_ref, lgn, lambda p: p * kn[:, :lora])
        o = (acc_ref[...] / l_ref[...]).astype(BF16)
        row = lax.broadcasted_iota(I32, (heads, o_ref.shape[-1]), 0)
        out = jnp.zeros((heads, o_ref.shape[-1]), F32)
        for h in range(heads):
            out = out + jnp.where(row == h, _dot(o, wuv_ref[h]), 0.0)
        o_ref[0] = jnp.sum(out, axis=0, keepdims=True).astype(o_ref.dtype)


def _sample_mla(dm, qabs, qpe, qfull, knew, wuv_pad, cache_ckv, kpe_t, page_table, pps):
    bs, n_pages = page_table.shape
    hb, lora, nope, ropeb, bv = dm["HB"], dm["KV"], dm["NOPE"], dm["ROPE"], dm["BV"]
    page = cache_ckv.shape[1]
    cfg = dict(pps=pps, KV=lora, scale=(nope + ropeb) ** -0.5)
    per_b = lambda shape: pl.BlockSpec((1,) + shape, lambda b, j, pt: (b, 0, 0))
    grid_spec = pltpu.PrefetchScalarGridSpec(
        num_scalar_prefetch=1, grid=(bs, n_pages // pps),
        in_specs=[per_b((hb, lora)), per_b((hb, ropeb)), per_b((hb, lora + LANES)), per_b((1, lora + LANES)),
                  pl.BlockSpec(wuv_pad.shape, lambda b, j, pt: (0, 0, 0))]
                 + _page_specs(pps, (page, lora), pps) + _page_specs(pps, (ropeb, page), pps),
        out_specs=per_b((1, hb * bv)),
        scratch_shapes=[pltpu.VMEM((hb, 1), F32), pltpu.VMEM((hb, 1), F32), pltpu.VMEM((hb, lora), F32)])
    return pl.pallas_call(
        functools.partial(_sample_mla_kernel, cfg), grid_spec=grid_spec,
        out_shape=jax.ShapeDtypeStruct((bs, 1, hb * bv), BF16),
        compiler_params=_cparams(("parallel", "arbitrary")), name="sample_mla",
    )(page_table, qabs, qpe, qfull, knew, wuv_pad, *([cache_ckv] * pps), *([kpe_t] * pps))


def _unmask_heads(xm, d):
    per = LANES // d
    return jnp.stack([xm[h][:, (h % per) * d:(h % per + 1) * d] for h in range(xm.shape[0])], axis=1)


def _sample_attention(dm, ps, cache_k, cache_v, cache_idx_k, cache_ckv, cache_kpe, page_table, wuv_pad):
    bs, n_pages = page_table.shape
    page = cache_k.shape[1]
    past = page * n_pages
    ha, da, hi, di = dm["HA"], dm["DA"], dm["HI"], dm["DI"]
    hb, lora, ropeb = dm["HB"], dm["KV"], dm["ROPE"]
    wa = ha * da
    pps_small = math.gcd(n_pages, 16)
    pps_big = math.gcd(n_pages, 8)
    idx_t = jnp.transpose(cache_idx_k, (0, 2, 1))
    kpe_t = jnp.transpose(cache_kpe, (0, 2, 1))
    k_t = jnp.transpose(cache_k, (0, 2, 3, 1))
    v_t = jnp.transpose(cache_v, (0, 2, 3, 1))

    qi = _unmask_heads(ps["qim"], di)
    w_i = ps["wit"][:hi].T
    wcol = jnp.broadcast_to(w_i[:, :, None], (bs, hi, page))
    scores = _sample_scores(dm, qi, wcol, idx_t, page_table, pps_small)
    bd = _block_diag_ones(hi * di, di, LANES, 1)
    w_pad = jnp.zeros((bs, LANES), F32).at[:, :hi].set(w_i)
    ki_rep = jnp.tile(ps["ki32"][:, :di], (1, hi))
    bias_t, bnew = _sample_select(dm, scores.reshape(1, bs, past), qi.reshape(bs, hi * di).astype(F32),
                                  ki_rep, w_pad, bd, past)

    per = LANES // da
    qbd = jnp.zeros((bs, ha, wa), BF16)
    for h in range(ha):
        g = h // per
        qbd = qbd.at[:, h, g * LANES:(g + 1) * LANES].set(ps["qam"][h])
    oa = _sample_dsa(dm, qbd, bias_t.reshape(bs, 1, past), bnew.reshape(bs, 1, LANES), ps["ka16"].reshape(bs, 1, wa),
                     ps["va16"].reshape(bs, 1, wa), k_t, v_t, page_table, pps_big)

    qfull = jnp.transpose(ps["qcat"], (1, 0, 2))
    qpe = _unmask_heads(ps["qcat"][:, :, lora:], ropeb)
    ob = _sample_mla(dm, qfull[:, :, :lora], qpe, qfull, ps["kcat16"].reshape(bs, 1, lora + LANES), wuv_pad,
                     cache_ckv, kpe_t, page_table, pps_small)
    return oa.reshape(bs, wa), ob.reshape(bs, -1)


def _model_dims(cache_k, cache_idx_k, cache_ckv, cache_kpe, w_in, g_cq, w_uq, w_uk, w_uv, sub_k1):
    ha, da = cache_k.shape[2], cache_k.shape[3]
    di, lora, ropeb = cache_idx_k.shape[-1], cache_ckv.shape[-1], cache_kpe.shape[-1]
    ql, hb, nope, bv = g_cq.shape[0], w_uq.shape[1], w_uk.shape[1], w_uv.shape[2]
    d = w_in.shape[0]
    rest = w_in.shape[1] - (3 * ha * da + di + ql + lora + ropeb + 2 * d)
    hi = rest // (di + 1)
    assert hi * (di + 1) == rest
    for v in (da, di, ropeb):
        assert LANES % v == 0
    for v in (ha * da, hi * di, hb * nope, hb * ropeb, lora, ql, d):
        assert v % LANES == 0
    return dict(HA=ha, DA=da, HI=hi, DI=di, HB=hb, NOPE=nope, ROPE=ropeb, KV=lora, QL=ql, BV=bv, D=d,
                HP=sub_k1.shape[0], NK=sub_k1.shape[1], DK=2 * sub_k1.shape[2])


def kernel(x_prompt, x_sample, cache_k, cache_v, cache_idx_k, cache_ckv, cache_kpe, page_table, g_attn, w_in, g_qa, g_ka, g_cq, w_uq, g_qb, w_uk, g_ckv, g_kpe, w_uv, w_pa, w_pb, w_o, g_ffn, w_pq, sub_k1, sub_k2, u_tab, v_tab):
    dm = _model_dims(cache_k, cache_idx_k, cache_ckv, cache_kpe, w_in, g_cq, w_uq, w_uk, w_uv, sub_k1)
    dm, consts = _prep_proj_weights(dm, g_attn, w_in, g_qa, g_ka, g_cq, w_uq, g_qb, w_uk, g_ckv, g_kpe)
    b, t, d = x_prompt.shape
    bs, ts, _ = x_sample.shape
    page, n_pages = cache_k.shape[1], page_table.shape[1]
    past = page * n_pages
    ha, da, di, lora, ropeb = dm["HA"], dm["DA"], dm["DI"], dm["KV"], dm["ROPE"]
    hp, nk, dk = dm["HP"], dm["NK"], dm["DK"]

    wuv_pad = _pad_wuv(w_uv)
    tail_w = (w_pa.astype(BF16), w_pb.astype(BF16), w_o.astype(BF16), g_ffn.reshape(1, -1), w_pq.astype(BF16))
    zeros_half = jnp.zeros((hp, nk, dk // 2), F32)
    k1p = jnp.concatenate([sub_k1, zeros_half], axis=-1).astype(BF16)
    k2p = jnp.concatenate([zeros_half, sub_k2], axis=-1).astype(BF16)
    peer_w = (k1p, k2p, u_tab.astype(BF16), v_tab.T.astype(BF16))

    n_p = b * t
    x_p = x_prompt.reshape(n_p, d)
    pp = _project(dm, consts, x_p, jnp.arange(t, dtype=I32), min(256, t))
    oa_p = _dsa_prompt(dm, pp, b, t, min(256, t))
    ob_p = _mla_prompt(dm, pp, wuv_pad, b, t, min(128, t), min(256, t))
    y_p = _tail(dm, x_p, oa_p, ob_p, pp["ga"], pp["gb"], tail_w, peer_w, min(256, n_p), 128, min(512, n_p), 512)

    n_s = bs * ts
    assert ts == 1
    x_s = x_sample.reshape(n_s, d)
    ps = _project(dm, consts, x_s, jnp.full((n_s,), past, I32), n_s)
    oa_s, ob_s = _sample_attention(dm, ps, cache_k, cache_v, cache_idx_k, cache_ckv, cache_kpe, page_table, wuv_pad)
    y_s = _tail(dm, x_s, oa_s, ob_s, ps["ga"], ps["gb"], tail_w, peer_w, n_s, n_s, n_s, 512)

    def kv_outs(p, bb, tt):
        return (p["ka32"].reshape(bb, tt, ha, da), p["va32"].reshape(bb, tt, ha, da),
                p["ki32"][:, :di].reshape(bb, tt, di), p["ckv32"].reshape(bb, tt, lora),
                p["kpe32"][:, :ropeb].reshape(bb, tt, ropeb))

    return (y_p.reshape(b, t, d), y_s.reshape(bs, ts, d)) + kv_outs(pp, b, t) + kv_outs(ps, bs, ts)
```

```python
import functools
import math

import numpy as np
import jax
import jax.numpy as jnp
from jax import lax
from jax.experimental import pallas as pl
from jax.experimental.pallas import tpu as pltpu

F32 = jnp.float32
BF16 = jnp.bfloat16
I32 = jnp.int32

ROPE_THETA = 10000.0
EPS = 1e-6
IDX_TOPK_MAX = 256
PEER_TOPK = 16
LANES = 128
NEG_INF = float("-inf")
VMEM_LIMIT = 56 * 1024 * 1024


def _cparams(sem):
    return pltpu.CompilerParams(dimension_semantics=sem, vmem_limit_bytes=VMEM_LIMIT)


def _dot(a, b):
    return jnp.dot(a, b, preferred_element_type=F32)


def _dot_nt(a, b):
    return lax.dot_general(a, b, (((1,), (1,)), ((), ())), preferred_element_type=F32)


def _dot_hilo(a, b):
    hi = a.astype(BF16)
    lo = (a - hi.astype(F32)).astype(BF16)
    return _dot(hi, b) + _dot(lo, b)


def _rope(x, c, s, half):
    w = x.shape[-1]
    fwd = pltpu.roll(x, half, 1)
    bwd = pltpu.roll(x, w - half, 1)
    lane = lax.broadcasted_iota(I32, (1, w), 1)
    first = (lane & (2 * half - 1)) < half
    return x * c + jnp.where(first, bwd, fwd) * s


def _group_masked(x, h, d):
    per = LANES // d
    g = h // per
    grp = x[:, g * LANES:(g + 1) * LANES]
    lane = lax.broadcasted_iota(I32, (1, LANES), 1)
    return jnp.where((lane // d) == (h % per), grp, 0.0)


def _proj_kernel(dm, x_ref, gat_ref, w_ref, wwi_ref, c64_ref, s64_ref, c32_ref, s32_ref,
                 gqa_ref, gka_ref, gcq_ref, gqbn_ref, gqbr_ref, gckv_ref, gkpe_ref,
                 bda_ref, bnr_ref, brn_ref, bdr_ref, wuq_ref, wuk_ref,
                 ka32, va32, ki32, ckv32, kpe32, ga, gb, ka16, va16, ki16, kcat16,
                 qam, qim, wit, qcat):
    off = dm["off"]
    x = x_ref[...]
    ms = jnp.mean(x * x, axis=-1, keepdims=True)
    h = (x * lax.rsqrt(ms + EPS) * gat_ref[...]).astype(BF16)

    def seg(name, n):
        a = off[name]
        return _dot(h, w_ref[:, a:a + n])

    ha, da, hi, di = dm["HA"], dm["DA"], dm["HI"], dm["DI"]
    hb, nope, ropeb, lora = dm["HB"], dm["NOPE"], dm["ROPE"], dm["KV"]
    wa = ha * da
    c64, s64 = c64_ref[...], s64_ref[...]
    c32, s32 = c32_ref[...], s32_ref[...]

    z = seg("qa", wa)
    q = z * lax.rsqrt(_dot_hilo(z * z, bda_ref[...]) * (1.0 / da) + EPS) * gqa_ref[...]
    q = _rope(q, c64, s64, da // 2)
    for hh in range(ha):
        qam[hh] = _group_masked(q, hh, da).astype(BF16)
    z = seg("ka", wa)
    k = z * lax.rsqrt(_dot_hilo(z * z, bda_ref[...]) * (1.0 / da) + EPS) * gka_ref[...]
    k = _rope(k, c64, s64, da // 2)
    ka32[...] = k
    ka16[...] = k.astype(BF16)
    z = seg("va", wa)
    va32[...] = z
    va16[...] = z.astype(BF16)

    z = seg("qi", hi * di)
    q = _rope(z, c32, s32, di // 2)
    for hh in range(hi):
        qim[hh] = _group_masked(q, hh, di).astype(BF16)
    wit[...] = _dot_nt(wwi_ref[...], h)
    z = seg("ki", LANES)
    k = _rope(z, c32[:, :LANES], s32[:, :LANES], di // 2)
    ki32[...] = k
    ki16[...] = k.astype(BF16)

    z = seg("cq", dm["QL"])
    zn = (z * lax.rsqrt(jnp.mean(z * z, axis=-1, keepdims=True) + EPS) * gcq_ref[...]).astype(BF16)
    qb = _dot(zn, wuq_ref[...])
    wn = hb * nope
    qn, qr = qb[:, :wn], qb[:, wn:]
    zzn, zzr = qn * qn, qr * qr
    totn = _dot_hilo(zzn, bda_ref[...]) + _dot_hilo(zzr, brn_ref[...])
    totr = _dot_hilo(zzn, bnr_ref[...]) + _dot_hilo(zzr, bdr_ref[...])
    inv_d = 1.0 / (nope + ropeb)
    qn = qn * lax.rsqrt(totn * inv_d + EPS) * gqbn_ref[...]
    qr = qr * lax.rsqrt(totr * inv_d + EPS) * gqbr_ref[...]
    qpe = _rope(qr, c32, s32, ropeb // 2)
    qabs = _dot(qn.astype(BF16), wuk_ref[...])
    for hh in range(hb):
        qcat[hh, :, 0:lora] = qabs[:, hh * lora:(hh + 1) * lora].astype(BF16)
        qcat[hh, :, lora:lora + LANES] = _group_masked(qpe, hh, ropeb).astype(BF16)

    z = seg("ckv", lora)
    ckv = z * lax.rsqrt(jnp.mean(z * z, axis=-1, keepdims=True) + EPS) * gckv_ref[...]
    ckv32[...] = ckv
    kcat16[:, 0:lora] = ckv.astype(BF16)
    z = seg("kpe", LANES)
    kp = z * lax.rsqrt(jnp.mean(z * z, axis=-1, keepdims=True) + EPS) * gkpe_ref[...]
    kp = _rope(kp, c32[:, :LANES], s32[:, :LANES], ropeb // 2)
    kpe32[...] = kp
    kcat16[:, lora:lora + LANES] = kp.astype(BF16)

    d = x.shape[-1]
    ga[...] = jax.nn.sigmoid(seg("ga", d))
    gb[...] = jax.nn.sigmoid(seg("gb", d))


def _rope_tables(pos, d, reps):
    half = d // 2
    inv = ROPE_THETA ** (-(jnp.arange(half, dtype=F32) * 2.0 / d))
    ang = pos.astype(F32)[:, None] * inv[None, :]
    cos, sin = jnp.cos(ang), jnp.sin(ang)
    c = jnp.concatenate([cos, cos], axis=-1)
    s = jnp.concatenate([-sin, sin], axis=-1)
    return jnp.tile(c, (1, reps)), jnp.tile(s, (1, reps))


def _block_diag_ones(rows, rd, cols, cd):
    r = np.arange(rows)[:, None] // rd
    c = np.arange(cols)[None, :] // cd
    return jnp.asarray((r == c).astype(np.float32), dtype=BF16)


def _prep_proj_weights(dm, g_attn, w_in, g_qa, g_ka, g_cq, w_uq, g_qb, w_uk, g_ckv, g_kpe):
    ha, da, hi, di = dm["HA"], dm["DA"], dm["HI"], dm["DI"]
    hb, nope, ropeb, lora, ql = dm["HB"], dm["NOPE"], dm["ROPE"], dm["KV"], dm["QL"]
    d = w_in.shape[0]
    sizes = (ha * da, ha * da, ha * da, hi * di, hi, di, ql, lora, ropeb, d, d)
    parts, o = [], 0
    for n in sizes:
        parts.append(w_in[:, o:o + n])
        o += n
    w_qa, w_ka, w_va, w_qi, w_wi, w_ki, w_cq, w_ckv, w_kpe, w_ga, w_gb = parts
    segs = [("qa", w_qa), ("ka", w_ka), ("va", w_va), ("qi", w_qi),
            ("ki", jnp.tile(w_ki, (1, LANES // di))), ("cq", w_cq), ("ckv", w_ckv),
            ("kpe", jnp.tile(w_kpe, (1, LANES // ropeb))), ("ga", w_ga), ("gb", w_gb)]
    off, o = {}, 0
    for name, w in segs:
        off[name] = o
        o += w.shape[1]
    w_all = jnp.concatenate([w for _, w in segs], axis=1).astype(BF16)
    hip = -(-hi // 8) * 8
    wwi = jnp.zeros((hip, d), F32).at[:hi].set(w_wi.T).astype(BF16)
    wuq = jnp.concatenate([w_uq[:, :, :nope].reshape(ql, hb * nope),
                           w_uq[:, :, nope:].reshape(ql, hb * ropeb)], axis=1).astype(BF16)
    wuk = jnp.zeros((hb * nope, hb * lora), F32)
    for hh in range(hb):
        wuk = wuk.at[hh * nope:(hh + 1) * nope, hh * lora:(hh + 1) * lora].set(w_uk[hh])
    wuk = wuk.astype(BF16)
    row = lambda v: v.reshape(1, -1).astype(F32)
    consts = dict(
        gat=row(g_attn), w=w_all, wwi=wwi,
        gqa=row(jnp.tile(g_qa, ha)), gka=row(jnp.tile(g_ka, ha)), gcq=row(g_cq),
        gqbn=row(jnp.tile(g_qb[:nope], hb)), gqbr=row(jnp.tile(g_qb[nope:], hb)),
        gckv=row(g_ckv), gkpe=row(jnp.tile(g_kpe, LANES // ropeb)),
        bda=_block_diag_ones(ha * da, da, ha * da, da),
        bnr=_block_diag_ones(hb * nope, nope, hb * ropeb, ropeb),
        brn=_block_diag_ones(hb * ropeb, ropeb, hb * nope, nope),
        bdr=_block_diag_ones(hb * ropeb, ropeb, hb * ropeb, ropeb),
        wuq=wuq, wuk=wuk)
    dm = dict(dm, off=off, HIP=hip, WTOT=o)
    return dm, consts


def _project(dm, consts, x2d, pos, tm):
    n, d = x2d.shape
    p = pos.shape[0]
    assert n % tm == 0 and p % tm == 0
    npb = p // tm
    ha, da, hi, di = dm["HA"], dm["DA"], dm["HI"], dm["DI"]
    hb, ropeb, lora = dm["HB"], dm["ROPE"], dm["KV"]
    wa = ha * da
    c64, s64 = _rope_tables(pos, da, ha)
    c32, s32 = _rope_tables(pos, di, hi)
    full = lambda a: pl.BlockSpec(a.shape, lambda i: (0,) * a.ndim)
    rows = lambda w: pl.BlockSpec((tm, w), lambda i: (i, 0))
    prow = lambda w: pl.BlockSpec((tm, w), lambda i: (i % npb, 0))
    hrows = lambda hh, w: pl.BlockSpec((hh, tm, w), lambda i: (0, i, 0))
    cn = ["gat", "w", "wwi"]
    cg = ["gqa", "gka", "gcq", "gqbn", "gqbr", "gckv", "gkpe", "bda", "bnr", "brn", "bdr", "wuq", "wuk"]
    ins = [x2d] + [consts[k] for k in cn] + [c64, s64, c32, s32] + [consts[k] for k in cg]
    in_specs = ([rows(d)] + [full(consts[k]) for k in cn] + [prow(wa), prow(wa), prow(hi * di), prow(hi * di)]
                + [full(consts[k]) for k in cg])
    sd = jax.ShapeDtypeStruct
    outs = [
        (sd((n, wa), F32), rows(wa)), (sd((n, wa), F32), rows(wa)), (sd((n, LANES), F32), rows(LANES)),
        (sd((n, lora), F32), rows(lora)), (sd((n, LANES), F32), rows(LANES)),
        (sd((n, d), F32), rows(d)), (sd((n, d), F32), rows(d)),
        (sd((n, wa), BF16), rows(wa)), (sd((n, wa), BF16), rows(wa)), (sd((n, LANES), BF16), rows(LANES)),
        (sd((n, lora + LANES), BF16), rows(lora + LANES)),
        (sd((ha, n, LANES), BF16), hrows(ha, LANES)), (sd((hi, n, LANES), BF16), hrows(hi, LANES)),
        (sd((dm["HIP"], n), F32), pl.BlockSpec((dm["HIP"], tm), lambda i: (0, i))),
        (sd((hb, n, lora + LANES), BF16), hrows(hb, lora + LANES)),
    ]
    res = pl.pallas_call(
        functools.partial(_proj_kernel, dm),
        grid=(n // tm,),
        in_specs=in_specs,
        out_specs=[o[1] for o in outs],
        out_shape=[o[0] for o in outs],
        compiler_params=_cparams(("parallel",)),
        name="proj",
    )(*ins)
    names = ["ka32", "va32", "ki32", "ckv32", "kpe32", "ga", "gb", "ka16", "va16", "ki16", "kcat16",
             "qam", "qim", "wit", "qcat"]
    return dict(zip(names, res))


INT_MIN = -2147483648


def _sortable_key(x):
    bits = pltpu.bitcast(x, I32)
    return bits ^ ((bits >> 31) & 0x7FFFFFFF)


def _dsa_prompt_kernel(cfg, qim_ref, ki_ref, wit_ref, qam_ref, ka_ref, va_ref, o_ref,
                       keys_ref, bias_ref, pmax_ref, m_ref, l_ref, acc_ref):
    tq, topk, hi, ha, da, t_total = cfg["tq"], cfg["topk"], cfg["HI"], cfg["HA"], cfg["DA"], cfg["T"]
    sc1, sc2, sca = cfg["sc1"], cfg["sc2"], cfg["sca"]
    c = tq
    j = pl.program_id(1)
    nch = j + 1
    row_i = lax.broadcasted_iota(I32, (c, tq), 0)
    col_i = lax.broadcasted_iota(I32, (c, tq), 1)
    t_idx = j * tq + col_i

    def chunk(ci):
        return pl.ds(pl.multiple_of(ci * c, c), c)

    def score_body(ci, carry):
        kc = ki_ref[chunk(ci), :]
        acc = jnp.zeros((c, tq), F32)
        for h in range(hi):
            s = _dot_nt(kc, qim_ref[h]) * sc1
            acc = acc + jnp.maximum(s, 0.0) * wit_ref[h:h + 1, :]
        score = acc * sc2
        score = jnp.where(ci * c + row_i <= t_idx, score, NEG_INF)
        keys_ref[chunk(ci), :] = _sortable_key(score)
        return carry

    lax.fori_loop(0, nch, score_body, 0)

    def count(ind):
        def body(ci, acc):
            kk = keys_ref[chunk(ci), :]
            return acc + jnp.sum(ind(kk, ci), axis=0, keepdims=True)
        return lax.fori_loop(0, nch, body, jnp.zeros((1, tq), I32))

    def bit_body(i, prefix):
        cand_u = prefix | lax.shift_left(jnp.int32(1), 31 - i)
        cand_s = cand_u ^ INT_MIN
        cnt = count(lambda kk, ci: jnp.where(kk >= cand_s, 1, 0))
        return jnp.where(cnt >= topk, cand_u, prefix)

    thr = lax.fori_loop(0, 32, bit_body, jnp.zeros((1, tq), I32)) ^ INT_MIN
    n_gt = count(lambda kk, ci: jnp.where(kk > thr, 1, 0))
    n_ge = count(lambda kk, ci: jnp.where(kk >= thr, 1, 0))
    need = topk - n_gt

    pmax_ref[...] = jnp.full((1, tq), t_total, I32)

    @pl.when(jnp.max(n_ge) > topk)
    def _():
        nbits = max(1, (t_total - 1).bit_length())

        def p_body(i, p):
            cand = p | lax.shift_left(jnp.int32(1), nbits - 1 - i)
            f = count(lambda kk, ci: jnp.where(kk == thr, jnp.where(ci * c + row_i < cand, 1, 0), 0))
            return jnp.where(f < need, cand, p)

        pmax_ref[...] = lax.fori_loop(0, nbits, p_body, jnp.zeros((1, tq), I32))

    pmax = pmax_ref[...]

    def bias_body(ci, carry):
        kk = keys_ref[chunk(ci), :]
        s_idx = ci * c + row_i
        sel = jnp.where(kk > thr, 1, jnp.where(kk == thr, jnp.where(s_idx <= pmax, 1, 0), 0))
        ok = jnp.where(s_idx <= t_idx, sel, 0)
        bias_ref[ci] = jnp.where(ok > 0, 0.0, NEG_INF).T
        return carry

    lax.fori_loop(0, nch, bias_body, 0)

    per = LANES // da
    lane = lax.broadcasted_iota(I32, (1, LANES), 1)
    m_ref[...] = jnp.full(m_ref.shape, NEG_INF, F32)
    l_ref[...] = jnp.zeros(l_ref.shape, F32)
    acc_ref[...] = jnp.zeros(acc_ref.shape, F32)

    def att_body(ci, carry):
        bias = bias_ref[ci]
        for h in range(ha):
            g = h // per
            kc = ka_ref[chunk(ci), g * LANES:(g + 1) * LANES]
            vc = va_ref[chunk(ci), g * LANES:(g + 1) * LANES]
            lg = _dot_nt(qam_ref[h], kc) * sca + bias
            m_prev = m_ref[h]
            m_new = jnp.maximum(m_prev, jnp.max(lg, axis=1, keepdims=True))
            m_safe = jnp.where(m_new == NEG_INF, 0.0, m_new)
            p = jnp.exp(lg - jnp.tile(m_safe, (1, c // LANES)))
            alpha = jnp.exp(m_prev - m_safe)
            l_ref[h] = alpha * l_ref[h] + jnp.sum(p, axis=1, keepdims=True)
            acc_ref[h] = alpha * acc_ref[h] + _dot(p.astype(BF16), vc)
            m_ref[h] = m_new
        return carry

    lax.fori_loop(0, nch, att_body, 0)
    for g in range(ha // per):
        out_g = jnp.zeros((tq, LANES), F32)
        for hh in range(per):
            h = g * per + hh
            out_g = jnp.where((lane // da) == hh, acc_ref[h] / l_ref[h], out_g)
        o_ref[:, g * LANES:(g + 1) * LANES] = out_g.astype(o_ref.dtype)


def _dsa_prompt(dm, pr, b, t, tq):
    hi, ha, da, di = dm["HI"], dm["HA"], dm["DA"], dm["DI"]
    wa = ha * da
    topk = min(IDX_TOPK_MAX, t // 4)
    assert t % tq == 0 and topk <= tq and tq % LANES == 0
    nq = t // tq
    cfg = dict(tq=tq, topk=topk, HI=hi, HA=ha, DA=da, T=t,
               sc1=di ** -0.5, sc2=hi ** -0.5, sca=da ** -0.5)
    return pl.pallas_call(
        functools.partial(_dsa_prompt_kernel, cfg),
        grid=(b, nq),
        in_specs=[
            pl.BlockSpec((hi, tq, LANES), lambda bi, j: (0, bi * nq + j, 0)),
            pl.BlockSpec((t, LANES), lambda bi, j: (bi, 0)),
            pl.BlockSpec((dm["HIP"], tq), lambda bi, j: (0, bi * nq + j)),
            pl.BlockSpec((ha, tq, LANES), lambda bi, j: (0, bi * nq + j, 0)),
            pl.BlockSpec((t, wa), lambda bi, j: (bi, 0)),
            pl.BlockSpec((t, wa), lambda bi, j: (bi, 0)),
        ],
        out_specs=pl.BlockSpec((tq, wa), lambda bi, j: (bi * nq + j, 0)),
        out_shape=jax.ShapeDtypeStruct((b * t, wa), BF16),
        scratch_shapes=[pltpu.VMEM((t, tq), I32), pltpu.VMEM((nq, tq, tq), F32), pltpu.VMEM((1, tq), I32),
                        pltpu.VMEM((ha, tq, LANES), F32), pltpu.VMEM((ha, tq, LANES), F32),
                        pltpu.VMEM((ha, tq, LANES), F32)],
        compiler_params=_cparams(("parallel", "arbitrary")),
        name="dsa_prompt",
    )(pr["qim"], pr["ki16"], pr["wit"], pr["qam"], pr["ka16"], pr["va16"])


def _mla_prompt_kernel(cfg, q_ref, k_ref, wuv_ref, o_ref, m_ref, l_ref, a_ref, acc_ref, p_ref):
    tq, ck, lora, scale = cfg["tq"], cfg["ck"], cfg["KV"], cfg["scale"]
    hb, _, kc = q_ref.shape
    rows = hb * tq
    j = pl.program_id(1)
    q = q_ref[...].reshape(rows, kc)
    m_ref[...] = jnp.full(m_ref.shape, NEG_INF, F32)
    l_ref[...] = jnp.zeros(l_ref.shape, F32)
    acc_ref[...] = jnp.zeros(acc_ref.shape, F32)
    t_idx = j * tq + lax.broadcasted_iota(I32, (tq, 1), 0)
    col = lax.broadcasted_iota(I32, (1, ck), 1)
    nch = ((j + 1) * tq + ck - 1) // ck

    def body(ci, carry):
        kk = k_ref[pl.ds(pl.multiple_of(ci * ck, ck), ck), :]
        s = _dot_nt(q, kk) * scale
        causal = ci * ck + col <= t_idx
        for h in range(hb):
            r = slice(h * tq, (h + 1) * tq)
            sh = jnp.where(causal, s[r], NEG_INF)
            m_prev = m_ref[r]
            m_new = jnp.maximum(m_prev, jnp.max(sh, axis=1, keepdims=True))
            p = jnp.exp(sh - jnp.tile(m_new, (1, ck // LANES)))
            alpha = jnp.exp(m_prev - m_new)
            l_ref[r] = alpha * l_ref[r] + jnp.sum(p, axis=1, keepdims=True)
            p_ref[r] = p.astype(BF16)
            a_ref[r] = alpha
            m_ref[r] = m_new
        pv = _dot(p_ref[...], kk[:, :lora])
        for h in range(hb):
            r = slice(h * tq, (h + 1) * tq)
            acc_ref[r] = acc_ref[r] * jnp.tile(a_ref[r], (1, lora // LANES)) + pv[r]
        return carry

    lax.fori_loop(0, nch, body, 0)
    out = None
    for h in range(hb):
        r = slice(h * tq, (h + 1) * tq)
        o = (acc_ref[r] / jnp.tile(l_ref[r], (1, lora // LANES))).astype(BF16)
        out = _dot(o, wuv_ref[h]) if out is None else out + _dot(o, wuv_ref[h])
    o_ref[...] = out.astype(o_ref.dtype)


def _pad_wuv(w_uv):
    hb, lora, bv = w_uv.shape
    w = jnp.zeros((hb, lora, hb * bv), F32)
    for h in range(hb):
        w = w.at[h, :, h * bv:(h + 1) * bv].set(w_uv[h])
    return w.astype(BF16)


def _mla_prompt(dm, pr, wuv_pad, b, t, tq, ck):
    hb, lora, nope, ropeb, bv = dm["HB"], dm["KV"], dm["NOPE"], dm["ROPE"], dm["BV"]
    kc = lora + LANES
    assert t % tq == 0 and t % ck == 0 and ck % LANES == 0
    nq = t // tq
    cfg = dict(tq=tq, ck=ck, KV=lora, scale=(nope + ropeb) ** -0.5)
    rows = hb * tq
    return pl.pallas_call(
        functools.partial(_mla_prompt_kernel, cfg),
        grid=(b, nq),
        in_specs=[
            pl.BlockSpec((hb, tq, kc), lambda bi, j: (0, bi * nq + j, 0)),
            pl.BlockSpec((t, kc), lambda bi, j: (bi, 0)),
            pl.BlockSpec(wuv_pad.shape, lambda bi, j: (0, 0, 0)),
        ],
        out_specs=pl.BlockSpec((tq, hb * bv), lambda bi, j: (bi * nq + j, 0)),
        out_shape=jax.ShapeDtypeStruct((b * t, hb * bv), BF16),
        scratch_shapes=[pltpu.VMEM((rows, LANES), F32), pltpu.VMEM((rows, LANES), F32), pltpu.VMEM((rows, LANES), F32),
                        pltpu.VMEM((rows, lora), F32), pltpu.VMEM((rows, ck), BF16)],
        compiler_params=_cparams(("parallel", "arbitrary")),
        name="mla_prompt",
    )(pr["qcat"], pr["kcat16"], wuv_pad)


def _finish_kernel(x_ref, oa_ref, ob_ref, ga_ref, gb_ref, wpa_ref, wpb_ref, wo_ref, gf_ref, wpq_ref,
                   x1_ref, x1n_ref, q_ref):
    merged = ga_ref[...] * _dot(oa_ref[...], wpa_ref[...]) + gb_ref[...] * _dot(ob_ref[...], wpb_ref[...])
    x1 = x_ref[...] + _dot(merged.astype(BF16), wo_ref[...])
    x1_ref[...] = x1
    ms = jnp.mean(x1 * x1, axis=-1, keepdims=True)
    x1n = (x1 * lax.rsqrt(ms + EPS) * gf_ref[...]).astype(BF16)
    x1n_ref[...] = x1n
    q_ref[...] = _dot(x1n, wpq_ref[...]).astype(q_ref.dtype)


def _finish(x2d, oa, ob, ga, gb, wts, tm):
    n, d = x2d.shape
    assert n % tm == 0
    rows = lambda w: pl.BlockSpec((tm, w), lambda i: (i, 0))
    full = lambda a: pl.BlockSpec(a.shape, lambda i: (0,) * a.ndim)
    wpa, wpb, wo, gf, wpq = wts
    dq = wpq.shape[1]
    return pl.pallas_call(
        _finish_kernel,
        grid=(n // tm,),
        in_specs=[rows(d), rows(oa.shape[1]), rows(ob.shape[1]), rows(d), rows(d),
                  full(wpa), full(wpb), full(wo), full(gf), full(wpq)],
        out_specs=[rows(d), rows(d), rows(dq)],
        out_shape=[jax.ShapeDtypeStruct((n, d), F32), jax.ShapeDtypeStruct((n, d), BF16),
                   jax.ShapeDtypeStruct((n, dq), BF16)],
        compiler_params=_cparams(("parallel",)),
        name="finish",
    )(x2d, oa, ob, ga, gb, wpa, wpb, wo, gf, wpq)


def _oddeven_merge(lo, hi, r):
    step = r * 2
    if step < hi - lo:
        yield from _oddeven_merge(lo, hi, step)
        yield from _oddeven_merge(lo + r, hi, step)
        yield from [(i, i + r) for i in range(lo + r, hi - r, step)]
    else:
        yield (lo, lo + r)


def _oddeven_merge_sort(lo, hi):
    if hi - lo >= 1:
        mid = lo + (hi - lo) // 2
        yield from _oddeven_merge_sort(lo, mid)
        yield from _oddeven_merge_sort(mid + 1, hi)
        yield from _oddeven_merge(lo, hi, 1)


SORT16 = tuple(_oddeven_merge_sort(0, PEER_TOPK - 1))
BITONIC16 = tuple((i, i + k) for k in (8, 4, 2, 1) for i in range(PEER_TOPK) if (i & k) == 0)
SUBLANES = 8


def _compare_exchange(x, net):
    for i, j in net:
        x[i], x[j] = jnp.maximum(x[i], x[j]), jnp.minimum(x[i], x[j])
    return x


def _merge_top16(a, b_rev):
    return _compare_exchange([jnp.maximum(x, y) for x, y in zip(a, b_rev)], BITONIC16)


def _top16_sorted(s):
    x = _compare_exchange([s[i * SUBLANES:(i + 1) * SUBLANES, :] for i in range(PEER_TOPK)], SORT16)
    for shift in (4, 6, 7):
        x = _merge_top16(x, [pltpu.roll(x[PEER_TOPK - 1 - i], shift, 0) for i in range(PEER_TOPK)])
    return x


def _peer_route_kernel(cfg, q_ref, k1_ref, k2_ref, s1_ref, s2_ref, e1_ref, e2_ref, thr_ref):
    hp, dk = cfg["HP"], cfg["DK"]
    tn = q_ref.shape[0]
    sub = lax.broadcasted_iota(I32, (SUBLANES, tn), 0)
    v1p = [jnp.zeros((SUBLANES, tn), F32)] * PEER_TOPK
    v2p = [jnp.zeros((SUBLANES, tn), F32)] * PEER_TOPK
    for h in range(hp):
        qh = q_ref[:, h * dk:(h + 1) * dk]
        s1 = _dot_nt(k1_ref[h], qh)
        s2 = _dot_nt(k2_ref[h], qh)
        s1_ref[0, h] = s1
        s2_ref[0, h] = s2
        t1 = _top16_sorted(s1)
        t2 = _top16_sorted(s2)
        v1p = [jnp.where(sub == h, a, b) for a, b in zip(t1, v1p)]
        v2p = [jnp.where(sub == h, a, b) for a, b in zip(t2, v2p)]
    cur = [v1p[0] + v2p[r2] for r2 in range(PEER_TOPK)]
    for r1 in range(1, PEER_TOPK):
        lim = PEER_TOPK // (r1 + 1)
        b_rev = [v1p[r1] + v2p[PEER_TOPK - 1 - i] if i >= PEER_TOPK - lim else None for i in range(PEER_TOPK)]
        cur = _compare_exchange([a if b is None else jnp.maximum(a, b) for a, b in zip(cur, b_rev)], BITONIC16)
    thr = cur[PEER_TOPK - 1]
    z = jnp.zeros((SUBLANES, tn), F32)
    for r in range(PEER_TOPK):
        z = z + jnp.exp(cur[r] - cur[0])
    thr_ref[0] = thr
    rz = 1.0 / z
    for h in range(hp):
        e1_ref[0, h] = jnp.exp(s1_ref[0, h] - v1p[0][h:h + 1, :]) * rz[h:h + 1, :]
        e2_ref[0, h] = jnp.exp(s2_ref[0, h] - v2p[0][h:h + 1, :])


def _peer_route(dm, q, k1p, k2p, tn):
    n = q.shape[0]
    hp, nk, dk = dm["HP"], dm["NK"], dm["DK"]
    assert n % tn == 0 and nk == PEER_TOPK * SUBLANES and hp <= SUBLANES and dk % LANES == 0
    big = pl.BlockSpec((1, hp, nk, tn), lambda i: (i, 0, 0, 0))
    sds = jax.ShapeDtypeStruct((n // tn, hp, nk, tn), F32)
    return pl.pallas_call(
        functools.partial(_peer_route_kernel, dict(HP=hp, DK=dk)),
        grid=(n // tn,),
        in_specs=[pl.BlockSpec((tn, hp * dk), lambda i: (i, 0)),
                  pl.BlockSpec(k1p.shape, lambda i: (0, 0, 0)), pl.BlockSpec(k2p.shape, lambda i: (0, 0, 0))],
        out_specs=[big, big, big, big, pl.BlockSpec((1, SUBLANES, tn), lambda i: (i, 0, 0))],
        out_shape=[sds, sds, sds, sds, jax.ShapeDtypeStruct((n // tn, SUBLANES, tn), F32)],
        compiler_params=_cparams(("parallel",)),
        name="peer_route",
    )(q, k1p, k2p)


def _peer_dense_kernel(cfg, x_ref, x1_ref, u_ref, vt_ref, s1_ref, s2_ref, e1_ref, e2_ref, thr_ref,
                       y_ref, acc_ref, act_ref, g_ref):
    hp, nk = cfg["HP"], cfg["NK"]
    eb = u_ref.shape[0]
    nts, tl = s1_ref.shape[0], s1_ref.shape[-1]
    e = pl.program_id(1)

    @pl.when(e == 0)
    def _():
        acc_ref[...] = jnp.zeros(acc_ref.shape, F32)

    a = _dot_nt(u_ref[...], x_ref[...])
    act_ref[...] = 0.5 * a * (1.0 + lax.erf(a * (2.0 ** -0.5)))

    for ts in range(nts):
        lanes = slice(ts * tl, (ts + 1) * tl)

        def key_body(k, carry, ts=ts, lanes=lanes):
            i1 = e * (eb // nk) + k
            g = None
            for h in range(hp):
                cand = s1_ref[ts, h, pl.ds(i1, 1), :] + s2_ref[ts, h]
                w = jnp.where(cand >= thr_ref[ts, h:h + 1, :], e1_ref[ts, h, pl.ds(i1, 1), :] * e2_ref[ts, h], 0.0)
                g = w if g is None else g + w
            rows = pl.ds(pl.multiple_of(k * nk, nk), nk)
            g_ref[rows, lanes] = (g * act_ref[rows, lanes]).astype(BF16)
            return carry

        lax.fori_loop(0, eb // nk, key_body, 0)
    acc_ref[...] += _dot(vt_ref[...], g_ref[...])

    @pl.when(e == pl.num_programs(1) - 1)
    def _():
        y_ref[...] = x1_ref[...] + acc_ref[...].T


def _peer_dense(dm, x1n, x1, u16, vt16, route, tn, eb):
    n, d = x1.shape
    ne = u16.shape[0]
    hp, nk = dm["HP"], dm["NK"]
    s1, s2, e1, e2, thr = route
    tl = s1.shape[-1]
    assert n % tn == 0 and ne % eb == 0 and eb % nk == 0 and tn % tl == 0
    nts = tn // tl
    big = pl.BlockSpec((nts, hp, nk, tl), lambda i, e: (i, 0, 0, 0))
    return pl.pallas_call(
        functools.partial(_peer_dense_kernel, dict(HP=hp, NK=nk)),
        grid=(n // tn, ne // eb),
        in_specs=[pl.BlockSpec((tn, d), lambda i, e: (i, 0)), pl.BlockSpec((tn, d), lambda i, e: (i, 0)),
                  pl.BlockSpec((eb, d), lambda i, e: (e, 0)), pl.BlockSpec((d, eb), lambda i, e: (0, e)),
                  big, big, big, big, pl.BlockSpec((nts, SUBLANES, tl), lambda i, e: (i, 0, 0))],
        out_specs=pl.BlockSpec((tn, d), lambda i, e: (i, 0)),
        out_shape=jax.ShapeDtypeStruct((n, d), F32),
        scratch_shapes=[pltpu.VMEM((d, tn), F32), pltpu.VMEM((eb, tn), F32), pltpu.VMEM((eb, tn), BF16)],
        compiler_params=_cparams(("parallel", "arbitrary")),
        name="peer_dense",
    )(x1n, x1, u16, vt16, s1, s2, e1, e2, thr)


def _tail(dm, x2d, oa, ob, ga, gb, tail_w, peer_w, tm, tn_route, tn_dense, eb):
    x1, x1n, q = _finish(x2d, oa, ob, ga, gb, tail_w, tm)
    k1p, k2p, u16, vt16 = peer_w
    route = _peer_route(dm, q, k1p, k2p, tn_route)
    return _peer_dense(dm, x1n, x1, u16, vt16, route, tn_dense, eb)


def _page_specs(n, shape, pps):
    zeros = (0,) * len(shape)
    return [pl.BlockSpec((1,) + shape, (lambda b, j, pt, k=k: (pt[b, j * pps + k],) + zeros)) for k in range(n)]


def _sample_scores_kernel(cfg, pt_ref, q_ref, w_ref, *refs):
    pps, sc1, sc2 = cfg["pps"], cfg["sc1"], cfg["sc2"]
    pages, out_ref = refs[:pps], refs[pps]
    kt = jnp.concatenate([r[0] for r in pages], axis=1).astype(BF16)
    s = _dot(q_ref[0], kt) * sc1
    w = jnp.tile(w_ref[0], (1, pps))
    out_ref[0] = jnp.sum(jnp.maximum(s, 0.0) * w, axis=0, keepdims=True) * sc2


def _sample_scores(dm, qi, wcol, idx_t, page_table, pps):
    bs, n_pages = page_table.shape
    hi, di = dm["HI"], dm["DI"]
    page = idx_t.shape[-1]
    cfg = dict(pps=pps, sc1=di ** -0.5, sc2=hi ** -0.5)
    grid_spec = pltpu.PrefetchScalarGridSpec(
        num_scalar_prefetch=1, grid=(bs, n_pages // pps),
        in_specs=[pl.BlockSpec((1, hi, di), lambda b, j, pt: (b, 0, 0)),
                  pl.BlockSpec((1, hi, page), lambda b, j, pt: (b, 0, 0))]
                 + _page_specs(pps, (di, page), pps),
        out_specs=pl.BlockSpec((1, 1, pps * page), lambda b, j, pt: (b, 0, j)))
    return pl.pallas_call(
        functools.partial(_sample_scores_kernel, cfg), grid_spec=grid_spec,
        out_shape=jax.ShapeDtypeStruct((bs, 1, n_pages * page), F32),
        compiler_params=_cparams(("parallel", "arbitrary")), name="sample_scores",
    )(page_table, qi, wcol, *([idx_t] * pps))


def _sample_select_kernel(cfg, s_ref, qi_ref, ki_ref, w_ref, bd_ref, bias_ref, bnew_ref, keys_ref):
    slots, topk, past, sc1, sc2 = cfg["slots"], cfg["topk"], cfg["past"], cfg["sc1"], cfg["sc2"]
    bsz, width = s_ref.shape[1], s_ref.shape[2]
    prod = qi_ref[...] * ki_ref[...].astype(BF16).astype(F32)
    hs = _dot_hilo(prod, bd_ref[...])
    s_new = jnp.sum(jnp.maximum(hs * sc1, 0.0) * w_ref[...], axis=1, keepdims=True) * sc2
    k_new = _sortable_key(s_new)
    for kk in range(slots):
        keys_ref[kk] = _sortable_key(s_ref[kk])
    lane = lax.broadcasted_iota(I32, (bsz, width), 1)

    def count(ind, ind_new):
        tot = ind_new
        for kk in range(slots):
            tot = tot + jnp.sum(ind(keys_ref[kk], kk), axis=1, keepdims=True)
        return tot

    def bit_body(i, prefix):
        cand_u = prefix | lax.shift_left(jnp.int32(1), 31 - i)
        cand_s = cand_u ^ INT_MIN
        cnt = count(lambda k, kk: jnp.where(k >= cand_s, 1.0, 0.0), jnp.where(k_new >= cand_s, 1.0, 0.0))
        return jnp.where(cnt >= topk, cand_u, prefix)

    thr = lax.fori_loop(0, 32, bit_body, jnp.zeros((bsz, 1), I32)) ^ INT_MIN
    n_gt = count(lambda k, kk: jnp.where(k > thr, 1.0, 0.0), jnp.where(k_new > thr, 1.0, 0.0))
    need = topk - n_gt
    nbits = max(1, past.bit_length())

    def p_body(i, p):
        cand = p | lax.shift_left(jnp.int32(1), nbits - 1 - i)
        f = count(lambda k, kk: jnp.where(k == thr, jnp.where(lane * slots + kk < cand, 1.0, 0.0), 0.0),
                  jnp.where(k_new == thr, jnp.where(past < cand, 1.0, 0.0), 0.0))
        return jnp.where(f < need, cand, p)

    pmax = lax.fori_loop(0, nbits, p_body, jnp.zeros((bsz, 1), I32))
    for kk in range(slots):
        k = keys_ref[kk]
        sel = jnp.where(k > thr, 1, jnp.where(k == thr, jnp.where(lane * slots + kk <= pmax, 1, 0), 0))
        bias_ref[kk] = jnp.where(sel > 0, 0.0, NEG_INF)
    sel_new = jnp.where(k_new > thr, 1, jnp.where(k_new == thr, jnp.where(past <= pmax, 1, 0), 0))
    bnew_ref[...] = jnp.broadcast_to(jnp.where(sel_new > 0, 0.0, NEG_INF), bnew_ref.shape)


def _sample_select(dm, scores_t, qi_flat, ki_rep, w_pad, bd, past):
    slots, bs, width = scores_t.shape
    hi, di = dm["HI"], dm["DI"]
    topk = min(IDX_TOPK_MAX, (past + 1) // 4)
    cfg = dict(slots=slots, topk=topk, past=past, sc1=di ** -0.5, sc2=hi ** -0.5)
    return pl.pallas_call(
        functools.partial(_sample_select_kernel, cfg),
        out_shape=[jax.ShapeDtypeStruct((slots, bs, width), F32), jax.ShapeDtypeStruct((bs, LANES), F32)],
        scratch_shapes=[pltpu.VMEM((slots, bs, width), I32)],
        compiler_params=pltpu.CompilerParams(vmem_limit_bytes=VMEM_LIMIT), name="sample_select",
    )(scores_t, qi_flat, ki_rep, w_pad, bd)


def _online_softmax_step(m_ref, l_ref, acc_ref, lg, pv_fn):
    m_prev = m_ref[...]
    m_new = jnp.maximum(m_prev, jnp.max(lg, axis=1, keepdims=True))
    m_safe = jnp.where(m_new == NEG_INF, 0.0, m_new)
    p = jnp.exp(lg - m_safe)
    alpha = jnp.exp(m_prev - m_safe)
    l_ref[...] = alpha * l_ref[...] + jnp.sum(p, axis=1, keepdims=True)
    acc_ref[...] = alpha * acc_ref[...] + pv_fn(p)
    m_ref[...] = m_new


def _sample_dsa_kernel(cfg, pt_ref, q_ref, bias_ref, bnew_ref, knew_ref, vnew_ref, *refs):
    pps, sca = cfg["pps"], cfg["sca"]
    kpages, vpages = refs[:pps], refs[pps:2 * pps]
    o_ref, m_ref, l_ref, acc_ref = refs[2 * pps:]
    j = pl.program_id(1)
    heads, wa = q_ref.shape[1], q_ref.shape[2]
    da = wa // heads
    page = kpages[0].shape[-1]

    @pl.when(j == 0)
    def _():
        m_ref[...] = jnp.full(m_ref.shape, NEG_INF, F32)
        l_ref[...] = jnp.zeros(l_ref.shape, F32)
        acc_ref[...] = jnp.zeros(acc_ref.shape, F32)

    q = q_ref[0]
    kt = jnp.concatenate([r[0].reshape(wa, page) for r in kpages], axis=1).astype(BF16)
    vt = jnp.concatenate([r[0].reshape(wa, page) for r in vpages], axis=1).astype(BF16)
    lg = _dot(q, kt) * sca + bias_ref[0]
    _online_softmax_step(m_ref, l_ref, acc_ref, lg, lambda p: _dot_nt(p.astype(BF16), vt))

    @pl.when(j == pl.num_programs(1) - 1)
    def _():
        lgn = jnp.sum(q.astype(F32) * knew_ref[0].astype(F32), axis=1, keepdims=True) * sca + bnew_ref[0][:, 0:1]
        _online_softmax_step(m_ref, l_ref, acc_ref, lgn, lambda p: p * vnew_ref[0].astype(F32))
        o = acc_ref[...] / l_ref[...]
        row = lax.broadcasted_iota(I32, (heads, wa), 0)
        lane = lax.broadcasted_iota(I32, (heads, wa), 1)
        o_ref[0] = jnp.sum(jnp.where(lane // da == row, o, 0.0), axis=0, keepdims=True).astype(o_ref.dtype)


def _sample_dsa(dm, qbd, bias, bnew, knew, vnew, kt, vt, page_table, pps):
    bs, n_pages = page_table.shape
    ha, da = dm["HA"], dm["DA"]
    wa = ha * da
    page = kt.shape[-1]
    cfg = dict(pps=pps, sca=da ** -0.5)
    per_b = lambda shape: pl.BlockSpec((1,) + shape, lambda b, j, pt: (b, 0, 0))
    grid_spec = pltpu.PrefetchScalarGridSpec(
        num_scalar_prefetch=1, grid=(bs, n_pages // pps),
        in_specs=[per_b((ha, wa)), pl.BlockSpec((1, 1, pps * page), lambda b, j, pt: (b, 0, j)),
                  per_b((1, LANES)), per_b((1, wa)), per_b((1, wa))]
                 + _page_specs(pps, (ha, da, page), pps) + _page_specs(pps, (ha, da, page), pps),
        out_specs=per_b((1, wa)),
        scratch_shapes=[pltpu.VMEM((ha, 1), F32), pltpu.VMEM((ha, 1), F32), pltpu.VMEM((ha, wa), F32)])
    return pl.pallas_call(
        functools.partial(_sample_dsa_kernel, cfg), grid_spec=grid_spec,
        out_shape=jax.ShapeDtypeStruct((bs, 1, wa), BF16),
        compiler_params=_cparams(("parallel", "arbitrary")), name="sample_dsa",
    )(page_table, qbd, bias, bnew, knew, vnew, *([kt] * pps), *([vt] * pps))


def _sample_mla_kernel(cfg, pt_ref, qabs_ref, qpe_ref, qfull_ref, knew_ref, wuv_ref, *refs):
    pps, lora, scale = cfg["pps"], cfg["KV"], cfg["scale"]
    cpages, ppages = refs[:pps], refs[pps:2 * pps]
    o_ref, m_ref, l_ref, acc_ref = refs[2 * pps:]
    j = pl.program_id(1)
    heads = qabs_ref.shape[1]

    @pl.when(j == 0)
    def _():
        m_ref[...] = jnp.full(m_ref.shape, NEG_INF, F32)
        l_ref[...] = jnp.zeros(l_ref.shape, F32)
        acc_ref[...] = jnp.zeros(acc_ref.shape, F32)

    c_all = jnp.concatenate([r[0] for r in cpages], axis=0).astype(BF16)
    p_all = jnp.concatenate([r[0] for r in ppages], axis=1).astype(BF16)
    lg = (_dot_nt(qabs_ref[0], c_all) + _dot(qpe_ref[0], p_all)) * scale
    _online_softmax_step(m_ref, l_ref, acc_ref, lg, lambda p: _dot(p.astype(BF16), c_all))

    @pl.when(j == pl.num_programs(1) - 1)
    def _():
        kn = knew_ref[0].astype(F32)
        lgn = jnp.sum(qfull_ref[0].astype(F32) * kn, axis=1, keepdims=True) * scale
        _online_softmax_step(m_ref, l_ref, acc_ref, lgn, lambda p: p * kn[:, :lora])
        o = (acc_ref[...] / l_ref[...]).astype(BF16)
        row = lax.broadcasted_iota(I32, (heads, o_ref.shape[-1]), 0)
        out = jnp.zeros((heads, o_ref.shape[-1]), F32)
        for h in range(heads):
            out = out + jnp.where(row == h, _dot(o, wuv_ref[h]), 0.0)
        o_ref[0] = jnp.sum(out, axis=0, keepdims=True).astype(o_ref.dtype)


def _sample_mla(dm, qabs, qpe, qfull, knew, wuv_pad, cache_ckv, kpe_t, page_table, pps):
    bs, n_pages = page_table.shape
    hb, lora, nope, ropeb, bv = dm["HB"], dm["KV"], dm["NOPE"], dm["ROPE"], dm["BV"]
    page = cache_ckv.shape[1]
    cfg = dict(pps=pps, KV=lora, scale=(nope + ropeb) ** -0.5)
    per_b = lambda shape: pl.BlockSpec((1,) + shape, lambda b, j, pt: (b, 0, 0))
    grid_spec = pltpu.PrefetchScalarGridSpec(
        num_scalar_prefetch=1, grid=(bs, n_pages // pps),
        in_specs=[per_b((hb, lora)), per_b((hb, ropeb)), per_b((hb, lora + LANES)), per_b((1, lora + LANES)),
                  pl.BlockSpec(wuv_pad.shape, lambda b, j, pt: (0, 0, 0))]
                 + _page_specs(pps, (page, lora), pps) + _page_specs(pps, (ropeb, page), pps),
        out_specs=per_b((1, hb * bv)),
        scratch_shapes=[pltpu.VMEM((hb, 1), F32), pltpu.VMEM((hb, 1), F32), pltpu.VMEM((hb, lora), F32)])
    return pl.pallas_call(
        functools.partial(_sample_mla_kernel, cfg), grid_spec=grid_spec,
        out_shape=jax.ShapeDtypeStruct((bs, 1, hb * bv), BF16),
        compiler_params=_cparams(("parallel", "arbitrary")), name="sample_mla",
    )(page_table, qabs, qpe, qfull, knew, wuv_pad, *([cache_ckv] * pps), *([kpe_t] * pps))


def _unmask_heads(xm, d):
    per = LANES // d
    return jnp.stack([xm[h][:, (h % per) * d:(h % per + 1) * d] for h in range(xm.shape[0])], axis=1)


def _sample_attention(dm, ps, cache_k, cache_v, cache_idx_k, cache_ckv, cache_kpe, page_table, wuv_pad):
    bs, n_pages = page_table.shape
    page = cache_k.shape[1]
    past = page * n_pages
    ha, da, hi, di = dm["HA"], dm["DA"], dm["HI"], dm["DI"]
    hb, lora, ropeb = dm["HB"], dm["KV"], dm["ROPE"]
    wa = ha * da
    pps_idx, pps_mla, pps_dsa = math.gcd(n_pages, 64), math.gcd(n_pages, 32), math.gcd(n_pages, 16)
    idx_t = jnp.transpose(cache_idx_k, (0, 2, 1))
    kpe_t = jnp.transpose(cache_kpe, (0, 2, 1))
    k_t = jnp.transpose(cache_k, (0, 2, 3, 1))
    v_t = jnp.transpose(cache_v, (0, 2, 3, 1))

    qi = _unmask_heads(ps["qim"], di)
    w_i = ps["wit"][:hi].T
    wcol = jnp.broadcast_to(w_i[:, :, None], (bs, hi, page))
    scores = _sample_scores(dm, qi, wcol, idx_t, page_table, pps_idx)
    bd = _block_diag_ones(hi * di, di, LANES, 1)
    w_pad = jnp.zeros((bs, LANES), F32).at[:, :hi].set(w_i)
    ki_rep = jnp.tile(ps["ki32"][:, :di], (1, hi))
    bias_t, bnew = _sample_select(dm, scores.reshape(1, bs, past), qi.reshape(bs, hi * di).astype(F32),
                                  ki_rep, w_pad, bd, past)

    per = LANES // da
    qbd = jnp.zeros((bs, ha, wa), BF16)
    for h in range(ha):
        g = h // per
        qbd = qbd.at[:, h, g * LANES:(g + 1) * LANES].set(ps["qam"][h])
    oa = _sample_dsa(dm, qbd, bias_t.reshape(bs, 1, past), bnew.reshape(bs, 1, LANES), ps["ka16"].reshape(bs, 1, wa),
                     ps["va16"].reshape(bs, 1, wa), k_t, v_t, page_table, pps_dsa)

    qfull = jnp.transpose(ps["qcat"], (1, 0, 2))
    qpe = _unmask_heads(ps["qcat"][:, :, lora:], ropeb)
    ob = _sample_mla(dm, qfull[:, :, :lora], qpe, qfull, ps["kcat16"].reshape(bs, 1, lora + LANES), wuv_pad,
                     cache_ckv, kpe_t, page_table, pps_mla)
    return oa.reshape(bs, wa), ob.reshape(bs, -1)


def _model_dims(cache_k, cache_idx_k, cache_ckv, cache_kpe, w_in, g_cq, w_uq, w_uk, w_uv, sub_k1):
    ha, da = cache_k.shape[2], cache_k.shape[3]
    di, lora, ropeb = cache_idx_k.shape[-1], cache_ckv.shape[-1], cache_kpe.shape[-1]
    ql, hb, nope, bv = g_cq.shape[0], w_uq.shape[1], w_uk.shape[1], w_uv.shape[2]
    d = w_in.shape[0]
    rest = w_in.shape[1] - (3 * ha * da + di + ql + lora + ropeb + 2 * d)
    hi = rest // (di + 1)
    assert hi * (di + 1) == rest
    for v in (da, di, ropeb):
        assert LANES % v == 0
    for v in (ha * da, hi * di, hb * nope, hb * ropeb, lora, ql, d):
        assert v % LANES == 0
    return dict(HA=ha, DA=da, HI=hi, DI=di, HB=hb, NOPE=nope, ROPE=ropeb, KV=lora, QL=ql, BV=bv, D=d,
                HP=sub_k1.shape[0], NK=sub_k1.shape[1], DK=2 * sub_k1.shape[2])


def kernel(x_prompt, x_sample, cache_k, cache_v, cache_idx_k, cache_ckv, cache_kpe, page_table, g_attn, w_in, g_qa, g_ka, g_cq, w_uq, g_qb, w_uk, g_ckv, g_kpe, w_uv, w_pa, w_pb, w_o, g_ffn, w_pq, sub_k1, sub_k2, u_tab, v_tab):
    dm = _model_dims(cache_k, cache_idx_k, cache_ckv, cache_kpe, w_in, g_cq, w_uq, w_uk, w_uv, sub_k1)
    dm, consts = _prep_proj_weights(dm, g_attn, w_in, g_qa, g_ka, g_cq, w_uq, g_qb, w_uk, g_ckv, g_kpe)
    b, t, d = x_prompt.shape
    bs, ts, _ = x_sample.shape
    page, n_pages = cache_k.shape[1], page_table.shape[1]
    past = page * n_pages
    ha, da, di, lora, ropeb = dm["HA"], dm["DA"], dm["DI"], dm["KV"], dm["ROPE"]
    hp, nk, dk = dm["HP"], dm["NK"], dm["DK"]

    wuv_pad = _pad_wuv(w_uv)
    tail_w = (w_pa.astype(BF16), w_pb.astype(BF16), w_o.astype(BF16), g_ffn.reshape(1, -1), w_pq.astype(BF16))
    zeros_half = jnp.zeros((hp, nk, dk // 2), F32)
    k1p = jnp.concatenate([sub_k1, zeros_half], axis=-1).astype(BF16)
    k2p = jnp.concatenate([zeros_half, sub_k2], axis=-1).astype(BF16)
    peer_w = (k1p, k2p, u_tab.astype(BF16), v_tab.T.astype(BF16))

    n_p = b * t
    x_p = x_prompt.reshape(n_p, d)
    pp = _project(dm, consts, x_p, jnp.arange(t, dtype=I32), min(256, t))
    oa_p = _dsa_prompt(dm, pp, b, t, min(256, t))
    ob_p = _mla_prompt(dm, pp, wuv_pad, b, t, min(128, t), min(256, t))
    y_p = _tail(dm, x_p, oa_p, ob_p, pp["ga"], pp["gb"], tail_w, peer_w, min(256, n_p), 128, min(512, n_p), 1024)

    n_s = bs * ts
    assert ts == 1
    x_s = x_sample.reshape(n_s, d)
    ps = _project(dm, consts, x_s, jnp.full((n_s,), past, I32), n_s)
    oa_s, ob_s = _sample_attention(dm, ps, cache_k, cache_v, cache_idx_k, cache_ckv, cache_kpe, page_table, wuv_pad)
    y_s = _tail(dm, x_s, oa_s, ob_s, ps["ga"], ps["gb"], tail_w, peer_w, n_s, n_s, n_s, 1024)

    def kv_outs(p, bb, tt):
        return (p["ka32"].reshape(bb, tt, ha, da), p["va32"].reshape(bb, tt, ha, da),
                p["ki32"][:, :di].reshape(bb, tt, di), p["ckv32"].reshape(bb, tt, lora),
                p["kpe32"][:, :ropeb].reshape(bb, tt, ropeb))

    return (y_p.reshape(b, t, d), y_s.reshape(bs, ts, d)) + kv_outs(pp, b, t) + kv_outs(ps, bs, ts)
```

```python
import functools
import math

import numpy as np
import jax
import jax.numpy as jnp
from jax import lax
from jax.experimental import pallas as pl
from jax.experimental.pallas import tpu as pltpu

F32 = jnp.float32
BF16 = jnp.bfloat16
I32 = jnp.int32

ROPE_THETA = 10000.0
EPS = 1e-6
IDX_TOPK_MAX = 256
PEER_TOPK = 16
LANES = 128
NEG_INF = float("-inf")
VMEM_LIMIT = 56 * 1024 * 1024


def _cparams(sem):
    return pltpu.CompilerParams(dimension_semantics=sem, vmem_limit_bytes=VMEM_LIMIT)


def _dot(a, b):
    return jnp.dot(a, b, preferred_element_type=F32)


def _dot_nt(a, b):
    return lax.dot_general(a, b, (((1,), (1,)), ((), ())), preferred_element_type=F32)


def _dot_hilo(a, b):
    hi = a.astype(BF16)
    lo = (a - hi.astype(F32)).astype(BF16)
    return _dot(hi, b) + _dot(lo, b)


def _rope(x, c, s, half):
    w = x.shape[-1]
    fwd = pltpu.roll(x, half, 1)
    bwd = pltpu.roll(x, w - half, 1)
    lane = lax.broadcasted_iota(I32, (1, w), 1)
    first = (lane & (2 * half - 1)) < half
    return x * c + jnp.where(first, bwd, fwd) * s


def _group_masked(x, h, d):
    per = LANES // d
    g = h // per
    grp = x[:, g * LANES:(g + 1) * LANES]
    lane = lax.broadcasted_iota(I32, (1, LANES), 1)
    return jnp.where((lane // d) == (h % per), grp, 0.0)


def _proj_kernel(dm, x_ref, gat_ref, w_ref, wwi_ref, c64_ref, s64_ref, c32_ref, s32_ref,
                 gqa_ref, gka_ref, gcq_ref, gqbn_ref, gqbr_ref, gckv_ref, gkpe_ref,
                 bda_ref, bnr_ref, brn_ref, bdr_ref, wuq_ref, wuk_ref,
                 ka32, va32, ki32, ckv32, kpe32, ga, gb, ka16, va16, ki16, kcat16,
                 qam, qim, wit, qcat):
    off = dm["off"]
    x = x_ref[...]
    ms = jnp.mean(x * x, axis=-1, keepdims=True)
    h = (x * lax.rsqrt(ms + EPS) * gat_ref[...]).astype(BF16)

    def seg(name, n):
        a = off[name]
        return _dot(h, w_ref[:, a:a + n])

    ha, da, hi, di = dm["HA"], dm["DA"], dm["HI"], dm["DI"]
    hb, nope, ropeb, lora = dm["HB"], dm["NOPE"], dm["ROPE"], dm["KV"]
    wa = ha * da
    c64, s64 = c64_ref[...], s64_ref[...]
    c32, s32 = c32_ref[...], s32_ref[...]

    z = seg("qa", wa)
    q = z * lax.rsqrt(_dot_hilo(z * z, bda_ref[...]) * (1.0 / da) + EPS) * gqa_ref[...]
    q = _rope(q, c64, s64, da // 2)
    for hh in range(ha):
        qam[hh] = _group_masked(q, hh, da).astype(BF16)
    z = seg("ka", wa)
    k = z * lax.rsqrt(_dot_hilo(z * z, bda_ref[...]) * (1.0 / da) + EPS) * gka_ref[...]
    k = _rope(k, c64, s64, da // 2)
    ka32[...] = k
    ka16[...] = k.astype(BF16)
    z = seg("va", wa)
    va32[...] = z
    va16[...] = z.astype(BF16)

    z = seg("qi", hi * di)
    q = _rope(z, c32, s32, di // 2)
    for hh in range(hi):
        qim[hh] = _group_masked(q, hh, di).astype(BF16)
    wit[...] = _dot_nt(wwi_ref[...], h)
    z = seg("ki", LANES)
    k = _rope(z, c32[:, :LANES], s32[:, :LANES], di // 2)
    ki32[...] = k
    ki16[...] = k.astype(BF16)

    z = seg("cq", dm["QL"])
    zn = (z * lax.rsqrt(jnp.mean(z * z, axis=-1, keepdims=True) + EPS) * gcq_ref[...]).astype(BF16)
    qb = _dot(zn, wuq_ref[...])
    wn = hb * nope
    qn, qr = qb[:, :wn], qb[:, wn:]
    zzn, zzr = qn * qn, qr * qr
    totn = _dot_hilo(zzn, bda_ref[...]) + _dot_hilo(zzr, brn_ref[...])
    totr = _dot_hilo(zzn, bnr_ref[...]) + _dot_hilo(zzr, bdr_ref[...])
    inv_d = 1.0 / (nope + ropeb)
    qn = qn * lax.rsqrt(totn * inv_d + EPS) * gqbn_ref[...]
    qr = qr * lax.rsqrt(totr * inv_d + EPS) * gqbr_ref[...]
    qpe = _rope(qr, c32, s32, ropeb // 2)
    qabs = _dot(qn.astype(BF16), wuk_ref[...])
    for hh in range(hb):
        qcat[hh, :, 0:lora] = qabs[:, hh * lora:(hh + 1) * lora].astype(BF16)
        qcat[hh, :, lora:lora + LANES] = _group_masked(qpe, hh, ropeb).astype(BF16)

    z = seg("ckv", lora)
    ckv = z * lax.rsqrt(jnp.mean(z * z, axis=-1, keepdims=True) + EPS) * gckv_ref[...]
    ckv32[...] = ckv
    kcat16[:, 0:lora] = ckv.astype(BF16)
    z = seg("kpe", LANES)
    kp = z * lax.rsqrt(jnp.mean(z * z, axis=-1, keepdims=True) + EPS) * gkpe_ref[...]
    kp = _rope(kp, c32[:, :LANES], s32[:, :LANES], ropeb // 2)
    kpe32[...] = kp
    kcat16[:, lora:lora + LANES] = kp.astype(BF16)

    d = x.shape[-1]
    ga[...] = jax.nn.sigmoid(seg("ga", d))
    gb[...] = jax.nn.sigmoid(seg("gb", d))


def _rope_tables(pos, d, reps):
    half = d // 2
    inv = ROPE_THETA ** (-(jnp.arange(half, dtype=F32) * 2.0 / d))
    ang = pos.astype(F32)[:, None] * inv[None, :]
    cos, sin = jnp.cos(ang), jnp.sin(ang)
    c = jnp.concatenate([cos, cos], axis=-1)
    s = jnp.concatenate([-sin, sin], axis=-1)
    return jnp.tile(c, (1, reps)), jnp.tile(s, (1, reps))


def _block_diag_ones(rows, rd, cols, cd):
    r = np.arange(rows)[:, None] // rd
    c = np.arange(cols)[None, :] // cd
    return jnp.asarray((r == c).astype(np.float32), dtype=BF16)


def _prep_proj_weights(dm, g_attn, w_in, g_qa, g_ka, g_cq, w_uq, g_qb, w_uk, g_ckv, g_kpe):
    ha, da, hi, di = dm["HA"], dm["DA"], dm["HI"], dm["DI"]
    hb, nope, ropeb, lora, ql = dm["HB"], dm["NOPE"], dm["ROPE"], dm["KV"], dm["QL"]
    d = w_in.shape[0]
    sizes = (ha * da, ha * da, ha * da, hi * di, hi, di, ql, lora, ropeb, d, d)
    parts, o = [], 0
    for n in sizes:
        parts.append(w_in[:, o:o + n])
        o += n
    w_qa, w_ka, w_va, w_qi, w_wi, w_ki, w_cq, w_ckv, w_kpe, w_ga, w_gb = parts
    segs = [("qa", w_qa), ("ka", w_ka), ("va", w_va), ("qi", w_qi),
            ("ki", jnp.tile(w_ki, (1, LANES // di))), ("cq", w_cq), ("ckv", w_ckv),
            ("kpe", jnp.tile(w_kpe, (1, LANES // ropeb))), ("ga", w_ga), ("gb", w_gb)]
    off, o = {}, 0
    for name, w in segs:
        off[name] = o
        o += w.shape[1]
    w_all = jnp.concatenate([w for _, w in segs], axis=1).astype(BF16)
    hip = -(-hi // 8) * 8
    wwi = jnp.zeros((hip, d), F32).at[:hi].set(w_wi.T).astype(BF16)
    wuq = jnp.concatenate([w_uq[:, :, :nope].reshape(ql, hb * nope),
                           w_uq[:, :, nope:].reshape(ql, hb * ropeb)], axis=1).astype(BF16)
    wuk = jnp.zeros((hb * nope, hb * lora), F32)
    for hh in range(hb):
        wuk = wuk.at[hh * nope:(hh + 1) * nope, hh * lora:(hh + 1) * lora].set(w_uk[hh])
    wuk = wuk.astype(BF16)
    row = lambda v: v.reshape(1, -1).astype(F32)
    consts = dict(
        gat=row(g_attn), w=w_all, wwi=wwi,
        gqa=row(jnp.tile(g_qa, ha)), gka=row(jnp.tile(g_ka, ha)), gcq=row(g_cq),
        gqbn=row(jnp.tile(g_qb[:nope], hb)), gqbr=row(jnp.tile(g_qb[nope:], hb)),
        gckv=row(g_ckv), gkpe=row(jnp.tile(g_kpe, LANES // ropeb)),
        bda=_block_diag_ones(ha * da, da, ha * da, da),
        bnr=_block_diag_ones(hb * nope, nope, hb * ropeb, ropeb),
        brn=_block_diag_ones(hb * ropeb, ropeb, hb * nope, nope),
        bdr=_block_diag_ones(hb * ropeb, ropeb, hb * ropeb, ropeb),
        wuq=wuq, wuk=wuk)
    dm = dict(dm, off=off, HIP=hip, WTOT=o)
    return dm, consts


def _project(dm, consts, x2d, pos, tm):
    n, d = x2d.shape
    p = pos.shape[0]
    assert n % tm == 0 and p % tm == 0
    npb = p // tm
    ha, da, hi, di = dm["HA"], dm["DA"], dm["HI"], dm["DI"]
    hb, ropeb, lora = dm["HB"], dm["ROPE"], dm["KV"]
    wa = ha * da
    c64, s64 = _rope_tables(pos, da, ha)
    c32, s32 = _rope_tables(pos, di, hi)
    full = lambda a: pl.BlockSpec(a.shape, lambda i: (0,) * a.ndim)
    rows = lambda w: pl.BlockSpec((tm, w), lambda i: (i, 0))
    prow = lambda w: pl.BlockSpec((tm, w), lambda i: (i % npb, 0))
    hrows = lambda hh, w: pl.BlockSpec((hh, tm, w), lambda i: (0, i, 0))
    cn = ["gat", "w", "wwi"]
    cg = ["gqa", "gka", "gcq", "gqbn", "gqbr", "gckv", "gkpe", "bda", "bnr", "brn", "bdr", "wuq", "wuk"]
    ins = [x2d] + [consts[k] for k in cn] + [c64, s64, c32, s32] + [consts[k] for k in cg]
    in_specs = ([rows(d)] + [full(consts[k]) for k in cn] + [prow(wa), prow(wa), prow(hi * di), prow(hi * di)]
                + [full(consts[k]) for k in cg])
    sd = jax.ShapeDtypeStruct
    outs = [
        (sd((n, wa), F32), rows(wa)), (sd((n, wa), F32), rows(wa)), (sd((n, LANES), F32), rows(LANES)),
        (sd((n, lora), F32), rows(lora)), (sd((n, LANES), F32), rows(LANES)),
        (sd((n, d), F32), rows(d)), (sd((n, d), F32), rows(d)),
        (sd((n, wa), BF16), rows(wa)), (sd((n, wa), BF16), rows(wa)), (sd((n, LANES), BF16), rows(LANES)),
        (sd((n, lora + LANES), BF16), rows(lora + LANES)),
        (sd((ha, n, LANES), BF16), hrows(ha, LANES)), (sd((hi, n, LANES), BF16), hrows(hi, LANES)),
        (sd((dm["HIP"], n), F32), pl.BlockSpec((dm["HIP"], tm), lambda i: (0, i))),
        (sd((hb, n, lora + LANES), BF16), hrows(hb, lora + LANES)),
    ]
    res = pl.pallas_call(
        functools.partial(_proj_kernel, dm),
        grid=(n // tm,),
        in_specs=in_specs,
        out_specs=[o[1] for o in outs],
        out_shape=[o[0] for o in outs],
        compiler_params=_cparams(("parallel",)),
        name="proj",
    )(*ins)
    names = ["ka32", "va32", "ki32", "ckv32", "kpe32", "ga", "gb", "ka16", "va16", "ki16", "kcat16",
             "qam", "qim", "wit", "qcat"]
    return dict(zip(names, res))


INT_MIN = -2147483648


def _sortable_key(x):
    bits = pltpu.bitcast(x, I32)
    return bits ^ ((bits >> 31) & 0x7FFFFFFF)


def _dsa_prompt_kernel(cfg, qim_ref, ki_ref, wit_ref, qam_ref, ka_ref, va_ref, o_ref,
                       keys_ref, bias_ref, pmax_ref, m_ref, l_ref, acc_ref):
    tq, topk, hi, ha, da, t_total = cfg["tq"], cfg["topk"], cfg["HI"], cfg["HA"], cfg["DA"], cfg["T"]
    sc1, sc2, sca = cfg["sc1"], cfg["sc2"], cfg["sca"]
    c = tq
    j = pl.program_id(1)
    nch = j + 1
    row_i = lax.broadcasted_iota(I32, (c, tq), 0)
    col_i = lax.broadcasted_iota(I32, (c, tq), 1)
    t_idx = j * tq + col_i

    def chunk(ci):
        return pl.ds(pl.multiple_of(ci * c, c), c)

    def score_body(ci, carry):
        kc = ki_ref[chunk(ci), :]
        acc = jnp.zeros((c, tq), F32)
        for h in range(hi):
            s = _dot_nt(kc, qim_ref[h]) * sc1
            acc = acc + jnp.maximum(s, 0.0) * wit_ref[h:h + 1, :]
        score = acc * sc2
        score = jnp.where(ci * c + row_i <= t_idx, score, NEG_INF)
        keys_ref[chunk(ci), :] = _sortable_key(score)
        return carry

    lax.fori_loop(0, nch, score_body, 0)

    def count(ind):
        def body(ci, acc):
            kk = keys_ref[chunk(ci), :]
            return acc + jnp.sum(ind(kk, ci), axis=0, keepdims=True)
        return lax.fori_loop(0, nch, body, jnp.zeros((1, tq), I32))

    def bit_body(i, prefix):
        cand_u = prefix | lax.shift_left(jnp.int32(1), 31 - i)
        cand_s = cand_u ^ INT_MIN
        cnt = count(lambda kk, ci: jnp.where(kk >= cand_s, 1, 0))
        return jnp.where(cnt >= topk, cand_u, prefix)

    thr = lax.fori_loop(0, 32, bit_body, jnp.zeros((1, tq), I32)) ^ INT_MIN
    n_gt = count(lambda kk, ci: jnp.where(kk > thr, 1, 0))
    n_ge = count(lambda kk, ci: jnp.where(kk >= thr, 1, 0))
    need = topk - n_gt

    pmax_ref[...] = jnp.full((1, tq), t_total, I32)

    @pl.when(jnp.max(n_ge) > topk)
    def _():
        nbits = max(1, (t_total - 1).bit_length())

        def p_body(i, p):
            cand = p | lax.shift_left(jnp.int32(1), nbits - 1 - i)
            f = count(lambda kk, ci: jnp.where(kk == thr, jnp.where(ci * c + row_i < cand, 1, 0), 0))
            return jnp.where(f < need, cand, p)

        pmax_ref[...] = lax.fori_loop(0, nbits, p_body, jnp.zeros((1, tq), I32))

    pmax = pmax_ref[...]

    def bias_body(ci, carry):
        kk = keys_ref[chunk(ci), :]
        s_idx = ci * c + row_i
        sel = jnp.where(kk > thr, 1, jnp.where(kk == thr, jnp.where(s_idx <= pmax, 1, 0), 0))
        ok = jnp.where(s_idx <= t_idx, sel, 0)
        bias_ref[ci] = jnp.where(ok > 0, 0.0, NEG_INF).T
        return carry

    lax.fori_loop(0, nch, bias_body, 0)

    per = LANES // da
    lane = lax.broadcasted_iota(I32, (1, LANES), 1)
    m_ref[...] = jnp.full(m_ref.shape, NEG_INF, F32)
    l_ref[...] = jnp.zeros(l_ref.shape, F32)
    acc_ref[...] = jnp.zeros(acc_ref.shape, F32)

    def att_body(ci, carry):
        bias = bias_ref[ci]
        for h in range(ha):
            g = h // per
            kc = ka_ref[chunk(ci), g * LANES:(g + 1) * LANES]
            vc = va_ref[chunk(ci), g * LANES:(g + 1) * LANES]
            lg = _dot_nt(qam_ref[h], kc) * sca + bias
            m_prev = m_ref[h]
            m_new = jnp.maximum(m_prev, jnp.max(lg, axis=1, keepdims=True))
            m_safe = jnp.where(m_new == NEG_INF, 0.0, m_new)
            p = jnp.exp(lg - jnp.tile(m_safe, (1, c // LANES)))
            alpha = jnp.exp(m_prev - m_safe)
            l_ref[h] = alpha * l_ref[h] + jnp.sum(p, axis=1, keepdims=True)
            acc_ref[h] = alpha * acc_ref[h] + _dot(p.astype(BF16), vc)
            m_ref[h] = m_new
        return carry

    lax.fori_loop(0, nch, att_body, 0)
    for g in range(ha // per):
        out_g = jnp.zeros((tq, LANES), F32)
        for hh in range(per):
            h = g * per + hh
            out_g = jnp.where((lane // da) == hh, acc_ref[h] / l_ref[h], out_g)
        o_ref[:, g * LANES:(g + 1) * LANES] = out_g.astype(o_ref.dtype)


def _dsa_prompt(dm, pr, b, t, tq):
    hi, ha, da, di = dm["HI"], dm["HA"], dm["DA"], dm["DI"]
    wa = ha * da
    topk = min(IDX_TOPK_MAX, t // 4)
    assert t % tq == 0 and topk <= tq and tq % LANES == 0
    nq = t // tq
    cfg = dict(tq=tq, topk=topk, HI=hi, HA=ha, DA=da, T=t,
               sc1=di ** -0.5, sc2=hi ** -0.5, sca=da ** -0.5)
    return pl.pallas_call(
        functools.partial(_dsa_prompt_kernel, cfg),
        grid=(b, nq),
        in_specs=[
            pl.BlockSpec((hi, tq, LANES), lambda bi, j: (0, bi * nq + j, 0)),
            pl.BlockSpec((t, LANES), lambda bi, j: (bi, 0)),
            pl.BlockSpec((dm["HIP"], tq), lambda bi, j: (0, bi * nq + j)),
            pl.BlockSpec((ha, tq, LANES), lambda bi, j: (0, bi * nq + j, 0)),
            pl.BlockSpec((t, wa), lambda bi, j: (bi, 0)),
            pl.BlockSpec((t, wa), lambda bi, j: (bi, 0)),
        ],
        out_specs=pl.BlockSpec((tq, wa), lambda bi, j: (bi * nq + j, 0)),
        out_shape=jax.ShapeDtypeStruct((b * t, wa), BF16),
        scratch_shapes=[pltpu.VMEM((t, tq), I32), pltpu.VMEM((nq, tq, tq), F32), pltpu.VMEM((1, tq), I32),
                        pltpu.VMEM((ha, tq, LANES), F32), pltpu.VMEM((ha, tq, LANES), F32),
                        pltpu.VMEM((ha, tq, LANES), F32)],
        compiler_params=_cparams(("parallel", "arbitrary")),
        name="dsa_prompt",
    )(pr["qim"], pr["ki16"], pr["wit"], pr["qam"], pr["ka16"], pr["va16"])


def _mla_prompt_kernel(cfg, q_ref, k_ref, wuv_ref, o_ref, m_ref, l_ref, a_ref, acc_ref, p_ref):
    tq, ck, lora, scale = cfg["tq"], cfg["ck"], cfg["KV"], cfg["scale"]
    hb, _, kc = q_ref.shape
    rows = hb * tq
    j = pl.program_id(1)
    q = q_ref[...].reshape(rows, kc)
    m_ref[...] = jnp.full(m_ref.shape, NEG_INF, F32)
    l_ref[...] = jnp.zeros(l_ref.shape, F32)
    acc_ref[...] = jnp.zeros(acc_ref.shape, F32)
    t_idx = j * tq + lax.broadcasted_iota(I32, (tq, 1), 0)
    col = lax.broadcasted_iota(I32, (1, ck), 1)
    nch = ((j + 1) * tq + ck - 1) // ck

    def body(ci, carry):
        kk = k_ref[pl.ds(pl.multiple_of(ci * ck, ck), ck), :]
        s = _dot_nt(q, kk) * scale
        causal = ci * ck + col <= t_idx
        for h in range(hb):
            r = slice(h * tq, (h + 1) * tq)
            sh = jnp.where(causal, s[r], NEG_INF)
            m_prev = m_ref[r]
            m_new = jnp.maximum(m_prev, jnp.max(sh, axis=1, keepdims=True))
            p = jnp.exp(sh - jnp.tile(m_new, (1, ck // LANES)))
            alpha = jnp.exp(m_prev - m_new)
            l_ref[r] = alpha * l_ref[r] + jnp.sum(p, axis=1, keepdims=True)
            p_ref[r] = p.astype(BF16)
            a_ref[r] = alpha
            m_ref[r] = m_new
        pv = _dot(p_ref[...], kk[:, :lora])
        for h in range(hb):
            r = slice(h * tq, (h + 1) * tq)
            acc_ref[r] = acc_ref[r] * jnp.tile(a_ref[r], (1, lora // LANES)) + pv[r]
        return carry

    lax.fori_loop(0, nch, body, 0)
    out = None
    for h in range(hb):
        r = slice(h * tq, (h + 1) * tq)
        o = (acc_ref[r] / jnp.tile(l_ref[r], (1, lora // LANES))).astype(BF16)
        out = _dot(o, wuv_ref[h]) if out is None else out + _dot(o, wuv_ref[h])
    o_ref[...] = out.astype(o_ref.dtype)


def _pad_wuv(w_uv):
    hb, lora, bv = w_uv.shape
    w = jnp.zeros((hb, lora, hb * bv), F32)
    for h in range(hb):
        w = w.at[h, :, h * bv:(h + 1) * bv].set(w_uv[h])
    return w.astype(BF16)


def _mla_prompt(dm, pr, wuv_pad, b, t, tq, ck):
    hb, lora, nope, ropeb, bv = dm["HB"], dm["KV"], dm["NOPE"], dm["ROPE"], dm["BV"]
    kc = lora + LANES
    assert t % tq == 0 and t % ck == 0 and ck % LANES == 0
    nq = t // tq
    cfg = dict(tq=tq, ck=ck, KV=lora, scale=(nope + ropeb) ** -0.5)
    rows = hb * tq
    return pl.pallas_call(
        functools.partial(_mla_prompt_kernel, cfg),
        grid=(b, nq),
        in_specs=[
            pl.BlockSpec((hb, tq, kc), lambda bi, j: (0, bi * nq + j, 0)),
            pl.BlockSpec((t, kc), lambda bi, j: (bi, 0)),
            pl.BlockSpec(wuv_pad.shape, lambda bi, j: (0, 0, 0)),
        ],
        out_specs=pl.BlockSpec((tq, hb * bv), lambda bi, j: (bi * nq + j, 0)),
        out_shape=jax.ShapeDtypeStruct((b * t, hb * bv), BF16),
        scratch_shapes=[pltpu.VMEM((rows, LANES), F32), pltpu.VMEM((rows, LANES), F32), pltpu.VMEM((rows, LANES), F32),
                        pltpu.VMEM((rows, lora), F32), pltpu.VMEM((rows, ck), BF16)],
        compiler_params=_cparams(("parallel", "arbitrary")),
        name="mla_prompt",
    )(pr["qcat"], pr["kcat16"], wuv_pad)


def _finish_kernel(x_ref, oa_ref, ob_ref, ga_ref, gb_ref, wpa_ref, wpb_ref, wo_ref, gf_ref, wpq_ref,
                   x1_ref, x1n_ref, q_ref):
    merged = ga_ref[...] * _dot(oa_ref[...], wpa_ref[...]) + gb_ref[...] * _dot(ob_ref[...], wpb_ref[...])
    x1 = x_ref[...] + _dot(merged.astype(BF16), wo_ref[...])
    x1_ref[...] = x1
    ms = jnp.mean(x1 * x1, axis=-1, keepdims=True)
    x1n = (x1 * lax.rsqrt(ms + EPS) * gf_ref[...]).astype(BF16)
    x1n_ref[...] = x1n
    q_ref[...] = _dot(x1n, wpq_ref[...]).astype(q_ref.dtype)


def _finish(x2d, oa, ob, ga, gb, wts, tm):
    n, d = x2d.shape
    assert n % tm == 0
    rows = lambda w: pl.BlockSpec((tm, w), lambda i: (i, 0))
    full = lambda a: pl.BlockSpec(a.shape, lambda i: (0,) * a.ndim)
    wpa, wpb, wo, gf, wpq = wts
    dq = wpq.shape[1]
    return pl.pallas_call(
        _finish_kernel,
        grid=(n // tm,),
        in_specs=[rows(d), rows(oa.shape[1]), rows(ob.shape[1]), rows(d), rows(d),
                  full(wpa), full(wpb), full(wo), full(gf), full(wpq)],
        out_specs=[rows(d), rows(d), rows(dq)],
        out_shape=[jax.ShapeDtypeStruct((n, d), F32), jax.ShapeDtypeStruct((n, d), BF16),
                   jax.ShapeDtypeStruct((n, dq), BF16)],
        compiler_params=_cparams(("parallel",)),
        name="finish",
    )(x2d, oa, ob, ga, gb, wpa, wpb, wo, gf, wpq)


def _oddeven_merge(lo, hi, r):
    step = r * 2
    if step < hi - lo:
        yield from _oddeven_merge(lo, hi, step)
        yield from _oddeven_merge(lo + r, hi, step)
        yield from [(i, i + r) for i in range(lo + r, hi - r, step)]
    else:
        yield (lo, lo + r)


def _oddeven_merge_sort(lo, hi):
    if hi - lo >= 1:
        mid = lo + (hi - lo) // 2
        yield from _oddeven_merge_sort(lo, mid)
        yield from _oddeven_merge_sort(mid + 1, hi)
        yield from _oddeven_merge(lo, hi, 1)


SORT16 = tuple(_oddeven_merge_sort(0, PEER_TOPK - 1))
BITONIC16 = tuple((i, i + k) for k in (8, 4, 2, 1) for i in range(PEER_TOPK) if (i & k) == 0)
SUBLANES = 8


def _compare_exchange(x, net):
    for i, j in net:
        x[i], x[j] = jnp.maximum(x[i], x[j]), jnp.minimum(x[i], x[j])
    return x


def _merge_top16(a, b_rev):
    return _compare_exchange([jnp.maximum(x, y) for x, y in zip(a, b_rev)], BITONIC16)


def _top16_sorted(s):
    x = _compare_exchange([s[i * SUBLANES:(i + 1) * SUBLANES, :] for i in range(PEER_TOPK)], SORT16)
    for shift in (4, 6, 7):
        x = _merge_top16(x, [pltpu.roll(x[PEER_TOPK - 1 - i], shift, 0) for i in range(PEER_TOPK)])
    return x


def _peer_route_kernel(cfg, q_ref, k1_ref, k2_ref, s1_ref, s2_ref, e1_ref, e2_ref):
    hp, dk = cfg["HP"], cfg["DK"]
    tn = q_ref.shape[0]
    sub = lax.broadcasted_iota(I32, (SUBLANES, tn), 0)
    v1p = [jnp.zeros((SUBLANES, tn), F32)] * PEER_TOPK
    v2p = [jnp.zeros((SUBLANES, tn), F32)] * PEER_TOPK
    for h in range(hp):
        qh = q_ref[:, h * dk:(h + 1) * dk]
        s1 = _dot_nt(k1_ref[h], qh)
        s2 = _dot_nt(k2_ref[h], qh)
        s1_ref[0, h] = s1
        s2_ref[0, h] = s2
        t1 = _top16_sorted(s1)
        t2 = _top16_sorted(s2)
        v1p = [jnp.where(sub == h, a, b) for a, b in zip(t1, v1p)]
        v2p = [jnp.where(sub == h, a, b) for a, b in zip(t2, v2p)]
    cur = [v1p[0] + v2p[r2] for r2 in range(PEER_TOPK)]
    for r1 in range(1, PEER_TOPK):
        lim = PEER_TOPK // (r1 + 1)
        b_rev = [v1p[r1] + v2p[PEER_TOPK - 1 - i] if i >= PEER_TOPK - lim else None for i in range(PEER_TOPK)]
        cur = _compare_exchange([a if b is None else jnp.maximum(a, b) for a, b in zip(cur, b_rev)], BITONIC16)
    thr = cur[PEER_TOPK - 1]
    z = jnp.zeros((SUBLANES, tn), F32)
    for r in range(PEER_TOPK):
        z = z + jnp.exp(cur[r] - cur[0])
    rz = 1.0 / z
    cut = []
    for r1 in range(PEER_TOPK):
        c = jnp.full((SUBLANES, tn), jnp.inf, F32)
        for r2 in range(PEER_TOPK // (r1 + 1)):
            c = jnp.where(v1p[r1] + v2p[r2] >= thr, v2p[r2], c)
        cut.append(c)
    for h in range(hp):
        s1 = s1_ref[0, h]
        e1_ref[0, h] = jnp.exp(s1 - v1p[0][h:h + 1, :]) * rz[h:h + 1, :]
        e2_ref[0, h] = jnp.exp(s2_ref[0, h] - v2p[0][h:h + 1, :])
        c1 = jnp.full(s1.shape, jnp.inf, F32)
        for r1 in range(PEER_TOPK):
            c1 = jnp.where(s1 == v1p[r1][h:h + 1, :], cut[r1][h:h + 1, :], c1)
        s1_ref[0, h] = c1


def _peer_route(dm, q, k1p, k2p, tn):
    n = q.shape[0]
    hp, nk, dk = dm["HP"], dm["NK"], dm["DK"]
    assert n % tn == 0 and nk == PEER_TOPK * SUBLANES and hp <= SUBLANES and dk % LANES == 0
    big = pl.BlockSpec((1, hp, nk, tn), lambda i: (i, 0, 0, 0))
    sds = jax.ShapeDtypeStruct((n // tn, hp, nk, tn), F32)
    return pl.pallas_call(
        functools.partial(_peer_route_kernel, dict(HP=hp, DK=dk)),
        grid=(n // tn,),
        in_specs=[pl.BlockSpec((tn, hp * dk), lambda i: (i, 0)),
                  pl.BlockSpec(k1p.shape, lambda i: (0, 0, 0)), pl.BlockSpec(k2p.shape, lambda i: (0, 0, 0))],
        out_specs=[big, big, big, big],
        out_shape=[sds, sds, sds, sds],
        compiler_params=_cparams(("parallel",)),
        name="peer_route",
    )(q, k1p, k2p)


def _peer_dense_kernel(cfg, x_ref, x1_ref, u_ref, vt_ref, c1_ref, s2_ref, e1_ref, e2_ref,
                       y_ref, acc_ref, act_ref, g_ref):
    hp, nk = cfg["HP"], cfg["NK"]
    eb = u_ref.shape[0]
    nts, tl = c1_ref.shape[0], c1_ref.shape[-1]
    e = pl.program_id(1)

    @pl.when(e == 0)
    def _():
        acc_ref[...] = jnp.zeros(acc_ref.shape, F32)

    a = _dot_nt(u_ref[...], x_ref[...])
    act_ref[...] = 0.5 * a * (1.0 + lax.erf(a * (2.0 ** -0.5)))

    for ts in range(nts):
        lanes = slice(ts * tl, (ts + 1) * tl)

        def key_body(k, carry, ts=ts, lanes=lanes):
            i1 = e * (eb // nk) + k
            g = None
            for h in range(hp):
                w = jnp.where(s2_ref[ts, h] >= c1_ref[ts, h, pl.ds(i1, 1), :],
                              e1_ref[ts, h, pl.ds(i1, 1), :] * e2_ref[ts, h], 0.0)
                g = w if g is None else g + w
            rows = pl.ds(pl.multiple_of(k * nk, nk), nk)
            g_ref[rows, lanes] = (g * act_ref[rows, lanes]).astype(BF16)
            return carry

        lax.fori_loop(0, eb // nk, key_body, 0)
    acc_ref[...] += _dot(vt_ref[...], g_ref[...])

    @pl.when(e == pl.num_programs(1) - 1)
    def _():
        y_ref[...] = x1_ref[...] + acc_ref[...].T


def _peer_dense(dm, x1n, x1, u16, vt16, route, tn, eb):
    n, d = x1.shape
    ne = u16.shape[0]
    hp, nk = dm["HP"], dm["NK"]
    c1, s2, e1, e2 = route
    tl = c1.shape[-1]
    assert n % tn == 0 and ne % eb == 0 and eb % nk == 0 and tn % tl == 0
    nts = tn // tl
    big = pl.BlockSpec((nts, hp, nk, tl), lambda i, e: (i, 0, 0, 0))
    return pl.pallas_call(
        functools.partial(_peer_dense_kernel, dict(HP=hp, NK=nk)),
        grid=(n // tn, ne // eb),
        in_specs=[pl.BlockSpec((tn, d), lambda i, e: (i, 0)), pl.BlockSpec((tn, d), lambda i, e: (i, 0)),
                  pl.BlockSpec((eb, d), lambda i, e: (e, 0)), pl.BlockSpec((d, eb), lambda i, e: (0, e)),
                  big, big, big, big],
        out_specs=pl.BlockSpec((tn, d), lambda i, e: (i, 0)),
        out_shape=jax.ShapeDtypeStruct((n, d), F32),
        scratch_shapes=[pltpu.VMEM((d, tn), F32), pltpu.VMEM((eb, tn), F32), pltpu.VMEM((eb, tn), BF16)],
        compiler_params=_cparams(("parallel", "arbitrary")),
        name="peer_dense",
    )(x1n, x1, u16, vt16, c1, s2, e1, e2)


def _tail(dm, x2d, oa, ob, ga, gb, tail_w, peer_w, tm, tn_route, tn_dense, eb):
    x1, x1n, q = _finish(x2d, oa, ob, ga, gb, tail_w, tm)
    k1p, k2p, u16, vt16 = peer_w
    route = _peer_route(dm, q, k1p, k2p, tn_route)
    return _peer_dense(dm, x1n, x1, u16, vt16, route, tn_dense, eb)


def _page_specs(n, shape, pps):
    zeros = (0,) * len(shape)
    return [pl.BlockSpec((1,) + shape, (lambda b, j, pt, k=k: (pt[b, j * pps + k],) + zeros)) for k in range(n)]


def _sample_scores_kernel(cfg, pt_ref, q_ref, w_ref, *refs):
    pps, sc1, sc2 = cfg["pps"], cfg["sc1"], cfg["sc2"]
    pages, out_ref = refs[:pps], refs[pps]
    kt = jnp.concatenate([r[0] for r in pages], axis=1).astype(BF16)
    s = _dot(q_ref[0], kt) * sc1
    w = jnp.tile(w_ref[0], (1, pps))
    out_ref[0] = jnp.sum(jnp.maximum(s, 0.0) * w, axis=0, keepdims=True) * sc2


def _sample_scores(dm, qi, wcol, idx_t, page_table, pps):
    bs, n_pages = page_table.shape
    hi, di = dm["HI"], dm["DI"]
    page = idx_t.shape[-1]
    cfg = dict(pps=pps, sc1=di ** -0.5, sc2=hi ** -0.5)
    grid_spec = pltpu.PrefetchScalarGridSpec(
        num_scalar_prefetch=1, grid=(bs, n_pages // pps),
        in_specs=[pl.BlockSpec((1, hi, di), lambda b, j, pt: (b, 0, 0)),
                  pl.BlockSpec((1, hi, page), lambda b, j, pt: (b, 0, 0))]
                 + _page_specs(pps, (di, page), pps),
        out_specs=pl.BlockSpec((1, 1, pps * page), lambda b, j, pt: (b, 0, j)))
    return pl.pallas_call(
        functools.partial(_sample_scores_kernel, cfg), grid_spec=grid_spec,
        out_shape=jax.ShapeDtypeStruct((bs, 1, n_pages * page), F32),
        compiler_params=_cparams(("parallel", "arbitrary")), name="sample_scores",
    )(page_table, qi, wcol, *([idx_t] * pps))


def _sample_select_kernel(cfg, s_ref, qi_ref, ki_ref, w_ref, bd_ref, bias_ref, bnew_ref, keys_ref):
    slots, topk, past, sc1, sc2 = cfg["slots"], cfg["topk"], cfg["past"], cfg["sc1"], cfg["sc2"]
    bsz, width = s_ref.shape[1], s_ref.shape[2]
    prod = qi_ref[...] * ki_ref[...].astype(BF16).astype(F32)
    hs = _dot_hilo(prod, bd_ref[...])
    s_new = jnp.sum(jnp.maximum(hs * sc1, 0.0) * w_ref[...], axis=1, keepdims=True) * sc2
    k_new = _sortable_key(s_new)
    for kk in range(slots):
        keys_ref[kk] = _sortable_key(s_ref[kk])
    lane = lax.broadcasted_iota(I32, (bsz, width), 1)

    def count(ind, ind_new):
        tot = ind_new
        for kk in range(slots):
            tot = tot + jnp.sum(ind(keys_ref[kk], kk), axis=1, keepdims=True)
        return tot

    def bit_body(i, prefix):
        cand_u = prefix | lax.shift_left(jnp.int32(1), 31 - i)
        cand_s = cand_u ^ INT_MIN
        cnt = count(lambda k, kk: jnp.where(k >= cand_s, 1.0, 0.0), jnp.where(k_new >= cand_s, 1.0, 0.0))
        return jnp.where(cnt >= topk, cand_u, prefix)

    thr = lax.fori_loop(0, 32, bit_body, jnp.zeros((bsz, 1), I32)) ^ INT_MIN
    n_gt = count(lambda k, kk: jnp.where(k > thr, 1.0, 0.0), jnp.where(k_new > thr, 1.0, 0.0))
    need = topk - n_gt
    nbits = max(1, past.bit_length())

    def p_body(i, p):
        cand = p | lax.shift_left(jnp.int32(1), nbits - 1 - i)
        f = count(lambda k, kk: jnp.where(k == thr, jnp.where(lane * slots + kk < cand, 1.0, 0.0), 0.0),
                  jnp.where(k_new == thr, jnp.where(past < cand, 1.0, 0.0), 0.0))
        return jnp.where(f < need, cand, p)

    pmax = lax.fori_loop(0, nbits, p_body, jnp.zeros((bsz, 1), I32))
    for kk in range(slots):
        k = keys_ref[kk]
        sel = jnp.where(k > thr, 1, jnp.where(k == thr, jnp.where(lane * slots + kk <= pmax, 1, 0), 0))
        bias_ref[kk] = jnp.where(sel > 0, 0.0, NEG_INF)
    sel_new = jnp.where(k_new > thr, 1, jnp.where(k_new == thr, jnp.where(past <= pmax, 1, 0), 0))
    bnew_ref[...] = jnp.broadcast_to(jnp.where(sel_new > 0, 0.0, NEG_INF), bnew_ref.shape)


def _sample_select(dm, scores_t, qi_flat, ki_rep, w_pad, bd, past):
    slots, bs, width = scores_t.shape
    hi, di = dm["HI"], dm["DI"]
    topk = min(IDX_TOPK_MAX, (past + 1) // 4)
    cfg = dict(slots=slots, topk=topk, past=past, sc1=di ** -0.5, sc2=hi ** -0.5)
    return pl.pallas_call(
        functools.partial(_sample_select_kernel, cfg),
        out_shape=[jax.ShapeDtypeStruct((slots, bs, width), F32), jax.ShapeDtypeStruct((bs, LANES), F32)],
        scratch_shapes=[pltpu.VMEM((slots, bs, width), I32)],
        compiler_params=pltpu.CompilerParams(vmem_limit_bytes=VMEM_LIMIT), name="sample_select",
    )(scores_t, qi_flat, ki_rep, w_pad, bd)


def _online_softmax_step(m_ref, l_ref, acc_ref, lg, pv_fn):
    m_prev = m_ref[...]
    m_new = jnp.maximum(m_prev, jnp.max(lg, axis=1, keepdims=True))
    m_safe = jnp.where(m_new == NEG_INF, 0.0, m_new)
    p = jnp.exp(lg - m_safe)
    alpha = jnp.exp(m_prev - m_safe)
    l_ref[...] = alpha * l_ref[...] + jnp.sum(p, axis=1, keepdims=True)
    acc_ref[...] = alpha * acc_ref[...] + pv_fn(p)
    m_ref[...] = m_new


def _sample_dsa_kernel(cfg, pt_ref, q_ref, bias_ref, bnew_ref, knew_ref, vnew_ref, *refs):
    pps, sca = cfg["pps"], cfg["sca"]
    kpages, vpages = refs[:pps], refs[pps:2 * pps]
    o_ref, m_ref, l_ref, acc_ref = refs[2 * pps:]
    j = pl.program_id(1)
    heads, wa = q_ref.shape[1], q_ref.shape[2]
    da = wa // heads
    page = kpages[0].shape[-1]

    @pl.when(j == 0)
    def _():
        m_ref[...] = jnp.full(m_ref.shape, NEG_INF, F32)
        l_ref[...] = jnp.zeros(l_ref.shape, F32)
        acc_ref[...] = jnp.zeros(acc_ref.shape, F32)

    q = q_ref[0]
    kt = jnp.concatenate([r[0].reshape(wa, page) for r in kpages], axis=1).astype(BF16)
    vt = jnp.concatenate([r[0].reshape(wa, page) for r in vpages], axis=1).astype(BF16)
    lg = _dot(q, kt) * sca + bias_ref[0]
    _online_softmax_step(m_ref, l_ref, acc_ref, lg, lambda p: _dot_nt(p.astype(BF16), vt))

    @pl.when(j == pl.num_programs(1) - 1)
    def _():
        lgn = jnp.sum(q.astype(F32) * knew_ref[0].astype(F32), axis=1, keepdims=True) * sca + bnew_ref[0][:, 0:1]
        _online_softmax_step(m_ref, l_ref, acc_ref, lgn, lambda p: p * vnew_ref[0].astype(F32))
        o = acc_ref[...] / l_ref[...]
        row = lax.broadcasted_iota(I32, (heads, wa), 0)
        lane = lax.broadcasted_iota(I32, (heads, wa), 1)
        o_ref[0] = jnp.sum(jnp.where(lane // da == row, o, 0.0), axis=0, keepdims=True).astype(o_ref.dtype)


def _sample_dsa(dm, qbd, bias, bnew, knew, vnew, kt, vt, page_table, pps):
    bs, n_pages = page_table.shape
    ha, da = dm["HA"], dm["DA"]
    wa = ha * da
    page = kt.shape[-1]
    cfg = dict(pps=pps, sca=da ** -0.5)
    per_b = lambda shape: pl.BlockSpec((1,) + shape, lambda b, j, pt: (b, 0, 0))
    grid_spec = pltpu.PrefetchScalarGridSpec(
        num_scalar_prefetch=1, grid=(bs, n_pages // pps),
        in_specs=[per_b((ha, wa)), pl.BlockSpec((1, 1, pps * page), lambda b, j, pt: (b, 0, j)),
                  per_b((1, LANES)), per_b((1, wa)), per_b((1, wa))]
                 + _page_specs(pps, (ha, da, page), pps) + _page_specs(pps, (ha, da, page), pps),
        out_specs=per_b((1, wa)),
        scratch_shapes=[pltpu.VMEM((ha, 1), F32), pltpu.VMEM((ha, 1), F32), pltpu.VMEM((ha, wa), F32)])
    return pl.pallas_call(
        functools.partial(_sample_dsa_kernel, cfg), grid_spec=grid_spec,
        out_shape=jax.ShapeDtypeStruct((bs, 1, wa), BF16),
        compiler_params=_cparams(("parallel", "arbitrary")), name="sample_dsa",
    )(page_table, qbd, bias, bnew, knew, vnew, *([kt] * pps), *([vt] * pps))


def _sample_mla_kernel(cfg, pt_ref, qabs_ref, qpe_ref, qfull_ref, knew_ref, wuv_ref, *refs):
    pps, lora, scale = cfg["pps"], cfg["KV"], cfg["scale"]
    cpages, ppages = refs[:pps], refs[pps:2 * pps]
    o_ref, m_ref, l_ref, acc_ref = refs[2 * pps:]
    j = pl.program_id(1)
    heads = qabs_ref.shape[1]

    @pl.when(j == 0)
    def _():
        m_ref[...] = jnp.full(m_ref.shape, NEG_INF, F32)
        l_ref[...] = jnp.zeros(l_ref.shape, F32)
        acc_ref[...] = jnp.zeros(acc_ref.shape, F32)

    c_all = jnp.concatenate([r[0] for r in cpages], axis=0).astype(BF16)
    p_all = jnp.concatenate([r[0] for r in ppages], axis=1).astype(BF16)
    lg = (_dot_nt(qabs_ref[0], c_all) + _dot(qpe_ref[0], p_all)) * scale
    _online_softmax_step(m_ref, l_ref, acc_ref, lg, lambda p: _dot(p.astype(BF16), c_all))

    @pl.when(j == pl.num_programs(1) - 1)
    def _():
        kn = knew_ref[0].astype(F32)
        lgn = jnp.sum(qfull_ref[0].astype(F32) * kn, axis=1, keepdims=True) * scale
        _online_softmax_step(m_ref, l_ref, acc_ref, lgn, lambda p: p * kn[:, :lora])
        o = (acc_ref[...] / l_ref[...]).astype(BF16)
        row = lax.broadcasted_iota(I32, (heads, o_ref.shape[-1]), 0)
        out = jnp.zeros((heads, o_ref.shape[-1]), F32)
        for h in range(heads):
            out = out + jnp.where(row == h, _dot(o, wuv_ref[h]), 0.0)
        o_ref[0] = jnp.sum(out, axis=0, keepdims=True).astype(o_ref.dtype)


def _sample_mla(dm, qabs, qpe, qfull, knew, wuv_pad, cache_ckv, kpe_t, page_table, pps):
    bs, n_pages = page_table.shape
    hb, lora, nope, ropeb, bv = dm["HB"], dm["KV"], dm["NOPE"], dm["ROPE"], dm["BV"]
    page = cache_ckv.shape[1]
    cfg = dict(pps=pps, KV=lora, scale=(nope + ropeb) ** -0.5)
    per_b = lambda shape: pl.BlockSpec((1,) + shape, lambda b, j, pt: (b, 0, 0))
    grid_spec = pltpu.PrefetchScalarGridSpec(
        num_scalar_prefetch=1, grid=(bs, n_pages // pps),
        in_specs=[per_b((hb, lora)), per_b((hb, ropeb)), per_b((hb, lora + LANES)), per_b((1, lora + LANES)),
                  pl.BlockSpec(wuv_pad.shape, lambda b, j, pt: (0, 0, 0))]
                 + _page_specs(pps, (page, lora), pps) + _page_specs(pps, (ropeb, page), pps),
        out_specs=per_b((1, hb * bv)),
        scratch_shapes=[pltpu.VMEM((hb, 1), F32), pltpu.VMEM((hb, 1), F32), pltpu.VMEM((hb, lora), F32)])
    return pl.pallas_call(
        functools.partial(_sample_mla_kernel, cfg), grid_spec=grid_spec,
        out_shape=jax.ShapeDtypeStruct((bs, 1, hb * bv), BF16),
        compiler_params=_cparams(("parallel", "arbitrary")), name="sample_mla",
    )(page_table, qabs, qpe, qfull, knew, wuv_pad, *([cache_ckv] * pps), *([kpe_t] * pps))


def _unmask_heads(xm, d):
    per = LANES // d
    return jnp.stack([xm[h][:, (h % per) * d:(h % per + 1) * d] for h in range(xm.shape[0])], axis=1)


def _sample_attention(dm, ps, cache_k, cache_v, cache_idx_k, cache_ckv, cache_kpe, page_table, wuv_pad):
    bs, n_pages = page_table.shape
    page = cache_k.shape[1]
    past = page * n_pages
    ha, da, hi, di = dm["HA"], dm["DA"], dm["HI"], dm["DI"]
    hb, lora, ropeb = dm["HB"], dm["KV"], dm["ROPE"]
    wa = ha * da
    pps_idx, pps_mla, pps_dsa = math.gcd(n_pages, 64), math.gcd(n_pages, 32), math.gcd(n_pages, 16)
    idx_t = jnp.transpose(cache_idx_k, (0, 2, 1))
    kpe_t = jnp.transpose(cache_kpe, (0, 2, 1))
    k_t = jnp.transpose(cache_k, (0, 2, 3, 1))
    v_t = jnp.transpose(cache_v, (0, 2, 3, 1))

    qi = _unmask_heads(ps["qim"], di)
    w_i = ps["wit"][:hi].T
    wcol = jnp.broadcast_to(w_i[:, :, None], (bs, hi, page))
    scores = _sample_scores(dm, qi, wcol, idx_t, page_table, pps_idx)
    bd = _block_diag_ones(hi * di, di, LANES, 1)
    w_pad = jnp.zeros((bs, LANES), F32).at[:, :hi].set(w_i)
    ki_rep = jnp.tile(ps["ki32"][:, :di], (1, hi))
    bias_t, bnew = _sample_select(dm, scores.reshape(1, bs, past), qi.reshape(bs, hi * di).astype(F32),
                                  ki_rep, w_pad, bd, past)

    per = LANES // da
    qbd = jnp.zeros((bs, ha, wa), BF16)
    for h in range(ha):
        g = h // per
        qbd = qbd.at[:, h, g * LANES:(g + 1) * LANES].set(ps["qam"][h])
    oa = _sample_dsa(dm, qbd, bias_t.reshape(bs, 1, past), bnew.reshape(bs, 1, LANES), ps["ka16"].reshape(bs, 1, wa),
                     ps["va16"].reshape(bs, 1, wa), k_t, v_t, page_table, pps_dsa)

    qfull = jnp.transpose(ps["qcat"], (1, 0, 2))
    qpe = _unmask_heads(ps["qcat"][:, :, lora:], ropeb)
    ob = _sample_mla(dm, qfull[:, :, :lora], qpe, qfull, ps["kcat16"].reshape(bs, 1, lora + LANES), wuv_pad,
                     cache_ckv, kpe_t, page_table, pps_mla)
    return oa.reshape(bs, wa), ob.reshape(bs, -1)


def _model_dims(cache_k, cache_idx_k, cache_ckv, cache_kpe, w_in, g_cq, w_uq, w_uk, w_uv, sub_k1):
    ha, da = cache_k.shape[2], cache_k.shape[3]
    di, lora, ropeb = cache_idx_k.shape[-1], cache_ckv.shape[-1], cache_kpe.shape[-1]
    ql, hb, nope, bv = g_cq.shape[0], w_uq.shape[1], w_uk.shape[1], w_uv.shape[2]
    d = w_in.shape[0]
    rest = w_in.shape[1] - (3 * ha * da + di + ql + lora + ropeb + 2 * d)
    hi = rest // (di + 1)
    assert hi * (di + 1) == rest
    for v in (da, di, ropeb):
        assert LANES % v == 0
    for v in (ha * da, hi * di, hb * nope, hb * ropeb, lora, ql, d):
        assert v % LANES == 0
    return dict(HA=ha, DA=da, HI=hi, DI=di, HB=hb, NOPE=nope, ROPE=ropeb, KV=lora, QL=ql, BV=bv, D=d,
                HP=sub_k1.shape[0], NK=sub_k1.shape[1], DK=2 * sub_k1.shape[2])


def kernel(x_prompt, x_sample, cache_k, cache_v, cache_idx_k, cache_ckv, cache_kpe, page_table, g_attn, w_in, g_qa, g_ka, g_cq, w_uq, g_qb, w_uk, g_ckv, g_kpe, w_uv, w_pa, w_pb, w_o, g_ffn, w_pq, sub_k1, sub_k2, u_tab, v_tab):
    dm = _model_dims(cache_k, cache_idx_k, cache_ckv, cache_kpe, w_in, g_cq, w_uq, w_uk, w_uv, sub_k1)
    dm, consts = _prep_proj_weights(dm, g_attn, w_in, g_qa, g_ka, g_cq, w_uq, g_qb, w_uk, g_ckv, g_kpe)
    b, t, d = x_prompt.shape
    bs, ts, _ = x_sample.shape
    page, n_pages = cache_k.shape[1], page_table.shape[1]
    past = page * n_pages
    ha, da, di, lora, ropeb = dm["HA"], dm["DA"], dm["DI"], dm["KV"], dm["ROPE"]
    hp, nk, dk = dm["HP"], dm["NK"], dm["DK"]

    wuv_pad = _pad_wuv(w_uv)
    tail_w = (w_pa.astype(BF16), w_pb.astype(BF16), w_o.astype(BF16), g_ffn.reshape(1, -1), w_pq.astype(BF16))
    zeros_half = jnp.zeros((hp, nk, dk // 2), F32)
    k1p = jnp.concatenate([sub_k1, zeros_half], axis=-1).astype(BF16)
    k2p = jnp.concatenate([zeros_half, sub_k2], axis=-1).astype(BF16)
    peer_w = (k1p, k2p, u_tab.astype(BF16), v_tab.T.astype(BF16))

    n_p = b * t
    x_p = x_prompt.reshape(n_p, d)
    pp = _project(dm, consts, x_p, jnp.arange(t, dtype=I32), min(256, t))
    oa_p = _dsa_prompt(dm, pp, b, t, min(256, t))
    ob_p = _mla_prompt(dm, pp, wuv_pad, b, t, min(128, t), min(256, t))
    y_p = _tail(dm, x_p, oa_p, ob_p, pp["ga"], pp["gb"], tail_w, peer_w, min(256, n_p), 128, min(512, n_p), 1024)

    n_s = bs * ts
    assert ts == 1
    x_s = x_sample.reshape(n_s, d)
    ps = _project(dm, consts, x_s, jnp.full((n_s,), past, I32), n_s)
    oa_s, ob_s = _sample_attention(dm, ps, cache_k, cache_v, cache_idx_k, cache_ckv, cache_kpe, page_table, wuv_pad)
    y_s = _tail(dm, x_s, oa_s, ob_s, ps["ga"], ps["gb"], tail_w, peer_w, n_s, n_s, n_s, 1024)

    def kv_outs(p, bb, tt):
        return (p["ka32"].reshape(bb, tt, ha, da), p["va32"].reshape(bb, tt, ha, da),
                p["ki32"][:, :di].reshape(bb, tt, di), p["ckv32"].reshape(bb, tt, lora),
                p["kpe32"][:, :ropeb].reshape(bb, tt, ropeb))

    return (y_p.reshape(b, t, d), y_s.reshape(bs, ts, d)) + kv_outs(pp, b, t) + kv_outs(ps, bs, ts)
```

```python
import functools
import math

import numpy as np
import jax
import jax.numpy as jnp
from jax import lax
from jax.experimental import pallas as pl
from jax.experimental.pallas import tpu as pltpu

F32 = jnp.float32
BF16 = jnp.bfloat16
I32 = jnp.int32

ROPE_THETA = 10000.0
EPS = 1e-6
IDX_TOPK_MAX = 256
PEER_TOPK = 16
LANES = 128
NEG_INF = float("-inf")
VMEM_LIMIT = 56 * 1024 * 1024


def _cparams(sem):
    return pltpu.CompilerParams(dimension_semantics=sem, vmem_limit_bytes=VMEM_LIMIT)


def _dot(a, b):
    return jnp.dot(a, b, preferred_element_type=F32)


def _dot_nt(a, b):
    return lax.dot_general(a, b, (((1,), (1,)), ((), ())), preferred_element_type=F32)


def _dot_hilo(a, b):
    hi = a.astype(BF16)
    lo = (a - hi.astype(F32)).astype(BF16)
    return _dot(hi, b) + _dot(lo, b)


def _rope(x, c, s, half):
    w = x.shape[-1]
    fwd = pltpu.roll(x, half, 1)
    bwd = pltpu.roll(x, w - half, 1)
    lane = lax.broadcasted_iota(I32, (1, w), 1)
    first = (lane & (2 * half - 1)) < half
    return x * c + jnp.where(first, bwd, fwd) * s


def _group_masked(x, h, d):
    per = LANES // d
    g = h // per
    grp = x[:, g * LANES:(g + 1) * LANES]
    lane = lax.broadcasted_iota(I32, (1, LANES), 1)
    return jnp.where((lane // d) == (h % per), grp, 0.0)


def _proj_kernel(dm, x_ref, gat_ref, w_ref, wwi_ref, c64_ref, s64_ref, c32_ref, s32_ref,
                 gqa_ref, gka_ref, gcq_ref, gqbn_ref, gqbr_ref, gckv_ref, gkpe_ref,
                 bda_ref, bnr_ref, brn_ref, bdr_ref, wuq_ref, wuk_ref,
                 ka32, va32, ki32, ckv32, kpe32, ga, gb, ka16, va16, ki16, kcat16,
                 qam, qim, wit, qcat):
    off = dm["off"]
    x = x_ref[...]
    ms = jnp.mean(x * x, axis=-1, keepdims=True)
    h = (x * lax.rsqrt(ms + EPS) * gat_ref[...]).astype(BF16)

    def seg(name, n):
        a = off[name]
        return _dot(h, w_ref[:, a:a + n])

    ha, da, hi, di = dm["HA"], dm["DA"], dm["HI"], dm["DI"]
    hb, nope, ropeb, lora = dm["HB"], dm["NOPE"], dm["ROPE"], dm["KV"]
    wa = ha * da
    c64, s64 = c64_ref[...], s64_ref[...]
    c32, s32 = c32_ref[...], s32_ref[...]

    z = seg("qa", wa)
    q = z * lax.rsqrt(_dot_hilo(z * z, bda_ref[...]) * (1.0 / da) + EPS) * gqa_ref[...]
    q = _rope(q, c64, s64, da // 2)
    for hh in range(ha):
        qam[hh] = _group_masked(q, hh, da).astype(BF16)
    z = seg("ka", wa)
    k = z * lax.rsqrt(_dot_hilo(z * z, bda_ref[...]) * (1.0 / da) + EPS) * gka_ref[...]
    k = _rope(k, c64, s64, da // 2)
    ka32[...] = k
    ka16[...] = k.astype(BF16)
    z = seg("va", wa)
    va32[...] = z
    va16[...] = z.astype(BF16)

    z = seg("qi", hi * di)
    q = _rope(z, c32, s32, di // 2)
    for hh in range(hi):
        qim[hh] = _group_masked(q, hh, di).astype(BF16)
    wit[...] = _dot_nt(wwi_ref[...], h)
    z = seg("ki", LANES)
    k = _rope(z, c32[:, :LANES], s32[:, :LANES], di // 2)
    ki32[...] = k
    ki16[...] = k.astype(BF16)

    z = seg("cq", dm["QL"])
    zn = (z * lax.rsqrt(jnp.mean(z * z, axis=-1, keepdims=True) + EPS) * gcq_ref[...]).astype(BF16)
    qb = _dot(zn, wuq_ref[...])
    wn = hb * nope
    qn, qr = qb[:, :wn], qb[:, wn:]
    zzn, zzr = qn * qn, qr * qr
    totn = _dot_hilo(zzn, bda_ref[...]) + _dot_hilo(zzr, brn_ref[...])
    totr = _dot_hilo(zzn, bnr_ref[...]) + _dot_hilo(zzr, bdr_ref[...])
    inv_d = 1.0 / (nope + ropeb)
    qn = qn * lax.rsqrt(totn * inv_d + EPS) * gqbn_ref[...]
    qr = qr * lax.rsqrt(totr * inv_d + EPS) * gqbr_ref[...]
    qpe = _rope(qr, c32, s32, ropeb // 2)
    qabs = _dot(qn.astype(BF16), wuk_ref[...])
    for hh in range(hb):
        qcat[hh, :, 0:lora] = qabs[:, hh * lora:(hh + 1) * lora].astype(BF16)
        qcat[hh, :, lora:lora + LANES] = _group_masked(qpe, hh, ropeb).astype(BF16)

    z = seg("ckv", lora)
    ckv = z * lax.rsqrt(jnp.mean(z * z, axis=-1, keepdims=True) + EPS) * gckv_ref[...]
    ckv32[...] = ckv
    kcat16[:, 0:lora] = ckv.astype(BF16)
    z = seg("kpe", LANES)
    kp = z * lax.rsqrt(jnp.mean(z * z, axis=-1, keepdims=True) + EPS) * gkpe_ref[...]
    kp = _rope(kp, c32[:, :LANES], s32[:, :LANES], ropeb // 2)
    kpe32[...] = kp
    kcat16[:, lora:lora + LANES] = kp.astype(BF16)

    d = x.shape[-1]
    ga[...] = jax.nn.sigmoid(seg("ga", d))
    gb[...] = jax.nn.sigmoid(seg("gb", d))


def _rope_tables(pos, d, reps):
    half = d // 2
    inv = ROPE_THETA ** (-(jnp.arange(half, dtype=F32) * 2.0 / d))
    ang = pos.astype(F32)[:, None] * inv[None, :]
    cos, sin = jnp.cos(ang), jnp.sin(ang)
    c = jnp.concatenate([cos, cos], axis=-1)
    s = jnp.concatenate([-sin, sin], axis=-1)
    return jnp.tile(c, (1, reps)), jnp.tile(s, (1, reps))


def _block_diag_ones(rows, rd, cols, cd):
    r = np.arange(rows)[:, None] // rd
    c = np.arange(cols)[None, :] // cd
    return jnp.asarray((r == c).astype(np.float32), dtype=BF16)


def _prep_proj_weights(dm, g_attn, w_in, g_qa, g_ka, g_cq, w_uq, g_qb, w_uk, g_ckv, g_kpe):
    ha, da, hi, di = dm["HA"], dm["DA"], dm["HI"], dm["DI"]
    hb, nope, ropeb, lora, ql = dm["HB"], dm["NOPE"], dm["ROPE"], dm["KV"], dm["QL"]
    d = w_in.shape[0]
    sizes = (ha * da, ha * da, ha * da, hi * di, hi, di, ql, lora, ropeb, d, d)
    parts, o = [], 0
    for n in sizes:
        parts.append(w_in[:, o:o + n])
        o += n
    w_qa, w_ka, w_va, w_qi, w_wi, w_ki, w_cq, w_ckv, w_kpe, w_ga, w_gb = parts
    segs = [("qa", w_qa), ("ka", w_ka), ("va", w_va), ("qi", w_qi),
            ("ki", jnp.tile(w_ki, (1, LANES // di))), ("cq", w_cq), ("ckv", w_ckv),
            ("kpe", jnp.tile(w_kpe, (1, LANES // ropeb))), ("ga", w_ga), ("gb", w_gb)]
    off, o = {}, 0
    for name, w in segs:
        off[name] = o
        o += w.shape[1]
    w_all = jnp.concatenate([w for _, w in segs], axis=1).astype(BF16)
    hip = -(-hi // 8) * 8
    wwi = jnp.zeros((hip, d), F32).at[:hi].set(w_wi.T).astype(BF16)
    wuq = jnp.concatenate([w_uq[:, :, :nope].reshape(ql, hb * nope),
                           w_uq[:, :, nope:].reshape(ql, hb * ropeb)], axis=1).astype(BF16)
    wuk = jnp.zeros((hb * nope, hb * lora), F32)
    for hh in range(hb):
        wuk = wuk.at[hh * nope:(hh + 1) * nope, hh * lora:(hh + 1) * lora].set(w_uk[hh])
    wuk = wuk.astype(BF16)
    row = lambda v: v.reshape(1, -1).astype(F32)
    consts = dict(
        gat=row(g_attn), w=w_all, wwi=wwi,
        gqa=row(jnp.tile(g_qa, ha)), gka=row(jnp.tile(g_ka, ha)), gcq=row(g_cq),
        gqbn=row(jnp.tile(g_qb[:nope], hb)), gqbr=row(jnp.tile(g_qb[nope:], hb)),
        gckv=row(g_ckv), gkpe=row(jnp.tile(g_kpe, LANES // ropeb)),
        bda=_block_diag_ones(ha * da, da, ha * da, da),
        bnr=_block_diag_ones(hb * nope, nope, hb * ropeb, ropeb),
        brn=_block_diag_ones(hb * ropeb, ropeb, hb * nope, nope),
        bdr=_block_diag_ones(hb * ropeb, ropeb, hb * ropeb, ropeb),
        wuq=wuq, wuk=wuk)
    dm = dict(dm, off=off, HIP=hip, WTOT=o)
    return dm, consts


def _project(dm, consts, x2d, pos, tm):
    n, d = x2d.shape
    p = pos.shape[0]
    assert n % tm == 0 and p % tm == 0
    npb = p // tm
    ha, da, hi, di = dm["HA"], dm["DA"], dm["HI"], dm["DI"]
    hb, ropeb, lora = dm["HB"], dm["ROPE"], dm["KV"]
    wa = ha * da
    c64, s64 = _rope_tables(pos, da, ha)
    c32, s32 = _rope_tables(pos, di, hi)
    full = lambda a: pl.BlockSpec(a.shape, lambda i: (0,) * a.ndim)
    rows = lambda w: pl.BlockSpec((tm, w), lambda i: (i, 0))
    prow = lambda w: pl.BlockSpec((tm, w), lambda i: (i % npb, 0))
    hrows = lambda hh, w: pl.BlockSpec((hh, tm, w), lambda i: (0, i, 0))
    cn = ["gat", "w", "wwi"]
    cg = ["gqa", "gka", "gcq", "gqbn", "gqbr", "gckv", "gkpe", "bda", "bnr", "brn", "bdr", "wuq", "wuk"]
    ins = [x2d] + [consts[k] for k in cn] + [c64, s64, c32, s32] + [consts[k] for k in cg]
    in_specs = ([rows(d)] + [full(consts[k]) for k in cn] + [prow(wa), prow(wa), prow(hi * di), prow(hi * di)]
                + [full(consts[k]) for k in cg])
    sd = jax.ShapeDtypeStruct
    outs = [
        (sd((n, wa), F32), rows(wa)), (sd((n, wa), F32), rows(wa)), (sd((n, LANES), F32), rows(LANES)),
        (sd((n, lora), F32), rows(lora)), (sd((n, LANES), F32), rows(LANES)),
        (sd((n, d), F32), rows(d)), (sd((n, d), F32), rows(d)),
        (sd((n, wa), BF16), rows(wa)), (sd((n, wa), BF16), rows(wa)), (sd((n, LANES), BF16), rows(LANES)),
        (sd((n, lora + LANES), BF16), rows(lora + LANES)),
        (sd((ha, n, LANES), BF16), hrows(ha, LANES)), (sd((hi, n, LANES), BF16), hrows(hi, LANES)),
        (sd((dm["HIP"], n), F32), pl.BlockSpec((dm["HIP"], tm), lambda i: (0, i))),
        (sd((hb, n, lora + LANES), BF16), hrows(hb, lora + LANES)),
    ]
    res = pl.pallas_call(
        functools.partial(_proj_kernel, dm),
        grid=(n // tm,),
        in_specs=in_specs,
        out_specs=[o[1] for o in outs],
        out_shape=[o[0] for o in outs],
        compiler_params=_cparams(("parallel",)),
        name="proj",
    )(*ins)
    names = ["ka32", "va32", "ki32", "ckv32", "kpe32", "ga", "gb", "ka16", "va16", "ki16", "kcat16",
             "qam", "qim", "wit", "qcat"]
    return dict(zip(names, res))


INT_MIN = -2147483648


def _sortable_key(x):
    bits = pltpu.bitcast(x, I32)
    return bits ^ ((bits >> 31) & 0x7FFFFFFF)


def _dsa_prompt_kernel(cfg, qim_ref, ki_ref, wit_ref, qam_ref, ka_ref, va_ref, o_ref,
                       keys_ref, bias_ref, pmax_ref, m_ref, l_ref, acc_ref):
    tq, topk, hi, ha, da, t_total = cfg["tq"], cfg["topk"], cfg["HI"], cfg["HA"], cfg["DA"], cfg["T"]
    sc1, sc2, sca = cfg["sc1"], cfg["sc2"], cfg["sca"]
    c = tq
    j = pl.program_id(1)
    nch = j + 1
    row_i = lax.broadcasted_iota(I32, (c, tq), 0)
    col_i = lax.broadcasted_iota(I32, (c, tq), 1)
    t_idx = j * tq + col_i

    def chunk(ci):
        return pl.ds(pl.multiple_of(ci * c, c), c)

    def score_body(ci, carry):
        kc = ki_ref[chunk(ci), :]
        acc = jnp.zeros((c, tq), F32)
        for h in range(hi):
            s = _dot_nt(kc, qim_ref[h]) * sc1
            acc = acc + jnp.maximum(s, 0.0) * wit_ref[h:h + 1, :]
        score = acc * sc2
        score = jnp.where(ci * c + row_i <= t_idx, score, NEG_INF)
        keys_ref[chunk(ci), :] = _sortable_key(score)
        return carry

    lax.fori_loop(0, nch, score_body, 0)

    def count(ind):
        def body(ci, acc):
            kk = keys_ref[chunk(ci), :]
            return acc + jnp.sum(ind(kk, ci), axis=0, keepdims=True)
        return lax.fori_loop(0, nch, body, jnp.zeros((1, tq), I32))

    def bit_body(i, prefix):
        cand_u = prefix | lax.shift_left(jnp.int32(1), 31 - i)
        cand_s = cand_u ^ INT_MIN
        cnt = count(lambda kk, ci: jnp.where(kk >= cand_s, 1, 0))
        return jnp.where(cnt >= topk, cand_u, prefix)

    thr = lax.fori_loop(0, 32, bit_body, jnp.zeros((1, tq), I32)) ^ INT_MIN
    n_gt = count(lambda kk, ci: jnp.where(kk > thr, 1, 0))
    n_ge = count(lambda kk, ci: jnp.where(kk >= thr, 1, 0))
    need = topk - n_gt

    pmax_ref[...] = jnp.full((1, tq), t_total, I32)

    @pl.when(jnp.max(n_ge) > topk)
    def _():
        nbits = max(1, (t_total - 1).bit_length())

        def p_body(i, p):
            cand = p | lax.shift_left(jnp.int32(1), nbits - 1 - i)
            f = count(lambda kk, ci: jnp.where(kk == thr, jnp.where(ci * c + row_i < cand, 1, 0), 0))
            return jnp.where(f < need, cand, p)

        pmax_ref[...] = lax.fori_loop(0, nbits, p_body, jnp.zeros((1, tq), I32))

    pmax = pmax_ref[...]

    def bias_body(ci, carry):
        kk = keys_ref[chunk(ci), :]
        s_idx = ci * c + row_i
        sel = jnp.where(kk > thr, 1, jnp.where(kk == thr, jnp.where(s_idx <= pmax, 1, 0), 0))
        ok = jnp.where(s_idx <= t_idx, sel, 0)
        bias_ref[ci] = jnp.where(ok > 0, 0.0, NEG_INF).T
        return carry

    lax.fori_loop(0, nch, bias_body, 0)

    per = LANES // da
    lane = lax.broadcasted_iota(I32, (1, LANES), 1)
    m_ref[...] = jnp.full(m_ref.shape, NEG_INF, F32)
    l_ref[...] = jnp.zeros(l_ref.shape, F32)
    acc_ref[...] = jnp.zeros(acc_ref.shape, F32)

    def att_body(ci, carry):
        bias = bias_ref[ci]
        for h in range(ha):
            g = h // per
            kc = ka_ref[chunk(ci), g * LANES:(g + 1) * LANES]
            vc = va_ref[chunk(ci), g * LANES:(g + 1) * LANES]
            lg = _dot_nt(qam_ref[h], kc) * sca + bias
            m_prev = m_ref[h]
            m_new = jnp.maximum(m_prev, jnp.max(lg, axis=1, keepdims=True))
            m_safe = jnp.where(m_new == NEG_INF, 0.0, m_new)
            p = jnp.exp(lg - jnp.tile(m_safe, (1, c // LANES)))
            alpha = jnp.exp(m_prev - m_safe)
            l_ref[h] = alpha * l_ref[h] + jnp.sum(p, axis=1, keepdims=True)
            acc_ref[h] = alpha * acc_ref[h] + _dot(p.astype(BF16), vc)
            m_ref[h] = m_new
        return carry

    lax.fori_loop(0, nch, att_body, 0)
    for g in range(ha // per):
        out_g = jnp.zeros((tq, LANES), F32)
        for hh in range(per):
            h = g * per + hh
            out_g = jnp.where((lane // da) == hh, acc_ref[h] / l_ref[h], out_g)
        o_ref[:, g * LANES:(g + 1) * LANES] = out_g.astype(o_ref.dtype)


def _dsa_prompt(dm, pr, b, t, tq):
    hi, ha, da, di = dm["HI"], dm["HA"], dm["DA"], dm["DI"]
    wa = ha * da
    topk = min(IDX_TOPK_MAX, t // 4)
    assert t % tq == 0 and topk <= tq and tq % LANES == 0
    nq = t // tq
    cfg = dict(tq=tq, topk=topk, HI=hi, HA=ha, DA=da, T=t,
               sc1=di ** -0.5, sc2=hi ** -0.5, sca=da ** -0.5)
    return pl.pallas_call(
        functools.partial(_dsa_prompt_kernel, cfg),
        grid=(b, nq),
        in_specs=[
            pl.BlockSpec((hi, tq, LANES), lambda bi, j: (0, bi * nq + j, 0)),
            pl.BlockSpec((t, LANES), lambda bi, j: (bi, 0)),
            pl.BlockSpec((dm["HIP"], tq), lambda bi, j: (0, bi * nq + j)),
            pl.BlockSpec((ha, tq, LANES), lambda bi, j: (0, bi * nq + j, 0)),
            pl.BlockSpec((t, wa), lambda bi, j: (bi, 0)),
            pl.BlockSpec((t, wa), lambda bi, j: (bi, 0)),
        ],
        out_specs=pl.BlockSpec((tq, wa), lambda bi, j: (bi * nq + j, 0)),
        out_shape=jax.ShapeDtypeStruct((b * t, wa), BF16),
        scratch_shapes=[pltpu.VMEM((t, tq), I32), pltpu.VMEM((nq, tq, tq), F32), pltpu.VMEM((1, tq), I32),
                        pltpu.VMEM((ha, tq, LANES), F32), pltpu.VMEM((ha, tq, LANES), F32),
                        pltpu.VMEM((ha, tq, LANES), F32)],
        compiler_params=_cparams(("parallel", "arbitrary")),
        name="dsa_prompt",
    )(pr["qim"], pr["ki16"], pr["wit"], pr["qam"], pr["ka16"], pr["va16"])


def _mla_prompt_kernel(cfg, q_ref, k_ref, wuv_ref, o_ref, m_ref, l_ref, a_ref, acc_ref, p_ref):
    tq, ck, lora, scale = cfg["tq"], cfg["ck"], cfg["KV"], cfg["scale"]
    hb, _, kc = q_ref.shape
    rows = hb * tq
    j = pl.program_id(1)
    q = q_ref[...].reshape(rows, kc)
    m_ref[...] = jnp.full(m_ref.shape, NEG_INF, F32)
    l_ref[...] = jnp.zeros(l_ref.shape, F32)
    acc_ref[...] = jnp.zeros(acc_ref.shape, F32)
    t_idx = j * tq + lax.broadcasted_iota(I32, (tq, 1), 0)
    col = lax.broadcasted_iota(I32, (1, ck), 1)
    nch = ((j + 1) * tq + ck - 1) // ck

    def body(ci, carry):
        kk = k_ref[pl.ds(pl.multiple_of(ci * ck, ck), ck), :]
        s = _dot_nt(q, kk) * scale
        causal = ci * ck + col <= t_idx
        for h in range(hb):
            r = slice(h * tq, (h + 1) * tq)
            sh = jnp.where(causal, s[r], NEG_INF)
            m_prev = m_ref[r]
            m_new = jnp.maximum(m_prev, jnp.max(sh, axis=1, keepdims=True))
            p = jnp.exp(sh - jnp.tile(m_new, (1, ck // LANES)))
            alpha = jnp.exp(m_prev - m_new)
            l_ref[r] = alpha * l_ref[r] + jnp.sum(p, axis=1, keepdims=True)
            p_ref[r] = p.astype(BF16)
            a_ref[r] = alpha
            m_ref[r] = m_new
        pv = _dot(p_ref[...], kk[:, :lora])
        for h in range(hb):
            r = slice(h * tq, (h + 1) * tq)
            acc_ref[r] = acc_ref[r] * jnp.tile(a_ref[r], (1, lora // LANES)) + pv[r]
        return carry

    lax.fori_loop(0, nch, body, 0)
    out = None
    for h in range(hb):
        r = slice(h * tq, (h + 1) * tq)
        o = (acc_ref[r] / jnp.tile(l_ref[r], (1, lora // LANES))).astype(BF16)
        out = _dot(o, wuv_ref[h]) if out is None else out + _dot(o, wuv_ref[h])
    o_ref[...] = out.astype(o_ref.dtype)


def _pad_wuv(w_uv):
    hb, lora, bv = w_uv.shape
    w = jnp.zeros((hb, lora, hb * bv), F32)
    for h in range(hb):
        w = w.at[h, :, h * bv:(h + 1) * bv].set(w_uv[h])
    return w.astype(BF16)


def _mla_prompt(dm, pr, wuv_pad, b, t, tq, ck):
    hb, lora, nope, ropeb, bv = dm["HB"], dm["KV"], dm["NOPE"], dm["ROPE"], dm["BV"]
    kc = lora + LANES
    assert t % tq == 0 and t % ck == 0 and ck % LANES == 0
    nq = t // tq
    cfg = dict(tq=tq, ck=ck, KV=lora, scale=(nope + ropeb) ** -0.5)
    rows = hb * tq
    return pl.pallas_call(
        functools.partial(_mla_prompt_kernel, cfg),
        grid=(b, nq),
        in_specs=[
            pl.BlockSpec((hb, tq, kc), lambda bi, j: (0, bi * nq + j, 0)),
            pl.BlockSpec((t, kc), lambda bi, j: (bi, 0)),
            pl.BlockSpec(wuv_pad.shape, lambda bi, j: (0, 0, 0)),
        ],
        out_specs=pl.BlockSpec((tq, hb * bv), lambda bi, j: (bi * nq + j, 0)),
        out_shape=jax.ShapeDtypeStruct((b * t, hb * bv), BF16),
        scratch_shapes=[pltpu.VMEM((rows, LANES), F32), pltpu.VMEM((rows, LANES), F32), pltpu.VMEM((rows, LANES), F32),
                        pltpu.VMEM((rows, lora), F32), pltpu.VMEM((rows, ck), BF16)],
        compiler_params=_cparams(("parallel", "arbitrary")),
        name="mla_prompt",
    )(pr["qcat"], pr["kcat16"], wuv_pad)


def _finish_kernel(x_ref, oa_ref, ob_ref, ga_ref, gb_ref, wpa_ref, wpb_ref, wo_ref, gf_ref, wpq_ref,
                   x1_ref, x1n_ref, q_ref):
    merged = ga_ref[...] * _dot(oa_ref[...], wpa_ref[...]) + gb_ref[...] * _dot(ob_ref[...], wpb_ref[...])
    x1 = x_ref[...] + _dot(merged.astype(BF16), wo_ref[...])
    x1_ref[...] = x1
    ms = jnp.mean(x1 * x1, axis=-1, keepdims=True)
    x1n = (x1 * lax.rsqrt(ms + EPS) * gf_ref[...]).astype(BF16)
    x1n_ref[...] = x1n
    q_ref[...] = _dot(x1n, wpq_ref[...]).astype(q_ref.dtype)


def _finish(x2d, oa, ob, ga, gb, wts, tm):
    n, d = x2d.shape
    assert n % tm == 0
    rows = lambda w: pl.BlockSpec((tm, w), lambda i: (i, 0))
    full = lambda a: pl.BlockSpec(a.shape, lambda i: (0,) * a.ndim)
    wpa, wpb, wo, gf, wpq = wts
    dq = wpq.shape[1]
    return pl.pallas_call(
        _finish_kernel,
        grid=(n // tm,),
        in_specs=[rows(d), rows(oa.shape[1]), rows(ob.shape[1]), rows(d), rows(d),
                  full(wpa), full(wpb), full(wo), full(gf), full(wpq)],
        out_specs=[rows(d), rows(d), rows(dq)],
        out_shape=[jax.ShapeDtypeStruct((n, d), F32), jax.ShapeDtypeStruct((n, d), BF16),
                   jax.ShapeDtypeStruct((n, dq), BF16)],
        compiler_params=_cparams(("parallel",)),
        name="finish",
    )(x2d, oa, ob, ga, gb, wpa, wpb, wo, gf, wpq)


def _oddeven_merge(lo, hi, r):
    step = r * 2
    if step < hi - lo:
        yield from _oddeven_merge(lo, hi, step)
        yield from _oddeven_merge(lo + r, hi, step)
        yield from [(i, i + r) for i in range(lo + r, hi - r, step)]
    else:
        yield (lo, lo + r)


def _oddeven_merge_sort(lo, hi):
    if hi - lo >= 1:
        mid = lo + (hi - lo) // 2
        yield from _oddeven_merge_sort(lo, mid)
        yield from _oddeven_merge_sort(mid + 1, hi)
        yield from _oddeven_merge(lo, hi, 1)


SORT16 = tuple(_oddeven_merge_sort(0, PEER_TOPK - 1))
BITONIC16 = tuple((i, i + k) for k in (8, 4, 2, 1) for i in range(PEER_TOPK) if (i & k) == 0)
SUBLANES = 8


def _compare_exchange(x, net):
    for i, j in net:
        x[i], x[j] = jnp.maximum(x[i], x[j]), jnp.minimum(x[i], x[j])
    return x


def _merge_top16(a, b_rev):
    return _compare_exchange([jnp.maximum(x, y) for x, y in zip(a, b_rev)], BITONIC16)


def _top16_sorted(s):
    x = _compare_exchange([s[i * SUBLANES:(i + 1) * SUBLANES, :] for i in range(PEER_TOPK)], SORT16)
    for shift in (4, 6, 7):
        x = _merge_top16(x, [pltpu.roll(x[PEER_TOPK - 1 - i], shift, 0) for i in range(PEER_TOPK)])
    return x


def _peer_route_kernel(cfg, q_ref, k1_ref, k2_ref, s1_ref, s2_ref, e1_ref, e2_ref):
    hp, dk = cfg["HP"], cfg["DK"]
    tn = q_ref.shape[0]
    sub = lax.broadcasted_iota(I32, (SUBLANES, tn), 0)
    v1p = [jnp.zeros((SUBLANES, tn), F32)] * PEER_TOPK
    v2p = [jnp.zeros((SUBLANES, tn), F32)] * PEER_TOPK
    for h in range(hp):
        qh = q_ref[:, h * dk:(h + 1) * dk]
        s1 = _dot_nt(k1_ref[h], qh)
        s2 = _dot_nt(k2_ref[h], qh)
        s1_ref[0, h] = s1
        s2_ref[0, h] = s2
        t1 = _top16_sorted(s1)
        t2 = _top16_sorted(s2)
        v1p = [jnp.where(sub == h, a, b) for a, b in zip(t1, v1p)]
        v2p = [jnp.where(sub == h, a, b) for a, b in zip(t2, v2p)]
    cur = [v1p[0] + v2p[r2] for r2 in range(PEER_TOPK)]
    for r1 in range(1, PEER_TOPK):
        lim = PEER_TOPK // (r1 + 1)
        b_rev = [v1p[r1] + v2p[PEER_TOPK - 1 - i] if i >= PEER_TOPK - lim else None for i in range(PEER_TOPK)]
        cur = _compare_exchange([a if b is None else jnp.maximum(a, b) for a, b in zip(cur, b_rev)], BITONIC16)
    thr = cur[PEER_TOPK - 1]
    z = jnp.zeros((SUBLANES, tn), F32)
    for r in range(PEER_TOPK):
        z = z + jnp.exp(cur[r] - cur[0])
    rz = 1.0 / z
    cut = []
    for r1 in range(PEER_TOPK):
        c = jnp.full((SUBLANES, tn), jnp.inf, F32)
        for r2 in range(PEER_TOPK // (r1 + 1)):
            c = jnp.where(v1p[r1] + v2p[r2] >= thr, v2p[r2], c)
        cut.append(c)
    for h in range(hp):
        s1 = s1_ref[0, h]
        e1_ref[0, h] = jnp.exp(s1 - v1p[0][h:h + 1, :]) * rz[h:h + 1, :]
        e2_ref[0, h] = jnp.exp(s2_ref[0, h] - v2p[0][h:h + 1, :])
        c1 = jnp.full(s1.shape, jnp.inf, F32)
        for r1 in range(PEER_TOPK):
            c1 = jnp.where(s1 == v1p[r1][h:h + 1, :], cut[r1][h:h + 1, :], c1)
        s1_ref[0, h] = c1


def _peer_route(dm, q, k1p, k2p, tn):
    n = q.shape[0]
    hp, nk, dk = dm["HP"], dm["NK"], dm["DK"]
    assert n % tn == 0 and nk == PEER_TOPK * SUBLANES and hp <= SUBLANES and dk % LANES == 0
    big = pl.BlockSpec((1, hp, nk, tn), lambda i: (i, 0, 0, 0))
    sds = jax.ShapeDtypeStruct((n // tn, hp, nk, tn), F32)
    return pl.pallas_call(
        functools.partial(_peer_route_kernel, dict(HP=hp, DK=dk)),
        grid=(n // tn,),
        in_specs=[pl.BlockSpec((tn, hp * dk), lambda i: (i, 0)),
                  pl.BlockSpec(k1p.shape, lambda i: (0, 0, 0)), pl.BlockSpec(k2p.shape, lambda i: (0, 0, 0))],
        out_specs=[big, big, big, big],
        out_shape=[sds, sds, sds, sds],
        compiler_params=_cparams(("parallel",)),
        name="peer_route",
    )(q, k1p, k2p)


def _peer_dense_kernel(cfg, x_ref, x1_ref, u_ref, vt_ref, c1_ref, s2_ref, e1_ref, e2_ref,
                       y_ref, acc_ref, act_ref, g_ref):
    hp, nk = cfg["HP"], cfg["NK"]
    eb = u_ref.shape[0]
    nts, tl = c1_ref.shape[0], c1_ref.shape[-1]
    e = pl.program_id(1)

    @pl.when(e == 0)
    def _():
        acc_ref[...] = jnp.zeros(acc_ref.shape, F32)

    a = _dot_nt(u_ref[...], x_ref[...])
    act_ref[...] = 0.5 * a * (1.0 + lax.erf(a * (2.0 ** -0.5)))

    for ts in range(nts):
        lanes = slice(ts * tl, (ts + 1) * tl)

        def key_body(k, carry, ts=ts, lanes=lanes):
            i1 = e * (eb // nk) + k
            g = None
            for h in range(hp):
                w = jnp.where(s2_ref[ts, h] >= c1_ref[ts, h, pl.ds(i1, 1), :],
                              e1_ref[ts, h, pl.ds(i1, 1), :] * e2_ref[ts, h], 0.0)
                g = w if g is None else g + w
            rows = pl.ds(pl.multiple_of(k * nk, nk), nk)
            g_ref[rows, lanes] = (g * act_ref[rows, lanes]).astype(BF16)
            return carry

        lax.fori_loop(0, eb // nk, key_body, 0)
    acc_ref[...] += _dot(vt_ref[...], g_ref[...])

    @pl.when(e == pl.num_programs(1) - 1)
    def _():
        y_ref[...] = x1_ref[...] + acc_ref[...].T


def _peer_dense(dm, x1n, x1, u16, vt16, route, tn, eb):
    n, d = x1.shape
    ne = u16.shape[0]
    hp, nk = dm["HP"], dm["NK"]
    c1, s2, e1, e2 = route
    tl = c1.shape[-1]
    assert n % tn == 0 and ne % eb == 0 and eb % nk == 0 and tn % tl == 0
    nts = tn // tl
    big = pl.BlockSpec((nts, hp, nk, tl), lambda i, e: (i, 0, 0, 0))
    return pl.pallas_call(
        functools.partial(_peer_dense_kernel, dict(HP=hp, NK=nk)),
        grid=(n // tn, ne // eb),
        in_specs=[pl.BlockSpec((tn, d), lambda i, e: (i, 0)), pl.BlockSpec((tn, d), lambda i, e: (i, 0)),
                  pl.BlockSpec((eb, d), lambda i, e: (e, 0)), pl.BlockSpec((d, eb), lambda i, e: (0, e)),
                  big, big, big, big],
        out_specs=pl.BlockSpec((tn, d), lambda i, e: (i, 0)),
        out_shape=jax.ShapeDtypeStruct((n, d), F32),
        scratch_shapes=[pltpu.VMEM((d, tn), F32), pltpu.VMEM((eb, tn), F32), pltpu.VMEM((eb, tn), BF16)],
        compiler_params=_cparams(("parallel", "arbitrary")),
        name="peer_dense",
    )(x1n, x1, u16, vt16, c1, s2, e1, e2)


def _tail(dm, x2d, oa, ob, ga, gb, tail_w, peer_w, tm, tn_route, tn_dense, eb):
    x1, x1n, q = _finish(x2d, oa, ob, ga, gb, tail_w, tm)
    k1p, k2p, u16, vt16 = peer_w
    route = _peer_route(dm, q, k1p, k2p, tn_route)
    return _peer_dense(dm, x1n, x1, u16, vt16, route, tn_dense, eb)


def _page_specs(n, shape, pps):
    zeros = (0,) * len(shape)
    return [pl.BlockSpec((1,) + shape, (lambda b, j, pt, k=k: (pt[b, j * pps + k],) + zeros)) for k in range(n)]


def _sample_scores_kernel(cfg, pt_ref, q_ref, w_ref, *refs):
    pps, sc1, sc2 = cfg["pps"], cfg["sc1"], cfg["sc2"]
    pages, out_ref = refs[:pps], refs[pps]
    kt = jnp.concatenate([r[0] for r in pages], axis=1).astype(BF16)
    s = _dot(q_ref[0], kt) * sc1
    w = jnp.tile(w_ref[0], (1, pps))
    out_ref[0] = jnp.sum(jnp.maximum(s, 0.0) * w, axis=0, keepdims=True) * sc2


def _sample_scores(dm, qi, wcol, idx_t, page_table, pps):
    bs, n_pages = page_table.shape
    hi, di = dm["HI"], dm["DI"]
    page = idx_t.shape[-1]
    cfg = dict(pps=pps, sc1=di ** -0.5, sc2=hi ** -0.5)
    grid_spec = pltpu.PrefetchScalarGridSpec(
        num_scalar_prefetch=1, grid=(bs, n_pages // pps),
        in_specs=[pl.BlockSpec((1, hi, di), lambda b, j, pt: (b, 0, 0)),
                  pl.BlockSpec((1, hi, page), lambda b, j, pt: (b, 0, 0))]
                 + _page_specs(pps, (di, page), pps),
        out_specs=pl.BlockSpec((1, 1, pps * page), lambda b, j, pt: (b, 0, j)))
    return pl.pallas_call(
        functools.partial(_sample_scores_kernel, cfg), grid_spec=grid_spec,
        out_shape=jax.ShapeDtypeStruct((bs, 1, n_pages * page), F32),
        compiler_params=_cparams(("parallel", "arbitrary")), name="sample_scores",
    )(page_table, qi, wcol, *([idx_t] * pps))


def _sample_select_kernel(cfg, s_ref, qi_ref, ki_ref, w_ref, bd_ref, bias_ref, bnew_ref, keys_ref):
    slots, topk, past, sc1, sc2 = cfg["slots"], cfg["topk"], cfg["past"], cfg["sc1"], cfg["sc2"]
    bsz, width = s_ref.shape[1], s_ref.shape[2]
    prod = qi_ref[...] * ki_ref[...].astype(BF16).astype(F32)
    hs = _dot_hilo(prod, bd_ref[...])
    s_new = jnp.sum(jnp.maximum(hs * sc1, 0.0) * w_ref[...], axis=1, keepdims=True) * sc2
    k_new = _sortable_key(s_new)
    for kk in range(slots):
        keys_ref[kk] = _sortable_key(s_ref[kk])
    lane = lax.broadcasted_iota(I32, (bsz, width), 1)

    def count(ind, ind_new):
        tot = ind_new
        for kk in range(slots):
            tot = tot + jnp.sum(ind(keys_ref[kk], kk), axis=1, keepdims=True)
        return tot

    def bit_body(i, prefix):
        cand_u = prefix | lax.shift_left(jnp.int32(1), 31 - i)
        cand_s = cand_u ^ INT_MIN
        cnt = count(lambda k, kk: jnp.where(k >= cand_s, 1.0, 0.0), jnp.where(k_new >= cand_s, 1.0, 0.0))
        return jnp.where(cnt >= topk, cand_u, prefix)

    thr = lax.fori_loop(0, 32, bit_body, jnp.zeros((bsz, 1), I32)) ^ INT_MIN
    n_gt = count(lambda k, kk: jnp.where(k > thr, 1.0, 0.0), jnp.where(k_new > thr, 1.0, 0.0))
    need = topk - n_gt
    nbits = max(1, past.bit_length())

    def p_body(i, p):
        cand = p | lax.shift_left(jnp.int32(1), nbits - 1 - i)
        f = count(lambda k, kk: jnp.where(k == thr, jnp.where(lane * slots + kk < cand, 1.0, 0.0), 0.0),
                  jnp.where(k_new == thr, jnp.where(past < cand, 1.0, 0.0), 0.0))
        return jnp.where(f < need, cand, p)

    pmax = lax.fori_loop(0, nbits, p_body, jnp.zeros((bsz, 1), I32))
    for kk in range(slots):
        k = keys_ref[kk]
        sel = jnp.where(k > thr, 1, jnp.where(k == thr, jnp.where(lane * slots + kk <= pmax, 1, 0), 0))
        bias_ref[kk] = jnp.where(sel > 0, 0.0, NEG_INF)
    sel_new = jnp.where(k_new > thr, 1, jnp.where(k_new == thr, jnp.where(past <= pmax, 1, 0), 0))
    bnew_ref[...] = jnp.broadcast_to(jnp.where(sel_new > 0, 0.0, NEG_INF), bnew_ref.shape)


def _sample_select(dm, scores_t, qi_flat, ki_rep, w_pad, bd, past):
    slots, bs, width = scores_t.shape
    hi, di = dm["HI"], dm["DI"]
    topk = min(IDX_TOPK_MAX, (past + 1) // 4)
    cfg = dict(slots=slots, topk=topk, past=past, sc1=di ** -0.5, sc2=hi ** -0.5)
    return pl.pallas_call(
        functools.partial(_sample_select_kernel, cfg),
        out_shape=[jax.ShapeDtypeStruct((slots, bs, width), F32), jax.ShapeDtypeStruct((bs, LANES), F32)],
        scratch_shapes=[pltpu.VMEM((slots, bs, width), I32)],
        compiler_params=pltpu.CompilerParams(vmem_limit_bytes=VMEM_LIMIT), name="sample_select",
    )(scores_t, qi_flat, ki_rep, w_pad, bd)


def _online_softmax_step(m_ref, l_ref, acc_ref, lg, pv_fn):
    m_prev = m_ref[...]
    m_new = jnp.maximum(m_prev, jnp.max(lg, axis=1, keepdims=True))
    m_safe = jnp.where(m_new == NEG_INF, 0.0, m_new)
    p = jnp.exp(lg - m_safe)
    alpha = jnp.exp(m_prev - m_safe)
    l_ref[...] = alpha * l_ref[...] + jnp.sum(p, axis=1, keepdims=True)
    acc_ref[...] = alpha * acc_ref[...] + pv_fn(p)
    m_ref[...] = m_new


def _sample_dsa_kernel(cfg, pt_ref, q_ref, bias_ref, bnew_ref, knew_ref, vnew_ref, *refs):
    pps, sca = cfg["pps"], cfg["sca"]
    kpages, vpages = refs[:pps], refs[pps:2 * pps]
    o_ref, m_ref, l_ref, acc_ref = refs[2 * pps:]
    j = pl.program_id(1)
    heads, wa = q_ref.shape[1], q_ref.shape[2]
    da = wa // heads
    page = kpages[0].shape[-1]

    @pl.when(j == 0)
    def _():
        m_ref[...] = jnp.full(m_ref.shape, NEG_INF, F32)
        l_ref[...] = jnp.zeros(l_ref.shape, F32)
        acc_ref[...] = jnp.zeros(acc_ref.shape, F32)

    q = q_ref[0]
    kt = jnp.concatenate([r[0].reshape(wa, page) for r in kpages], axis=1).astype(BF16)
    vt = jnp.concatenate([r[0].reshape(wa, page) for r in vpages], axis=1).astype(BF16)
    lg = _dot(q, kt) * sca + bias_ref[0]
    _online_softmax_step(m_ref, l_ref, acc_ref, lg, lambda p: _dot_nt(p.astype(BF16), vt))

    @pl.when(j == pl.num_programs(1) - 1)
    def _():
        lgn = jnp.sum(q.astype(F32) * knew_ref[0].astype(F32), axis=1, keepdims=True) * sca + bnew_ref[0][:, 0:1]
        _online_softmax_step(m_ref, l_ref, acc_ref, lgn, lambda p: p * vnew_ref[0].astype(F32))
        o = acc_ref[...] / l_ref[...]
        row = lax.broadcasted_iota(I32, (heads, wa), 0)
        lane = lax.broadcasted_iota(I32, (heads, wa), 1)
        o_ref[0] = jnp.sum(jnp.where(lane // da == row, o, 0.0), axis=0, keepdims=True).astype(o_ref.dtype)


def _sample_dsa(dm, qbd, bias, bnew, knew, vnew, kt, vt, page_table, pps):
    bs, n_pages = page_table.shape
    ha, da = dm["HA"], dm["DA"]
    wa = ha * da
    page = kt.shape[-1]
    cfg = dict(pps=pps, sca=da ** -0.5)
    per_b = lambda shape: pl.BlockSpec((1,) + shape, lambda b, j, pt: (b, 0, 0))
    grid_spec = pltpu.PrefetchScalarGridSpec(
        num_scalar_prefetch=1, grid=(bs, n_pages // pps),
        in_specs=[per_b((ha, wa)), pl.BlockSpec((1, 1, pps * page), lambda b, j, pt: (b, 0, j)),
                  per_b((1, LANES)), per_b((1, wa)), per_b((1, wa))]
                 + _page_specs(pps, (ha, da, page), pps) + _page_specs(pps, (ha, da, page), pps),
        out_specs=per_b((1, wa)),
        scratch_shapes=[pltpu.VMEM((ha, 1), F32), pltpu.VMEM((ha, 1), F32), pltpu.VMEM((ha, wa), F32)])
    return pl.pallas_call(
        functools.partial(_sample_dsa_kernel, cfg), grid_spec=grid_spec,
        out_shape=jax.ShapeDtypeStruct((bs, 1, wa), BF16),
        compiler_params=_cparams(("parallel", "arbitrary")), name="sample_dsa",
    )(page_table, qbd, bias, bnew, knew, vnew, *([kt] * pps), *([vt] * pps))


def _sample_mla_kernel(cfg, pt_ref, qabs_ref, qpe_ref, qfull_ref, knew_ref, wuv_ref, *refs):
    pps, lora, scale = cfg["pps"], cfg["KV"], cfg["scale"]
    cpages, ppages = refs[:pps], refs[pps:2 * pps]
    o_ref, m_ref, l_ref, acc_ref = refs[2 * pps:]
    j = pl.program_id(1)
    heads = qabs_ref.shape[1]

    @pl.when(j == 0)
    def _():
        m_ref[...] = jnp.full(m_ref.shape, NEG_INF, F32)
        l_ref[...] = jnp.zeros(l_ref.shape, F32)
        acc_ref[...] = jnp.zeros(acc_ref.shape, F32)

    c_all = jnp.concatenate([r[0] for r in cpages], axis=0).astype(BF16)
    p_all = jnp.concatenate([r[0] for r in ppages], axis=1).astype(BF16)
    lg = (_dot_nt(qabs_ref[0], c_all) + _dot(qpe_ref[0], p_all)) * scale
    _online_softmax_step(m_ref, l_ref, acc_ref, lg, lambda p: _dot(p.astype(BF16), c_all))

    @pl.when(j == pl.num_programs(1) - 1)
    def _():
        kn = knew_ref[0].astype(F32)
        lgn = jnp.sum(qfull_ref[0].astype(F32) * kn, axis=1, keepdims=True) * scale
        _online_softmax_step(m_ref, l_ref, acc_ref, lgn, lambda p: p * kn[:, :lora])
        o = (acc_ref[...] / l_ref[...]).astype(BF16)
        row = lax.broadcasted_iota(I32, (heads, o_ref.shape[-1]), 0)
        out = jnp.zeros((heads, o_ref.shape[-1]), F32)
        for h in range(heads):
            out = out + jnp.where(row == h, _dot(o, wuv_ref[h]), 0.0)
        o_ref[0] = jnp.sum(out, axis=0, keepdims=True).astype(o_ref.dtype)


def _sample_mla(dm, qabs, qpe, qfull, knew, wuv_pad, cache_ckv, kpe_t, page_table, pps):
    bs, n_pages = page_table.shape
    hb, lora, nope, ropeb, bv = dm["HB"], dm["KV"], dm["NOPE"], dm["ROPE"], dm["BV"]
    page = cache_ckv.shape[1]
    cfg = dict(pps=pps, KV=lora, scale=(nope + ropeb) ** -0.5)
    per_b = lambda shape: pl.BlockSpec((1,) + shape, lambda b, j, pt: (b, 0, 0))
    grid_spec = pltpu.PrefetchScalarGridSpec(
        num_scalar_prefetch=1, grid=(bs, n_pages // pps),
        in_specs=[per_b((hb, lora)), per_b((hb, ropeb)), per_b((hb, lora + LANES)), per_b((1, lora + LANES)),
                  pl.BlockSpec(wuv_pad.shape, lambda b, j, pt: (0, 0, 0))]
                 + _page_specs(pps, (page, lora), pps) + _page_specs(pps, (ropeb, page), pps),
        out_specs=per_b((1, hb * bv)),
        scratch_shapes=[pltpu.VMEM((hb, 1), F32), pltpu.VMEM((hb, 1), F32), pltpu.VMEM((hb, lora), F32)])
    return pl.pallas_call(
        functools.partial(_sample_mla_kernel, cfg), grid_spec=grid_spec,
        out_shape=jax.ShapeDtypeStruct((bs, 1, hb * bv), BF16),
        compiler_params=_cparams(("parallel", "arbitrary")), name="sample_mla",
    )(page_table, qabs, qpe, qfull, knew, wuv_pad, *([cache_ckv] * pps), *([kpe_t] * pps))


def _unmask_heads(xm, d):
    per = LANES // d
    return jnp.stack([xm[h][:, (h % per) * d:(h % per + 1) * d] for h in range(xm.shape[0])], axis=1)


def _sample_attention(dm, ps, cache_k, cache_v, cache_idx_k, cache_ckv, cache_kpe, page_table, wuv_pad):
    bs, n_pages = page_table.shape
    page = cache_k.shape[1]
    past = page * n_pages
    ha, da, hi, di = dm["HA"], dm["DA"], dm["HI"], dm["DI"]
    hb, lora, ropeb = dm["HB"], dm["KV"], dm["ROPE"]
    wa = ha * da
    pps_idx, pps_mla, pps_dsa = math.gcd(n_pages, 64), math.gcd(n_pages, 32), math.gcd(n_pages, 16)
    idx_t = jnp.transpose(cache_idx_k, (0, 2, 1))
    kpe_t = jnp.transpose(cache_kpe, (0, 2, 1))
    k_t = jnp.transpose(cache_k, (0, 2, 3, 1))
    v_t = jnp.transpose(cache_v, (0, 2, 3, 1))

    qi = _unmask_heads(ps["qim"], di)
    w_i = ps["wit"][:hi].T
    wcol = jnp.broadcast_to(w_i[:, :, None], (bs, hi, page))
    scores = _sample_scores(dm, qi, wcol, idx_t, page_table, pps_idx)
    bd = _block_diag_ones(hi * di, di, LANES, 1)
    w_pad = jnp.zeros((bs, LANES), F32).at[:, :hi].set(w_i)
    ki_rep = jnp.tile(ps["ki32"][:, :di], (1, hi))
    bias_t, bnew = _sample_select(dm, scores.reshape(1, bs, past), qi.reshape(bs, hi * di).astype(F32),
                                  ki_rep, w_pad, bd, past)

    per = LANES // da
    qbd = jnp.zeros((bs, ha, wa), BF16)
    for h in range(ha):
        g = h // per
        qbd = qbd.at[:, h, g * LANES:(g + 1) * LANES].set(ps["qam"][h])
    oa = _sample_dsa(dm, qbd, bias_t.reshape(bs, 1, past), bnew.reshape(bs, 1, LANES), ps["ka16"].reshape(bs, 1, wa),
                     ps["va16"].reshape(bs, 1, wa), k_t, v_t, page_table, pps_dsa)

    qfull = jnp.transpose(ps["qcat"], (1, 0, 2))
    qpe = _unmask_heads(ps["qcat"][:, :, lora:], ropeb)
    ob = _sample_mla(dm, qfull[:, :, :lora], qpe, qfull, ps["kcat16"].reshape(bs, 1, lora + LANES), wuv_pad,
                     cache_ckv, kpe_t, page_table, pps_mla)
    return oa.reshape(bs, wa), ob.reshape(bs, -1)


def _tile_sizes(t, n_p, n_s):
    return dict(
        proj=min(256, t),
        dsa_q=min(256, t),
        mla_q=min(128, t), mla_k=min(512, t),
        finish=min(256, n_p),
        route=min(LANES, n_p),
        dense_tok=min(512, n_p), dense_exp=1024)


def _model_dims(cache_k, cache_idx_k, cache_ckv, cache_kpe, w_in, g_cq, w_uq, w_uk, w_uv, sub_k1):
    ha, da = cache_k.shape[2], cache_k.shape[3]
    di, lora, ropeb = cache_idx_k.shape[-1], cache_ckv.shape[-1], cache_kpe.shape[-1]
    ql, hb, nope, bv = g_cq.shape[0], w_uq.shape[1], w_uk.shape[1], w_uv.shape[2]
    d = w_in.shape[0]
    rest = w_in.shape[1] - (3 * ha * da + di + ql + lora + ropeb + 2 * d)
    hi = rest // (di + 1)
    assert hi * (di + 1) == rest
    for v in (da, di, ropeb):
        assert LANES % v == 0
    for v in (ha * da, hi * di, hb * nope, hb * ropeb, lora, ql, d):
        assert v % LANES == 0
    return dict(HA=ha, DA=da, HI=hi, DI=di, HB=hb, NOPE=nope, ROPE=ropeb, KV=lora, QL=ql, BV=bv, D=d,
                HP=sub_k1.shape[0], NK=sub_k1.shape[1], DK=2 * sub_k1.shape[2])


def kernel(x_prompt, x_sample, cache_k, cache_v, cache_idx_k, cache_ckv, cache_kpe, page_table, g_attn, w_in, g_qa, g_ka, g_cq, w_uq, g_qb, w_uk, g_ckv, g_kpe, w_uv, w_pa, w_pb, w_o, g_ffn, w_pq, sub_k1, sub_k2, u_tab, v_tab):
    dm = _model_dims(cache_k, cache_idx_k, cache_ckv, cache_kpe, w_in, g_cq, w_uq, w_uk, w_uv, sub_k1)
    dm, consts = _prep_proj_weights(dm, g_attn, w_in, g_qa, g_ka, g_cq, w_uq, g_qb, w_uk, g_ckv, g_kpe)
    b, t, d = x_prompt.shape
    bs, ts, _ = x_sample.shape
    page, n_pages = cache_k.shape[1], page_table.shape[1]
    past = page * n_pages
    ha, da, di, lora, ropeb = dm["HA"], dm["DA"], dm["DI"], dm["KV"], dm["ROPE"]
    hp, nk, dk = dm["HP"], dm["NK"], dm["DK"]

    wuv_pad = _pad_wuv(w_uv)
    tail_w = (w_pa.astype(BF16), w_pb.astype(BF16), w_o.astype(BF16), g_ffn.reshape(1, -1), w_pq.astype(BF16))
    zeros_half = jnp.zeros((hp, nk, dk // 2), F32)
    k1p = jnp.concatenate([sub_k1, zeros_half], axis=-1).astype(BF16)
    k2p = jnp.concatenate([zeros_half, sub_k2], axis=-1).astype(BF16)
    peer_w = (k1p, k2p, u_tab.astype(BF16), v_tab.T.astype(BF16))

    n_p, n_s = b * t, bs * ts
    assert ts == 1
    tiles = _tile_sizes(t, n_p, n_s)

    x_p = x_prompt.reshape(n_p, d)
    pp = _project(dm, consts, x_p, jnp.arange(t, dtype=I32), tiles["proj"])
    oa_p = _dsa_prompt(dm, pp, b, t, tiles["dsa_q"])
    ob_p = _mla_prompt(dm, pp, wuv_pad, b, t, tiles["mla_q"], tiles["mla_k"])
    y_p = _tail(dm, x_p, oa_p, ob_p, pp["ga"], pp["gb"], tail_w, peer_w,
                tiles["finish"], tiles["route"], tiles["dense_tok"], tiles["dense_exp"])

    x_s = x_sample.reshape(n_s, d)
    ps = _project(dm, consts, x_s, jnp.full((n_s,), past, I32), n_s)
    oa_s, ob_s = _sample_attention(dm, ps, cache_k, cache_v, cache_idx_k, cache_ckv, cache_kpe, page_table, wuv_pad)
    y_s = _tail(dm, x_s, oa_s, ob_s, ps["ga"], ps["gb"], tail_w, peer_w, n_s, n_s, n_s, tiles["dense_exp"])

    def kv_outs(p, bb, tt):
        return (p["ka32"].reshape(bb, tt, ha, da), p["va32"].reshape(bb, tt, ha, da),
                p["ki32"][:, :di].reshape(bb, tt, di), p["ckv32"].reshape(bb, tt, lora),
                p["kpe32"][:, :ropeb].reshape(bb, tt, ropeb))

    return (y_p.reshape(b, t, d), y_s.reshape(bs, ts, d)) + kv_outs(pp, b, t) + kv_outs(ps, bs, ts)
```

```python
import functools
import math

import numpy as np
import jax
import jax.numpy as jnp
from jax import lax
from jax.experimental import pallas as pl
from jax.experimental.pallas import tpu as pltpu

F32 = jnp.float32
BF16 = jnp.bfloat16
I32 = jnp.int32

ROPE_THETA = 10000.0
EPS = 1e-6
IDX_TOPK_MAX = 256
PEER_TOPK = 16
LANES = 128
NEG_INF = float("-inf")
VMEM_LIMIT = 56 * 1024 * 1024


def _cparams(sem):
    return pltpu.CompilerParams(dimension_semantics=sem, vmem_limit_bytes=VMEM_LIMIT)


def _dot(a, b):
    return jnp.dot(a, b, preferred_element_type=F32)


def _dot_nt(a, b):
    return lax.dot_general(a, b, (((1,), (1,)), ((), ())), preferred_element_type=F32)


def _dot_hilo(a, b):
    hi = a.astype(BF16)
    lo = (a - hi.astype(F32)).astype(BF16)
    return _dot(hi, b) + _dot(lo, b)


def _rope(x, c, s, half):
    w = x.shape[-1]
    fwd = pltpu.roll(x, half, 1)
    bwd = pltpu.roll(x, w - half, 1)
    lane = lax.broadcasted_iota(I32, (1, w), 1)
    first = (lane & (2 * half - 1)) < half
    return x * c + jnp.where(first, bwd, fwd) * s


def _group_masked(x, h, d):
    per = LANES // d
    g = h // per
    grp = x[:, g * LANES:(g + 1) * LANES]
    lane = lax.broadcasted_iota(I32, (1, LANES), 1)
    return jnp.where((lane // d) == (h % per), grp, 0.0)


def _proj_kernel(dm, x_ref, gat_ref, w_ref, wwi_ref, c64_ref, s64_ref, c32_ref, s32_ref,
                 gqa_ref, gka_ref, gcq_ref, gqbn_ref, gqbr_ref, gckv_ref, gkpe_ref,
                 bda_ref, bnr_ref, brn_ref, bdr_ref, wuq_ref, wuk_ref,
                 ka32, va32, ki32, ckv32, kpe32, ga, gb, ka16, va16, ki16, kcat16,
                 qam, qim, wit, qcat):
    off = dm["off"]
    x = x_ref[...]
    ms = jnp.mean(x * x, axis=-1, keepdims=True)
    h = (x * lax.rsqrt(ms + EPS) * gat_ref[...]).astype(BF16)

    def seg(name, n):
        a = off[name]
        return _dot(h, w_ref[:, a:a + n])

    ha, da, hi, di = dm["HA"], dm["DA"], dm["HI"], dm["DI"]
    hb, nope, ropeb, lora = dm["HB"], dm["NOPE"], dm["ROPE"], dm["KV"]
    wa = ha * da
    c64, s64 = c64_ref[...], s64_ref[...]
    c32, s32 = c32_ref[...], s32_ref[...]

    z = seg("qa", wa)
    q = z * lax.rsqrt(_dot_hilo(z * z, bda_ref[...]) * (1.0 / da) + EPS) * gqa_ref[...]
    q = _rope(q, c64, s64, da // 2)
    for hh in range(ha):
        qam[hh] = _group_masked(q, hh, da).astype(BF16)
    z = seg("ka", wa)
    k = z * lax.rsqrt(_dot_hilo(z * z, bda_ref[...]) * (1.0 / da) + EPS) * gka_ref[...]
    k = _rope(k, c64, s64, da // 2)
    ka32[...] = k
    ka16[...] = k.astype(BF16)
    z = seg("va", wa)
    va32[...] = z
    va16[...] = z.astype(BF16)

    z = seg("qi", hi * di)
    q = _rope(z, c32, s32, di // 2)
    for hh in range(hi):
        qim[hh] = _group_masked(q, hh, di).astype(BF16)
    wit[...] = _dot_nt(wwi_ref[...], h)
    z = seg("ki", LANES)
    k = _rope(z, c32[:, :LANES], s32[:, :LANES], di // 2)
    ki32[...] = k
    ki16[...] = k.astype(BF16)

    z = seg("cq", dm["QL"])
    zn = (z * lax.rsqrt(jnp.mean(z * z, axis=-1, keepdims=True) + EPS) * gcq_ref[...]).astype(BF16)
    qb = _dot(zn, wuq_ref[...])
    wn = hb * nope
    qn, qr = qb[:, :wn], qb[:, wn:]
    zzn, zzr = qn * qn, qr * qr
    totn = _dot_hilo(zzn, bda_ref[...]) + _dot_hilo(zzr, brn_ref[...])
    totr = _dot_hilo(zzn, bnr_ref[...]) + _dot_hilo(zzr, bdr_ref[...])
    inv_d = 1.0 / (nope + ropeb)
    qn = qn * lax.rsqrt(totn * inv_d + EPS) * gqbn_ref[...]
    qr = qr * lax.rsqrt(totr * inv_d + EPS) * gqbr_ref[...]
    qpe = _rope(qr, c32, s32, ropeb // 2)
    qabs = _dot(qn.astype(BF16), wuk_ref[...])
    for hh in range(hb):
        qcat[hh, :, 0:lora] = qabs[:, hh * lora:(hh + 1) * lora].astype(BF16)
        qcat[hh, :, lora:lora + LANES] = _group_masked(qpe, hh, ropeb).astype(BF16)

    z = seg("ckv", lora)
    ckv = z * lax.rsqrt(jnp.mean(z * z, axis=-1, keepdims=True) + EPS) * gckv_ref[...]
    ckv32[...] = ckv
    kcat16[:, 0:lora] = ckv.astype(BF16)
    z = seg("kpe", LANES)
    kp = z * lax.rsqrt(jnp.mean(z * z, axis=-1, keepdims=True) + EPS) * gkpe_ref[...]
    kp = _rope(kp, c32[:, :LANES], s32[:, :LANES], ropeb // 2)
    kpe32[...] = kp
    kcat16[:, lora:lora + LANES] = kp.astype(BF16)

    d = x.shape[-1]
    ga[...] = jax.nn.sigmoid(seg("ga", d))
    gb[...] = jax.nn.sigmoid(seg("gb", d))


def _rope_tables(pos, d, reps):
    half = d // 2
    inv = ROPE_THETA ** (-(jnp.arange(half, dtype=F32) * 2.0 / d))
    ang = pos.astype(F32)[:, None] * inv[None, :]
    cos, sin = jnp.cos(ang), jnp.sin(ang)
    c = jnp.concatenate([cos, cos], axis=-1)
    s = jnp.concatenate([-sin, sin], axis=-1)
    return jnp.tile(c, (1, reps)), jnp.tile(s, (1, reps))


def _block_diag_ones(rows, rd, cols, cd):
    r = np.arange(rows)[:, None] // rd
    c = np.arange(cols)[None, :] // cd
    return jnp.asarray((r == c).astype(np.float32), dtype=BF16)


def _prep_proj_weights(dm, g_attn, w_in, g_qa, g_ka, g_cq, w_uq, g_qb, w_uk, g_ckv, g_kpe):
    ha, da, hi, di = dm["HA"], dm["DA"], dm["HI"], dm["DI"]
    hb, nope, ropeb, lora, ql = dm["HB"], dm["NOPE"], dm["ROPE"], dm["KV"], dm["QL"]
    d = w_in.shape[0]
    sizes = (ha * da, ha * da, ha * da, hi * di, hi, di, ql, lora, ropeb, d, d)
    parts, o = [], 0
    for n in sizes:
        parts.append(w_in[:, o:o + n])
        o += n
    w_qa, w_ka, w_va, w_qi, w_wi, w_ki, w_cq, w_ckv, w_kpe, w_ga, w_gb = parts
    segs = [("qa", w_qa), ("ka", w_ka), ("va", w_va), ("qi", w_qi),
            ("ki", jnp.tile(w_ki, (1, LANES // di))), ("cq", w_cq), ("ckv", w_ckv),
            ("kpe", jnp.tile(w_kpe, (1, LANES // ropeb))), ("ga", w_ga), ("gb", w_gb)]
    off, o = {}, 0
    for name, w in segs:
        off[name] = o
        o += w.shape[1]
    w_all = jnp.concatenate([w for _, w in segs], axis=1).astype(BF16)
    hip = -(-hi // 8) * 8
    wwi = jnp.zeros((hip, d), F32).at[:hi].set(w_wi.T).astype(BF16)
    wuq = jnp.concatenate([w_uq[:, :, :nope].reshape(ql, hb * nope),
                           w_uq[:, :, nope:].reshape(ql, hb * ropeb)], axis=1).astype(BF16)
    wuk = jnp.zeros((hb * nope, hb * lora), F32)
    for hh in range(hb):
        wuk = wuk.at[hh * nope:(hh + 1) * nope, hh * lora:(hh + 1) * lora].set(w_uk[hh])
    wuk = wuk.astype(BF16)
    row = lambda v: v.reshape(1, -1).astype(F32)
    consts = dict(
        gat=row(g_attn), w=w_all, wwi=wwi,
        gqa=row(jnp.tile(g_qa, ha)), gka=row(jnp.tile(g_ka, ha)), gcq=row(g_cq),
        gqbn=row(jnp.tile(g_qb[:nope], hb)), gqbr=row(jnp.tile(g_qb[nope:], hb)),
        gckv=row(g_ckv), gkpe=row(jnp.tile(g_kpe, LANES // ropeb)),
        bda=_block_diag_ones(ha * da, da, ha * da, da),
        bnr=_block_diag_ones(hb * nope, nope, hb * ropeb, ropeb),
        brn=_block_diag_ones(hb * ropeb, ropeb, hb * nope, nope),
        bdr=_block_diag_ones(hb * ropeb, ropeb, hb * ropeb, ropeb),
        wuq=wuq, wuk=wuk)
    dm = dict(dm, off=off, HIP=hip, WTOT=o)
    return dm, consts


def _project(dm, consts, x2d, pos, tm):
    n, d = x2d.shape
    p = pos.shape[0]
    assert n % tm == 0 and p % tm == 0
    npb = p // tm
    ha, da, hi, di = dm["HA"], dm["DA"], dm["HI"], dm["DI"]
    hb, ropeb, lora = dm["HB"], dm["ROPE"], dm["KV"]
    wa = ha * da
    c64, s64 = _rope_tables(pos, da, ha)
    c32, s32 = _rope_tables(pos, di, hi)
    full = lambda a: pl.BlockSpec(a.shape, lambda i: (0,) * a.ndim)
    rows = lambda w: pl.BlockSpec((tm, w), lambda i: (i, 0))
    prow = lambda w: pl.BlockSpec((tm, w), lambda i: (i % npb, 0))
    hrows = lambda hh, w: pl.BlockSpec((hh, tm, w), lambda i: (0, i, 0))
    cn = ["gat", "w", "wwi"]
    cg = ["gqa", "gka", "gcq", "gqbn", "gqbr", "gckv", "gkpe", "bda", "bnr", "brn", "bdr", "wuq", "wuk"]
    ins = [x2d] + [consts[k] for k in cn] + [c64, s64, c32, s32] + [consts[k] for k in cg]
    in_specs = ([rows(d)] + [full(consts[k]) for k in cn] + [prow(wa), prow(wa), prow(hi * di), prow(hi * di)]
                + [full(consts[k]) for k in cg])
    sd = jax.ShapeDtypeStruct
    outs = [
        (sd((n, wa), F32), rows(wa)), (sd((n, wa), F32), rows(wa)), (sd((n, LANES), F32), rows(LANES)),
        (sd((n, lora), F32), rows(lora)), (sd((n, LANES), F32), rows(LANES)),
        (sd((n, d), F32), rows(d)), (sd((n, d), F32), rows(d)),
        (sd((n, wa), BF16), rows(wa)), (sd((n, wa), BF16), rows(wa)), (sd((n, LANES), BF16), rows(LANES)),
        (sd((n, lora + LANES), BF16), rows(lora + LANES)),
        (sd((ha, n, LANES), BF16), hrows(ha, LANES)), (sd((hi, n, LANES), BF16), hrows(hi, LANES)),
        (sd((dm["HIP"], n), F32), pl.BlockSpec((dm["HIP"], tm), lambda i: (0, i))),
        (sd((hb, n, lora + LANES), BF16), hrows(hb, lora + LANES)),
    ]
    res = pl.pallas_call(
        functools.partial(_proj_kernel, dm),
        grid=(n // tm,),
        in_specs=in_specs,
        out_specs=[o[1] for o in outs],
        out_shape=[o[0] for o in outs],
        compiler_params=_cparams(("parallel",)),
        name="proj",
    )(*ins)
    names = ["ka32", "va32", "ki32", "ckv32", "kpe32", "ga", "gb", "ka16", "va16", "ki16", "kcat16",
             "qam", "qim", "wit", "qcat"]
    return dict(zip(names, res))


INT_MIN = -2147483648


def _sortable_key(x):
    bits = pltpu.bitcast(x, I32)
    return bits ^ ((bits >> 31) & 0x7FFFFFFF)


def _dsa_prompt_kernel(cfg, qim_ref, ki_ref, wit_ref, qam_ref, ka_ref, va_ref, o_ref,
                       keys_ref, bias_ref, pmax_ref, m_ref, l_ref, acc_ref):
    tq, topk, hi, ha, da, t_total = cfg["tq"], cfg["topk"], cfg["HI"], cfg["HA"], cfg["DA"], cfg["T"]
    sc1, sc2, sca = cfg["sc1"], cfg["sc2"], cfg["sca"]
    c = tq
    j = pl.program_id(1)
    nch = j + 1
    row_i = lax.broadcasted_iota(I32, (c, tq), 0)
    col_i = lax.broadcasted_iota(I32, (c, tq), 1)
    t_idx = j * tq + col_i

    def chunk(ci):
        return pl.ds(pl.multiple_of(ci * c, c), c)

    def score_body(ci, carry):
        kc = ki_ref[chunk(ci), :]
        acc = jnp.zeros((c, tq), F32)
        for h in range(hi):
            s = _dot_nt(kc, qim_ref[h]) * sc1
            acc = acc + jnp.maximum(s, 0.0) * wit_ref[h:h + 1, :]
        score = acc * sc2
        score = jnp.where(ci * c + row_i <= t_idx, score, NEG_INF)
        keys_ref[chunk(ci), :] = _sortable_key(score)
        return carry

    lax.fori_loop(0, nch, score_body, 0)

    def count(ind):
        def body(ci, acc):
            kk = keys_ref[chunk(ci), :]
            return acc + jnp.sum(ind(kk, ci), axis=0, keepdims=True)
        return lax.fori_loop(0, nch, body, jnp.zeros((1, tq), I32))

    def bit_body(i, prefix):
        cand_u = prefix | lax.shift_left(jnp.int32(1), 31 - i)
        cand_s = cand_u ^ INT_MIN
        cnt = count(lambda kk, ci: jnp.where(kk >= cand_s, 1, 0))
        return jnp.where(cnt >= topk, cand_u, prefix)

    thr = lax.fori_loop(0, 32, bit_body, jnp.zeros((1, tq), I32)) ^ INT_MIN
    n_gt = count(lambda kk, ci: jnp.where(kk > thr, 1, 0))
    n_ge = count(lambda kk, ci: jnp.where(kk >= thr, 1, 0))
    need = topk - n_gt

    pmax_ref[...] = jnp.full((1, tq), t_total, I32)

    @pl.when(jnp.max(n_ge) > topk)
    def _():
        nbits = max(1, (t_total - 1).bit_length())

        def p_body(i, p):
            cand = p | lax.shift_left(jnp.int32(1), nbits - 1 - i)
            f = count(lambda kk, ci: jnp.where(kk == thr, jnp.where(ci * c + row_i < cand, 1, 0), 0))
            return jnp.where(f < need, cand, p)

        pmax_ref[...] = lax.fori_loop(0, nbits, p_body, jnp.zeros((1, tq), I32))

    pmax = pmax_ref[...]

    def bias_body(ci, carry):
        kk = keys_ref[chunk(ci), :]
        s_idx = ci * c + row_i
        sel = jnp.where(kk > thr, 1, jnp.where(kk == thr, jnp.where(s_idx <= pmax, 1, 0), 0))
        ok = jnp.where(s_idx <= t_idx, sel, 0)
        bias_ref[ci] = jnp.where(ok > 0, 0.0, NEG_INF).T
        return carry

    lax.fori_loop(0, nch, bias_body, 0)

    per = LANES // da
    lane = lax.broadcasted_iota(I32, (1, LANES), 1)
    m_ref[...] = jnp.full(m_ref.shape, NEG_INF, F32)
    l_ref[...] = jnp.zeros(l_ref.shape, F32)
    acc_ref[...] = jnp.zeros(acc_ref.shape, F32)

    def att_body(ci, carry):
        bias = bias_ref[ci]
        for h in range(ha):
            g = h // per
            kc = ka_ref[chunk(ci), g * LANES:(g + 1) * LANES]
            vc = va_ref[chunk(ci), g * LANES:(g + 1) * LANES]
            lg = _dot_nt(qam_ref[h], kc) * sca + bias
            m_prev = m_ref[h]
            m_new = jnp.maximum(m_prev, jnp.max(lg, axis=1, keepdims=True))
            m_safe = jnp.where(m_new == NEG_INF, 0.0, m_new)
            p = jnp.exp(lg - jnp.tile(m_safe, (1, c // LANES)))
            alpha = jnp.exp(m_prev - m_safe)
            l_ref[h] = alpha * l_ref[h] + jnp.sum(p, axis=1, keepdims=True)
            acc_ref[h] = alpha * acc_ref[h] + _dot(p.astype(BF16), vc)
            m_ref[h] = m_new
        return carry

    lax.fori_loop(0, nch, att_body, 0)
    for g in range(ha // per):
        out_g = jnp.zeros((tq, LANES), F32)
        for hh in range(per):
            h = g * per + hh
            out_g = jnp.where((lane // da) == hh, acc_ref[h] / l_ref[h], out_g)
        o_ref[:, g * LANES:(g + 1) * LANES] = out_g.astype(o_ref.dtype)


def _dsa_prompt(dm, pr, b, t, tq):
    hi, ha, da, di = dm["HI"], dm["HA"], dm["DA"], dm["DI"]
    wa = ha * da
    topk = min(IDX_TOPK_MAX, t // 4)
    assert t % tq == 0 and topk <= tq and tq % LANES == 0
    nq = t // tq
    cfg = dict(tq=tq, topk=topk, HI=hi, HA=ha, DA=da, T=t,
               sc1=di ** -0.5, sc2=hi ** -0.5, sca=da ** -0.5)
    return pl.pallas_call(
        functools.partial(_dsa_prompt_kernel, cfg),
        grid=(b, nq),
        in_specs=[
            pl.BlockSpec((hi, tq, LANES), lambda bi, j: (0, bi * nq + j, 0)),
            pl.BlockSpec((t, LANES), lambda bi, j: (bi, 0)),
            pl.BlockSpec((dm["HIP"], tq), lambda bi, j: (0, bi * nq + j)),
            pl.BlockSpec((ha, tq, LANES), lambda bi, j: (0, bi * nq + j, 0)),
            pl.BlockSpec((t, wa), lambda bi, j: (bi, 0)),
            pl.BlockSpec((t, wa), lambda bi, j: (bi, 0)),
        ],
        out_specs=pl.BlockSpec((tq, wa), lambda bi, j: (bi * nq + j, 0)),
        out_shape=jax.ShapeDtypeStruct((b * t, wa), BF16),
        scratch_shapes=[pltpu.VMEM((t, tq), I32), pltpu.VMEM((nq, tq, tq), F32), pltpu.VMEM((1, tq), I32),
                        pltpu.VMEM((ha, tq, LANES), F32), pltpu.VMEM((ha, tq, LANES), F32),
                        pltpu.VMEM((ha, tq, LANES), F32)],
        compiler_params=_cparams(("parallel", "arbitrary")),
        name="dsa_prompt",
    )(pr["qim"], pr["ki16"], pr["wit"], pr["qam"], pr["ka16"], pr["va16"])


def _mla_prompt_kernel(cfg, q_ref, k_ref, wuv_ref, o_ref, m_ref, l_ref, a_ref, acc_ref, p_ref):
    tq, ck, lora, scale = cfg["tq"], cfg["ck"], cfg["KV"], cfg["scale"]
    hb, _, kc = q_ref.shape
    rows = hb * tq
    j = pl.program_id(1)
    q = q_ref[...].reshape(rows, kc)
    m_ref[...] = jnp.full(m_ref.shape, NEG_INF, F32)
    l_ref[...] = jnp.zeros(l_ref.shape, F32)
    acc_ref[...] = jnp.zeros(acc_ref.shape, F32)
    t_idx = j * tq + lax.broadcasted_iota(I32, (tq, 1), 0)
    col = lax.broadcasted_iota(I32, (1, ck), 1)
    nch = ((j + 1) * tq + ck - 1) // ck

    def body(ci, carry):
        kk = k_ref[pl.ds(pl.multiple_of(ci * ck, ck), ck), :]
        s = _dot_nt(q, kk) * scale
        causal = ci * ck + col <= t_idx
        for h in range(hb):
            r = slice(h * tq, (h + 1) * tq)
            sh = jnp.where(causal, s[r], NEG_INF)
            m_prev = m_ref[r]
            m_new = jnp.maximum(m_prev, jnp.max(sh, axis=1, keepdims=True))
            p = jnp.exp(sh - jnp.tile(m_new, (1, ck // LANES)))
            alpha = jnp.exp(m_prev - m_new)
            l_ref[r] = alpha * l_ref[r] + jnp.sum(p, axis=1, keepdims=True)
            p_ref[r] = p.astype(BF16)
            a_ref[r] = alpha
            m_ref[r] = m_new
        pv = _dot(p_ref[...], kk[:, :lora])
        for h in range(hb):
            r = slice(h * tq, (h + 1) * tq)
            acc_ref[r] = acc_ref[r] * jnp.tile(a_ref[r], (1, lora // LANES)) + pv[r]
        return carry

    lax.fori_loop(0, nch, body, 0)
    out = None
    for h in range(hb):
        r = slice(h * tq, (h + 1) * tq)
        o = (acc_ref[r] / jnp.tile(l_ref[r], (1, lora // LANES))).astype(BF16)
        out = _dot(o, wuv_ref[h]) if out is None else out + _dot(o, wuv_ref[h])
    o_ref[...] = out.astype(o_ref.dtype)


def _pad_wuv(w_uv):
    hb, lora, bv = w_uv.shape
    w = jnp.zeros((hb, lora, hb * bv), F32)
    for h in range(hb):
        w = w.at[h, :, h * bv:(h + 1) * bv].set(w_uv[h])
    return w.astype(BF16)


def _mla_prompt(dm, pr, wuv_pad, b, t, tq, ck):
    hb, lora, nope, ropeb, bv = dm["HB"], dm["KV"], dm["NOPE"], dm["ROPE"], dm["BV"]
    kc = lora + LANES
    assert t % tq == 0 and t % ck == 0 and ck % LANES == 0
    nq = t // tq
    cfg = dict(tq=tq, ck=ck, KV=lora, scale=(nope + ropeb) ** -0.5)
    rows = hb * tq
    return pl.pallas_call(
        functools.partial(_mla_prompt_kernel, cfg),
        grid=(b, nq),
        in_specs=[
            pl.BlockSpec((hb, tq, kc), lambda bi, j: (0, bi * nq + j, 0)),
            pl.BlockSpec((t, kc), lambda bi, j: (bi, 0)),
            pl.BlockSpec(wuv_pad.shape, lambda bi, j: (0, 0, 0)),
        ],
        out_specs=pl.BlockSpec((tq, hb * bv), lambda bi, j: (bi * nq + j, 0)),
        out_shape=jax.ShapeDtypeStruct((b * t, hb * bv), BF16),
        scratch_shapes=[pltpu.VMEM((rows, LANES), F32), pltpu.VMEM((rows, LANES), F32), pltpu.VMEM((rows, LANES), F32),
                        pltpu.VMEM((rows, lora), F32), pltpu.VMEM((rows, ck), BF16)],
        compiler_params=_cparams(("parallel", "arbitrary")),
        name="mla_prompt",
    )(pr["qcat"], pr["kcat16"], wuv_pad)


def _finish_kernel(x_ref, oa_ref, ob_ref, ga_ref, gb_ref, wpa_ref, wpb_ref, wo_ref, gf_ref, wpq_ref,
                   x1_ref, x1n_ref, q_ref):
    merged = ga_ref[...] * _dot(oa_ref[...], wpa_ref[...]) + gb_ref[...] * _dot(ob_ref[...], wpb_ref[...])
    x1 = x_ref[...] + _dot(merged.astype(BF16), wo_ref[...])
    x1_ref[...] = x1
    ms = jnp.mean(x1 * x1, axis=-1, keepdims=True)
    x1n = (x1 * lax.rsqrt(ms + EPS) * gf_ref[...]).astype(BF16)
    x1n_ref[...] = x1n
    q_ref[...] = _dot(x1n, wpq_ref[...]).astype(q_ref.dtype)


def _finish(x2d, oa, ob, ga, gb, wts, tm):
    n, d = x2d.shape
    assert n % tm == 0
    rows = lambda w: pl.BlockSpec((tm, w), lambda i: (i, 0))
    full = lambda a: pl.BlockSpec(a.shape, lambda i: (0,) * a.ndim)
    wpa, wpb, wo, gf, wpq = wts
    dq = wpq.shape[1]
    return pl.pallas_call(
        _finish_kernel,
        grid=(n // tm,),
        in_specs=[rows(d), rows(oa.shape[1]), rows(ob.shape[1]), rows(d), rows(d),
                  full(wpa), full(wpb), full(wo), full(gf), full(wpq)],
        out_specs=[rows(d), rows(d), rows(dq)],
        out_shape=[jax.ShapeDtypeStruct((n, d), F32), jax.ShapeDtypeStruct((n, d), BF16),
                   jax.ShapeDtypeStruct((n, dq), BF16)],
        compiler_params=_cparams(("parallel",)),
        name="finish",
    )(x2d, oa, ob, ga, gb, wpa, wpb, wo, gf, wpq)


def _oddeven_merge(lo, hi, r):
    step = r * 2
    if step < hi - lo:
        yield from _oddeven_merge(lo, hi, step)
        yield from _oddeven_merge(lo + r, hi, step)
        yield from [(i, i + r) for i in range(lo + r, hi - r, step)]
    else:
        yield (lo, lo + r)


def _oddeven_merge_sort(lo, hi):
    if hi - lo >= 1:
        mid = lo + (hi - lo) // 2
        yield from _oddeven_merge_sort(lo, mid)
        yield from _oddeven_merge_sort(mid + 1, hi)
        yield from _oddeven_merge(lo, hi, 1)


SORT16 = tuple(_oddeven_merge_sort(0, PEER_TOPK - 1))
BITONIC16 = tuple((i, i + k) for k in (8, 4, 2, 1) for i in range(PEER_TOPK) if (i & k) == 0)
SUBLANES = 8


def _compare_exchange(x, net):
    for i, j in net:
        x[i], x[j] = jnp.maximum(x[i], x[j]), jnp.minimum(x[i], x[j])
    return x


def _merge_top16(a, b_rev):
    return _compare_exchange([jnp.maximum(x, y) for x, y in zip(a, b_rev)], BITONIC16)


def _top16_sorted(s):
    x = _compare_exchange([s[i * SUBLANES:(i + 1) * SUBLANES, :] for i in range(PEER_TOPK)], SORT16)
    for shift in (4, 6, 7):
        x = _merge_top16(x, [pltpu.roll(x[PEER_TOPK - 1 - i], shift, 0) for i in range(PEER_TOPK)])
    return x


def _peer_route_kernel(cfg, q_ref, k1_ref, k2_ref, s1_ref, s2_ref, e1_ref, e2_ref):
    hp, dk = cfg["HP"], cfg["DK"]
    tn = q_ref.shape[0]
    sub = lax.broadcasted_iota(I32, (SUBLANES, tn), 0)
    v1p = [jnp.zeros((SUBLANES, tn), F32)] * PEER_TOPK
    v2p = [jnp.zeros((SUBLANES, tn), F32)] * PEER_TOPK
    for h in range(hp):
        qh = q_ref[:, h * dk:(h + 1) * dk]
        s1 = _dot_nt(k1_ref[h], qh)
        s2 = _dot_nt(k2_ref[h], qh)
        s1_ref[0, h] = s1
        s2_ref[0, h] = s2
        t1 = _top16_sorted(s1)
        t2 = _top16_sorted(s2)
        v1p = [jnp.where(sub == h, a, b) for a, b in zip(t1, v1p)]
        v2p = [jnp.where(sub == h, a, b) for a, b in zip(t2, v2p)]
    cur = [v1p[0] + v2p[r2] for r2 in range(PEER_TOPK)]
    for r1 in range(1, PEER_TOPK):
        lim = PEER_TOPK // (r1 + 1)
        b_rev = [v1p[r1] + v2p[PEER_TOPK - 1 - i] if i >= PEER_TOPK - lim else None for i in range(PEER_TOPK)]
        cur = _compare_exchange([a if b is None else jnp.maximum(a, b) for a, b in zip(cur, b_rev)], BITONIC16)
    thr = cur[PEER_TOPK - 1]
    z = jnp.zeros((SUBLANES, tn), F32)
    for r in range(PEER_TOPK):
        z = z + jnp.exp(cur[r] - cur[0])
    rz = 1.0 / z
    cut = []
    for r1 in range(PEER_TOPK):
        c = jnp.full((SUBLANES, tn), jnp.inf, F32)
        for r2 in range(PEER_TOPK // (r1 + 1)):
            c = jnp.where(v1p[r1] + v2p[r2] >= thr, v2p[r2], c)
        cut.append(c)
    for h in range(hp):
        s1 = s1_ref[0, h]
        e1_ref[0, h] = jnp.exp(s1 - v1p[0][h:h + 1, :]) * rz[h:h + 1, :]
        e2_ref[0, h] = jnp.exp(s2_ref[0, h] - v2p[0][h:h + 1, :])
        c1 = jnp.full(s1.shape, jnp.inf, F32)
        for r1 in range(PEER_TOPK):
            c1 = jnp.where(s1 == v1p[r1][h:h + 1, :], cut[r1][h:h + 1, :], c1)
        s1_ref[0, h] = c1


def _peer_route(dm, q, k1p, k2p, tn):
    n = q.shape[0]
    hp, nk, dk = dm["HP"], dm["NK"], dm["DK"]
    assert n % tn == 0 and nk == PEER_TOPK * SUBLANES and hp <= SUBLANES and dk % LANES == 0
    big = pl.BlockSpec((1, hp, nk, tn), lambda i: (i, 0, 0, 0))
    sds = jax.ShapeDtypeStruct((n // tn, hp, nk, tn), F32)
    return pl.pallas_call(
        functools.partial(_peer_route_kernel, dict(HP=hp, DK=dk)),
        grid=(n // tn,),
        in_specs=[pl.BlockSpec((tn, hp * dk), lambda i: (i, 0)),
                  pl.BlockSpec(k1p.shape, lambda i: (0, 0, 0)), pl.BlockSpec(k2p.shape, lambda i: (0, 0, 0))],
        out_specs=[big, big, big, big],
        out_shape=[sds, sds, sds, sds],
        compiler_params=_cparams(("parallel",)),
        name="peer_route",
    )(q, k1p, k2p)


def _peer_dense_kernel(cfg, x_ref, x1_ref, u_ref, vt_ref, c1_ref, s2_ref, e1_ref, e2_ref,
                       y_ref, acc_ref, act_ref, g_ref):
    hp, nk = cfg["HP"], cfg["NK"]
    eb = u_ref.shape[0]
    nts, tl = c1_ref.shape[0], c1_ref.shape[-1]
    e = pl.program_id(1)

    @pl.when(e == 0)
    def _():
        acc_ref[...] = jnp.zeros(acc_ref.shape, F32)

    a = _dot_nt(u_ref[...], x_ref[...])
    act_ref[...] = 0.5 * a * (1.0 + lax.erf(a * (2.0 ** -0.5)))

    for ts in range(nts):
        lanes = slice(ts * tl, (ts + 1) * tl)

        def key_body(k, carry, ts=ts, lanes=lanes):
            i1 = e * (eb // nk) + k
            g = None
            for h in range(hp):
                w = jnp.where(s2_ref[ts, h] >= c1_ref[ts, h, pl.ds(i1, 1), :],
                              e1_ref[ts, h, pl.ds(i1, 1), :] * e2_ref[ts, h], 0.0)
                g = w if g is None else g + w
            rows = pl.ds(pl.multiple_of(k * nk, nk), nk)
            g_ref[rows, lanes] = (g * act_ref[rows, lanes]).astype(BF16)
            return carry

        lax.fori_loop(0, eb // nk, key_body, 0)
    acc_ref[...] += _dot(vt_ref[...], g_ref[...])

    @pl.when(e == pl.num_programs(1) - 1)
    def _():
        y_ref[...] = x1_ref[...] + acc_ref[...].T


def _peer_dense(dm, x1n, x1, u16, vt16, route, tn, eb):
    n, d = x1.shape
    ne = u16.shape[0]
    hp, nk = dm["HP"], dm["NK"]
    c1, s2, e1, e2 = route
    tl = c1.shape[-1]
    assert n % tn == 0 and ne % eb == 0 and eb % nk == 0 and tn % tl == 0
    nts = tn // tl
    big = pl.BlockSpec((nts, hp, nk, tl), lambda i, e: (i, 0, 0, 0))
    return pl.pallas_call(
        functools.partial(_peer_dense_kernel, dict(HP=hp, NK=nk)),
        grid=(n // tn, ne // eb),
        in_specs=[pl.BlockSpec((tn, d), lambda i, e: (i, 0)), pl.BlockSpec((tn, d), lambda i, e: (i, 0)),
                  pl.BlockSpec((eb, d), lambda i, e: (e, 0)), pl.BlockSpec((d, eb), lambda i, e: (0, e)),
                  big, big, big, big],
        out_specs=pl.BlockSpec((tn, d), lambda i, e: (i, 0)),
        out_shape=jax.ShapeDtypeStruct((n, d), F32),
        scratch_shapes=[pltpu.VMEM((d, tn), F32), pltpu.VMEM((eb, tn), F32), pltpu.VMEM((eb, tn), BF16)],
        compiler_params=_cparams(("parallel", "arbitrary")),
        name="peer_dense",
    )(x1n, x1, u16, vt16, c1, s2, e1, e2)


def _tail(dm, x2d, oa, ob, ga, gb, tail_w, peer_w, tm, tn_route, tn_dense, eb):
    x1, x1n, q = _finish(x2d, oa, ob, ga, gb, tail_w, tm)
    k1p, k2p, u16, vt16 = peer_w
    route = _peer_route(dm, q, k1p, k2p, tn_route)
    return _peer_dense(dm, x1n, x1, u16, vt16, route, tn_dense, eb)


def _page_specs(n, shape, pps):
    zeros = (0,) * len(shape)
    return [pl.BlockSpec((1,) + shape, (lambda b, j, pt, k=k: (pt[b, j * pps + k],) + zeros)) for k in range(n)]


def _sample_scores_kernel(cfg, pt_ref, q_ref, w_ref, *refs):
    pps, sc1, sc2 = cfg["pps"], cfg["sc1"], cfg["sc2"]
    pages, out_ref = refs[:pps], refs[pps]
    kt = jnp.concatenate([r[0] for r in pages], axis=1).astype(BF16)
    s = _dot(q_ref[0], kt) * sc1
    w = jnp.tile(w_ref[0], (1, pps))
    out_ref[0] = jnp.sum(jnp.maximum(s, 0.0) * w, axis=0, keepdims=True) * sc2


def _sample_scores(dm, qi, wcol, idx_t, page_table, pps):
    bs, n_pages = page_table.shape
    hi, di = dm["HI"], dm["DI"]
    page = idx_t.shape[-1]
    cfg = dict(pps=pps, sc1=di ** -0.5, sc2=hi ** -0.5)
    grid_spec = pltpu.PrefetchScalarGridSpec(
        num_scalar_prefetch=1, grid=(bs, n_pages // pps),
        in_specs=[pl.BlockSpec((1, hi, di), lambda b, j, pt: (b, 0, 0)),
                  pl.BlockSpec((1, hi, page), lambda b, j, pt: (b, 0, 0))]
                 + _page_specs(pps, (di, page), pps),
        out_specs=pl.BlockSpec((1, 1, pps * page), lambda b, j, pt: (b, 0, j)))
    return pl.pallas_call(
        functools.partial(_sample_scores_kernel, cfg), grid_spec=grid_spec,
        out_shape=jax.ShapeDtypeStruct((bs, 1, n_pages * page), F32),
        compiler_params=_cparams(("parallel", "arbitrary")), name="sample_scores",
    )(page_table, qi, wcol, *([idx_t] * pps))


def _sample_select_kernel(cfg, s_ref, qi_ref, ki_ref, w_ref, bd_ref, bias_ref, bnew_ref, keys_ref):
    slots, topk, past, sc1, sc2 = cfg["slots"], cfg["topk"], cfg["past"], cfg["sc1"], cfg["sc2"]
    bsz, width = s_ref.shape[1], s_ref.shape[2]
    prod = qi_ref[...] * ki_ref[...].astype(BF16).astype(F32)
    hs = _dot_hilo(prod, bd_ref[...])
    s_new = jnp.sum(jnp.maximum(hs * sc1, 0.0) * w_ref[...], axis=1, keepdims=True) * sc2
    k_new = _sortable_key(s_new)
    for kk in range(slots):
        keys_ref[kk] = _sortable_key(s_ref[kk])
    lane = lax.broadcasted_iota(I32, (bsz, width), 1)

    def count(ind, ind_new):
        tot = ind_new
        for kk in range(slots):
            tot = tot + jnp.sum(ind(keys_ref[kk], kk), axis=1, keepdims=True)
        return tot

    def bit_body(i, prefix):
        cand_u = prefix | lax.shift_left(jnp.int32(1), 31 - i)
        cand_s = cand_u ^ INT_MIN
        cnt = count(lambda k, kk: jnp.where(k >= cand_s, 1.0, 0.0), jnp.where(k_new >= cand_s, 1.0, 0.0))
        return jnp.where(cnt >= topk, cand_u, prefix)

    thr = lax.fori_loop(0, 32, bit_body, jnp.zeros((bsz, 1), I32)) ^ INT_MIN
    n_gt = count(lambda k, kk: jnp.where(k > thr, 1.0, 0.0), jnp.where(k_new > thr, 1.0, 0.0))
    need = topk - n_gt
    nbits = max(1, past.bit_length())

    def p_body(i, p):
        cand = p | lax.shift_left(jnp.int32(1), nbits - 1 - i)
        f = count(lambda k, kk: jnp.where(k == thr, jnp.where(lane * slots + kk < cand, 1.0, 0.0), 0.0),
                  jnp.where(k_new == thr, jnp.where(past < cand, 1.0, 0.0), 0.0))
        return jnp.where(f < need, cand, p)

    pmax = lax.fori_loop(0, nbits, p_body, jnp.zeros((bsz, 1), I32))
    for kk in range(slots):
        k = keys_ref[kk]
        sel = jnp.where(k > thr, 1, jnp.where(k == thr, jnp.where(lane * slots + kk <= pmax, 1, 0), 0))
        bias_ref[kk] = jnp.where(sel > 0, 0.0, NEG_INF)
    sel_new = jnp.where(k_new > thr, 1, jnp.where(k_new == thr, jnp.where(past <= pmax, 1, 0), 0))
    bnew_ref[...] = jnp.broadcast_to(jnp.where(sel_new > 0, 0.0, NEG_INF), bnew_ref.shape)


def _sample_select(dm, scores_t, qi_flat, ki_rep, w_pad, bd, past):
    slots, bs, width = scores_t.shape
    hi, di = dm["HI"], dm["DI"]
    topk = min(IDX_TOPK_MAX, (past + 1) // 4)
    cfg = dict(slots=slots, topk=topk, past=past, sc1=di ** -0.5, sc2=hi ** -0.5)
    return pl.pallas_call(
        functools.partial(_sample_select_kernel, cfg),
        out_shape=[jax.ShapeDtypeStruct((slots, bs, width), F32), jax.ShapeDtypeStruct((bs, LANES), F32)],
        scratch_shapes=[pltpu.VMEM((slots, bs, width), I32)],
        compiler_params=pltpu.CompilerParams(vmem_limit_bytes=VMEM_LIMIT), name="sample_select",
    )(scores_t, qi_flat, ki_rep, w_pad, bd)


def _online_softmax_step(m_ref, l_ref, acc_ref, lg, pv_fn):
    m_prev = m_ref[...]
    m_new = jnp.maximum(m_prev, jnp.max(lg, axis=1, keepdims=True))
    m_safe = jnp.where(m_new == NEG_INF, 0.0, m_new)
    p = jnp.exp(lg - m_safe)
    alpha = jnp.exp(m_prev - m_safe)
    l_ref[...] = alpha * l_ref[...] + jnp.sum(p, axis=1, keepdims=True)
    acc_ref[...] = alpha * acc_ref[...] + pv_fn(p)
    m_ref[...] = m_new


def _sample_dsa_kernel(cfg, pt_ref, q_ref, bias_ref, bnew_ref, knew_ref, vnew_ref, *refs):
    pps, sca = cfg["pps"], cfg["sca"]
    kpages, vpages = refs[:pps], refs[pps:2 * pps]
    o_ref, m_ref, l_ref, acc_ref = refs[2 * pps:]
    j = pl.program_id(1)
    heads, wa = q_ref.shape[1], q_ref.shape[2]
    da = wa // heads
    page = kpages[0].shape[-1]

    @pl.when(j == 0)
    def _():
        m_ref[...] = jnp.full(m_ref.shape, NEG_INF, F32)
        l_ref[...] = jnp.zeros(l_ref.shape, F32)
        acc_ref[...] = jnp.zeros(acc_ref.shape, F32)

    q = q_ref[0]
    kt = jnp.concatenate([r[0].reshape(wa, page) for r in kpages], axis=1).astype(BF16)
    vt = jnp.concatenate([r[0].reshape(wa, page) for r in vpages], axis=1).astype(BF16)
    lg = _dot(q, kt) * sca + bias_ref[0]
    _online_softmax_step(m_ref, l_ref, acc_ref, lg, lambda p: _dot_nt(p.astype(BF16), vt))

    @pl.when(j == pl.num_programs(1) - 1)
    def _():
        lgn = jnp.sum(q.astype(F32) * knew_ref[0].astype(F32), axis=1, keepdims=True) * sca + bnew_ref[0][:, 0:1]
        _online_softmax_step(m_ref, l_ref, acc_ref, lgn, lambda p: p * vnew_ref[0].astype(F32))
        o = acc_ref[...] / l_ref[...]
        row = lax.broadcasted_iota(I32, (heads, wa), 0)
        lane = lax.broadcasted_iota(I32, (heads, wa), 1)
        o_ref[0] = jnp.sum(jnp.where(lane // da == row, o, 0.0), axis=0, keepdims=True).astype(o_ref.dtype)


def _sample_dsa(dm, qbd, bias, bnew, knew, vnew, kt, vt, page_table, pps):
    bs, n_pages = page_table.shape
    ha, da = dm["HA"], dm["DA"]
    wa = ha * da
    page = kt.shape[-1]
    cfg = dict(pps=pps, sca=da ** -0.5)
    per_b = lambda shape: pl.BlockSpec((1,) + shape, lambda b, j, pt: (b, 0, 0))
    grid_spec = pltpu.PrefetchScalarGridSpec(
        num_scalar_prefetch=1, grid=(bs, n_pages // pps),
        in_specs=[per_b((ha, wa)), pl.BlockSpec((1, 1, pps * page), lambda b, j, pt: (b, 0, j)),
                  per_b((1, LANES)), per_b((1, wa)), per_b((1, wa))]
                 + _page_specs(pps, (ha, da, page), pps) + _page_specs(pps, (ha, da, page), pps),
        out_specs=per_b((1, wa)),
        scratch_shapes=[pltpu.VMEM((ha, 1), F32), pltpu.VMEM((ha, 1), F32), pltpu.VMEM((ha, wa), F32)])
    return pl.pallas_call(
        functools.partial(_sample_dsa_kernel, cfg), grid_spec=grid_spec,
        out_shape=jax.ShapeDtypeStruct((bs, 1, wa), BF16),
        compiler_params=_cparams(("parallel", "arbitrary")), name="sample_dsa",
    )(page_table, qbd, bias, bnew, knew, vnew, *([kt] * pps), *([vt] * pps))


def _sample_mla_kernel(cfg, pt_ref, qabs_ref, qpe_ref, qfull_ref, knew_ref, wuv_ref, *refs):
    pps, lora, scale = cfg["pps"], cfg["KV"], cfg["scale"]
    cpages, ppages = refs[:pps], refs[pps:2 * pps]
    o_ref, m_ref, l_ref, acc_ref = refs[2 * pps:]
    j = pl.program_id(1)
    heads = qabs_ref.shape[1]

    @pl.when(j == 0)
    def _():
        m_ref[...] = jnp.full(m_ref.shape, NEG_INF, F32)
        l_ref[...] = jnp.zeros(l_ref.shape, F32)
        acc_ref[...] = jnp.zeros(acc_ref.shape, F32)

    c_all = jnp.concatenate([r[0] for r in cpages], axis=0).astype(BF16)
    p_all = jnp.concatenate([r[0] for r in ppages], axis=1).astype(BF16)
    lg = (_dot_nt(qabs_ref[0], c_all) + _dot(qpe_ref[0], p_all)) * scale
    _online_softmax_step(m_ref, l_ref, acc_ref, lg, lambda p: _dot(p.astype(BF16), c_all))

    @pl.when(j == pl.num_programs(1) - 1)
    def _():
        kn = knew_ref[0].astype(F32)
        lgn = jnp.sum(qfull_ref[0].astype(F32) * kn, axis=1, keepdims=True) * scale
        _online_softmax_step(m_ref, l_ref, acc_ref, lgn, lambda p: p * kn[:, :lora])
        o = (acc_ref[...] / l_ref[...]).astype(BF16)
        row = lax.broadcasted_iota(I32, (heads, o_ref.shape[-1]), 0)
        out = jnp.zeros((heads, o_ref.shape[-1]), F32)
        for h in range(heads):
            out = out + jnp.where(row == h, _dot(o, wuv_ref[h]), 0.0)
        o_ref[0] = jnp.sum(out, axis=0, keepdims=True).astype(o_ref.dtype)


def _sample_mla(dm, qabs, qpe, qfull, knew, wuv_pad, cache_ckv, kpe_t, page_table, pps):
    bs, n_pages = page_table.shape
    hb, lora, nope, ropeb, bv = dm["HB"], dm["KV"], dm["NOPE"], dm["ROPE"], dm["BV"]
    page = cache_ckv.shape[1]
    cfg = dict(pps=pps, KV=lora, scale=(nope + ropeb) ** -0.5)
    per_b = lambda shape: pl.BlockSpec((1,) + shape, lambda b, j, pt: (b, 0, 0))
    grid_spec = pltpu.PrefetchScalarGridSpec(
        num_scalar_prefetch=1, grid=(bs, n_pages // pps),
        in_specs=[per_b((hb, lora)), per_b((hb, ropeb)), per_b((hb, lora + LANES)), per_b((1, lora + LANES)),
                  pl.BlockSpec(wuv_pad.shape, lambda b, j, pt: (0, 0, 0))]
                 + _page_specs(pps, (page, lora), pps) + _page_specs(pps, (ropeb, page), pps),
        out_specs=per_b((1, hb * bv)),
        scratch_shapes=[pltpu.VMEM((hb, 1), F32), pltpu.VMEM((hb, 1), F32), pltpu.VMEM((hb, lora), F32)])
    return pl.pallas_call(
        functools.partial(_sample_mla_kernel, cfg), grid_spec=grid_spec,
        out_shape=jax.ShapeDtypeStruct((bs, 1, hb * bv), BF16),
        compiler_params=_cparams(("parallel", "arbitrary")), name="sample_mla",
    )(page_table, qabs, qpe, qfull, knew, wuv_pad, *([cache_ckv] * pps), *([kpe_t] * pps))


def _unmask_heads(xm, d):
    per = LANES // d
    return jnp.stack([xm[h][:, (h % per) * d:(h % per + 1) * d] for h in range(xm.shape[0])], axis=1)


def _sample_attention(dm, ps, cache_k, cache_v, cache_idx_k, cache_ckv, cache_kpe, page_table, wuv_pad):
    bs, n_pages = page_table.shape
    page = cache_k.shape[1]
    past = page * n_pages
    ha, da, hi, di = dm["HA"], dm["DA"], dm["HI"], dm["DI"]
    hb, lora, ropeb = dm["HB"], dm["KV"], dm["ROPE"]
    wa = ha * da
    pps_idx, pps_mla, pps_dsa = math.gcd(n_pages, 64), math.gcd(n_pages, 32), math.gcd(n_pages, 16)
    idx_t = jnp.transpose(cache_idx_k, (0, 2, 1))
    kpe_t = jnp.transpose(cache_kpe, (0, 2, 1))
    k_t = jnp.transpose(cache_k, (0, 2, 3, 1))
    v_t = jnp.transpose(cache_v, (0, 2, 3, 1))

    qi = _unmask_heads(ps["qim"], di)
    w_i = ps["wit"][:hi].T
    wcol = jnp.broadcast_to(w_i[:, :, None], (bs, hi, page))
    scores = _sample_scores(dm, qi, wcol, idx_t, page_table, pps_idx)
    bd = _block_diag_ones(hi * di, di, LANES, 1)
    w_pad = jnp.zeros((bs, LANES), F32).at[:, :hi].set(w_i)
    ki_rep = jnp.tile(ps["ki32"][:, :di], (1, hi))
    bias_t, bnew = _sample_select(dm, scores.reshape(1, bs, past), qi.reshape(bs, hi * di).astype(F32),
                                  ki_rep, w_pad, bd, past)

    per = LANES // da
    qbd = jnp.zeros((bs, ha, wa), BF16)
    for h in range(ha):
        g = h // per
        qbd = qbd.at[:, h, g * LANES:(g + 1) * LANES].set(ps["qam"][h])
    oa = _sample_dsa(dm, qbd, bias_t.reshape(bs, 1, past), bnew.reshape(bs, 1, LANES), ps["ka16"].reshape(bs, 1, wa),
                     ps["va16"].reshape(bs, 1, wa), k_t, v_t, page_table, pps_dsa)

    qfull = jnp.transpose(ps["qcat"], (1, 0, 2))
    qpe = _unmask_heads(ps["qcat"][:, :, lora:], ropeb)
    ob = _sample_mla(dm, qfull[:, :, :lora], qpe, qfull, ps["kcat16"].reshape(bs, 1, lora + LANES), wuv_pad,
                     cache_ckv, kpe_t, page_table, pps_mla)
    return oa.reshape(bs, wa), ob.reshape(bs, -1)


def _tile_sizes(t, n_p, n_s):
    return dict(
        proj=min(256, t),
        dsa_q=min(256, t),
        mla_q=min(128, t), mla_k=min(512, t),
        finish=min(256, n_p),
        route=min(LANES, n_p),
        dense_tok=min(512, n_p), dense_exp=2048)


def _model_dims(cache_k, cache_idx_k, cache_ckv, cache_kpe, w_in, g_cq, w_uq, w_uk, w_uv, sub_k1):
    ha, da = cache_k.shape[2], cache_k.shape[3]
    di, lora, ropeb = cache_idx_k.shape[-1], cache_ckv.shape[-1], cache_kpe.shape[-1]
    ql, hb, nope, bv = g_cq.shape[0], w_uq.shape[1], w_uk.shape[1], w_uv.shape[2]
    d = w_in.shape[0]
    rest = w_in.shape[1] - (3 * ha * da + di + ql + lora + ropeb + 2 * d)
    hi = rest // (di + 1)
    assert hi * (di + 1) == rest
    for v in (da, di, ropeb):
        assert LANES % v == 0
    for v in (ha * da, hi * di, hb * nope, hb * ropeb, lora, ql, d):
        assert v % LANES == 0
    return dict(HA=ha, DA=da, HI=hi, DI=di, HB=hb, NOPE=nope, ROPE=ropeb, KV=lora, QL=ql, BV=bv, D=d,
                HP=sub_k1.shape[0], NK=sub_k1.shape[1], DK=2 * sub_k1.shape[2])


def kernel(x_prompt, x_sample, cache_k, cache_v, cache_idx_k, cache_ckv, cache_kpe, page_table, g_attn, w_in, g_qa, g_ka, g_cq, w_uq, g_qb, w_uk, g_ckv, g_kpe, w_uv, w_pa, w_pb, w_o, g_ffn, w_pq, sub_k1, sub_k2, u_tab, v_tab):
    dm = _model_dims(cache_k, cache_idx_k, cache_ckv, cache_kpe, w_in, g_cq, w_uq, w_uk, w_uv, sub_k1)
    dm, consts = _prep_proj_weights(dm, g_attn, w_in, g_qa, g_ka, g_cq, w_uq, g_qb, w_uk, g_ckv, g_kpe)
    b, t, d = x_prompt.shape
    bs, ts, _ = x_sample.shape
    page, n_pages = cache_k.shape[1], page_table.shape[1]
    past = page * n_pages
    ha, da, di, lora, ropeb = dm["HA"], dm["DA"], dm["DI"], dm["KV"], dm["ROPE"]
    hp, nk, dk = dm["HP"], dm["NK"], dm["DK"]

    wuv_pad = _pad_wuv(w_uv)
    tail_w = (w_pa.astype(BF16), w_pb.astype(BF16), w_o.astype(BF16), g_ffn.reshape(1, -1), w_pq.astype(BF16))
    zeros_half = jnp.zeros((hp, nk, dk // 2), F32)
    k1p = jnp.concatenate([sub_k1, zeros_half], axis=-1).astype(BF16)
    k2p = jnp.concatenate([zeros_half, sub_k2], axis=-1).astype(BF16)
    peer_w = (k1p, k2p, u_tab.astype(BF16), v_tab.T.astype(BF16))

    n_p, n_s = b * t, bs * ts
    assert ts == 1
    tiles = _tile_sizes(t, n_p, n_s)

    x_p = x_prompt.reshape(n_p, d)
    pp = _project(dm, consts, x_p, jnp.arange(t, dtype=I32), tiles["proj"])
    oa_p = _dsa_prompt(dm, pp, b, t, tiles["dsa_q"])
    ob_p = _mla_prompt(dm, pp, wuv_pad, b, t, tiles["mla_q"], tiles["mla_k"])
    y_p = _tail(dm, x_p, oa_p, ob_p, pp["ga"], pp["gb"], tail_w, peer_w,
                tiles["finish"], tiles["route"], tiles["dense_tok"], tiles["dense_exp"])

    x_s = x_sample.reshape(n_s, d)
    ps = _project(dm, consts, x_s, jnp.full((n_s,), past, I32), n_s)
    oa_s, ob_s = _sample_attention(dm, ps, cache_k, cache_v, cache_idx_k, cache_ckv, cache_kpe, page_table, wuv_pad)
    y_s = _tail(dm, x_s, oa_s, ob_s, ps["ga"], ps["gb"], tail_w, peer_w, n_s, n_s, n_s, tiles["dense_exp"])

    def kv_outs(p, bb, tt):
        return (p["ka32"].reshape(bb, tt, ha, da), p["va32"].reshape(bb, tt, ha, da),
                p["ki32"][:, :di].reshape(bb, tt, di), p["ckv32"].reshape(bb, tt, lora),
                p["kpe32"][:, :ropeb].reshape(bb, tt, ropeb))

    return (y_p.reshape(b, t, d), y_s.reshape(bs, ts, d)) + kv_outs(pp, b, t) + kv_outs(ps, bs, ts)
```
